```python
import math
import jax, jax.numpy as jnp
from jax import lax
import numpy as np

D_MODEL = 1024
BATCH = 4
SEQ = 4096
DEPTH = 4

N_A = DEPTH // 2
N_B = DEPTH - N_A
CONV_WIDTH = 31
CONV_CH = D_MODEL
N_HEADS = 8
QK_NOPE_DIM = 128
QK_ROPE_DIM = 64
V_HEAD_DIM = 128
Q_LORA_RANK = 384
KV_LORA_RANK = 256
ROPE_THETA = 10000.0
Q_BLOCK = 128
ATTN_SCALE = 1.0 / math.sqrt(QK_NOPE_DIM + QK_ROPE_DIM)
N_EXPERTS = 16
N_GROUPS = 4
EXPERTS_PER_GROUP = N_EXPERTS // N_GROUPS
TOP_K = 2
D_EXPERT = 512
ROW_BLOCK = 128
EPS = 1e-6
NEG_INF = -1e30

kernel_name = "yoco_conformer_mla_grouped_moe_adaln"


def rms_norm(x, g):
    xf = x.astype(jnp.float32)
    y = xf * lax.rsqrt(jnp.mean(xf * xf, axis=-1, keepdims=True) + EPS)
    return (y * g.astype(jnp.float32)).astype(x.dtype)


def layer_norm(x, g, b):
    xf = x.astype(jnp.float32)
    mu = jnp.mean(xf, axis=-1, keepdims=True)
    var = jnp.mean(jnp.square(xf - mu), axis=-1, keepdims=True)
    y = (xf - mu) * lax.rsqrt(var + EPS)
    return (y * g.astype(jnp.float32) + b.astype(jnp.float32)).astype(x.dtype)


def modulate(x, g, shift, scale):
    return rms_norm(x, g) * (1 + scale[:, None, :]) + shift[:, None, :]


def rope_tables(positions, dtype):
    inv_freq = ROPE_THETA ** (-jnp.arange(0, QK_ROPE_DIM, 2, dtype=jnp.float32) / QK_ROPE_DIM)
    ang = positions.astype(jnp.float32)[..., None] * inv_freq
    return jnp.cos(ang).astype(dtype), jnp.sin(ang).astype(dtype)


def apply_rope(x, cos, sin):
    x1, x2 = jnp.split(x, 2, axis=-1)
    return jnp.concatenate([x1 * cos - x2 * sin, x1 * sin + x2 * cos], axis=-1)


def conformer_conv(h, w_in, b_in, w_dw, b_dw, ln_g, ln_b, w_out):
    u = h @ w_in + b_in
    a, gate = jnp.split(u, 2, axis=-1)
    u = a * jax.nn.sigmoid(gate)
    u = lax.conv_general_dilated(
        u, w_dw[:, None, :], window_strides=(1,), padding=[(CONV_WIDTH - 1, 0)],
        dimension_numbers=("NWC", "WIO", "NWC"), feature_group_count=CONV_CH) + b_dw
    u = jax.nn.silu(layer_norm(u, ln_g, ln_b))
    return u @ w_out


def shared_kv(x, kv_in_g, w_dkv, kv_norm_g, w_ukv, cos, sin):
    Bsz, S, _ = x.shape
    ckv = rms_norm(x, kv_in_g) @ w_dkv
    latent = rms_norm(ckv[..., :KV_LORA_RANK], kv_norm_g)
    k_rope = apply_rope(ckv[..., KV_LORA_RANK:], cos, sin)
    kv = (latent @ w_ukv).reshape(Bsz, S, N_HEADS, QK_NOPE_DIM + V_HEAD_DIM)
    return kv[..., :QK_NOPE_DIM], k_rope, kv[..., QK_NOPE_DIM:]


def mla_attention(q_nope, q_rope, k_nope, k_rope, v):
    Bsz, S, H, _ = q_nope.shape
    nb = S // Q_BLOCK
    qn = q_nope.reshape(Bsz, nb, Q_BLOCK, H, QK_NOPE_DIM).transpose(1, 0, 2, 3, 4)
    qr = q_rope.reshape(Bsz, nb, Q_BLOCK, H, QK_ROPE_DIM).transpose(1, 0, 2, 3, 4)
    q_pos = jnp.arange(S, dtype=jnp.int32).reshape(nb, Q_BLOCK)
    k_pos = jnp.arange(S, dtype=jnp.int32)

    def one_block(args):
        qn_b, qr_b, qp = args
        s = (jnp.einsum("bqhd,bkhd->bhqk", qn_b, k_nope)
             + jnp.einsum("bqhr,bkr->bhqk", qr_b, k_rope)).astype(jnp.float32) * ATTN_SCALE
        s = jnp.where(k_pos[None, :] <= qp[:, None], s, NEG_INF)
        p = jax.nn.softmax(s, axis=-1).astype(v.dtype)
        return jnp.einsum("bhqk,bkhd->bqhd", p, v)

    o = lax.map(one_block, (qn, qr, q_pos))
    return o.transpose(1, 0, 2, 3, 4).reshape(Bsz, S, H * V_HEAD_DIM)


def route(h2, router_w, router_b):
    T = h2.shape[0]
    s = jax.nn.sigmoid(h2.astype(jnp.float32) @ router_w.astype(jnp.float32))
    sb = (s + router_b.astype(jnp.float32)).reshape(T, N_GROUPS, EXPERTS_PER_GROUP)
    grp_score = lax.top_k(sb, 2)[0].sum(-1)
    g = jnp.argmax(grp_score, axis=-1).astype(jnp.int32)
    within = jnp.take_along_axis(sb, g[:, None, None], axis=1)[:, 0]
    _, loc = lax.top_k(within, TOP_K)
    idx = g[:, None] * EXPERTS_PER_GROUP + loc.astype(jnp.int32)
    w = jnp.take_along_axis(s, idx, axis=1)
    return idx, w / jnp.sum(w, axis=-1, keepdims=True)


def moe_ffn(h, router_w, router_b, w_gate, w_up, w_down):
    Bsz, S, D = h.shape
    T = Bsz * S
    h2 = h.reshape(T, D)
    idx, w = route(h2, router_w, router_b)
    flat_e = idx.reshape(-1)
    flat_w = w.reshape(-1).astype(h.dtype)
    flat_tok = jnp.repeat(jnp.arange(T, dtype=jnp.int32), TOP_K)
    order = jnp.argsort(flat_e)
    sorted_e = flat_e[order]
    counts = jnp.bincount(flat_e, length=N_EXPERTS)
    padded = ((counts + ROW_BLOCK - 1) // ROW_BLOCK) * ROW_BLOCK
    pad_end = jnp.cumsum(padded)
    pad_start = pad_end - padded
    start = jnp.cumsum(counts) - counts
    rank = jnp.arange(T * TOP_K, dtype=jnp.int32) - start[sorted_e]
    dest = pad_start[sorted_e] + rank
    n_rows = T * TOP_K + N_EXPERTS * ROW_BLOCK
    n_blk = n_rows // ROW_BLOCK
    buf_tok = jnp.full((n_rows,), T, jnp.int32).at[dest].set(flat_tok[order])
    buf_w = jnp.zeros((n_rows,), h.dtype).at[dest].set(flat_w[order])
    blk_start = jnp.arange(n_blk, dtype=pad_end.dtype) * ROW_BLOCK
    blk_e = jnp.minimum(jnp.searchsorted(pad_end, blk_start, side="right"), N_EXPERTS - 1)
    h_pad = jnp.concatenate([h2, jnp.zeros((1, D), h.dtype)], axis=0)
    xb = h_pad[buf_tok].reshape(n_blk, ROW_BLOCK, D)

    def expert_block(args):
        xr, e = args
        return (jax.nn.silu(xr @ w_gate[e]) * (xr @ w_up[e])) @ w_down[e]

    yb = lax.map(expert_block, (xb, blk_e)).reshape(n_rows, D)
    y = jnp.zeros((T + 1, D), h.dtype).at[buf_tok].add(yb * buf_w[:, None])[:T]
    return y.reshape(Bsz, S, D)


def setup_inputs(seed: int = 0) -> dict:
    key = jax.random.key(seed)
    ks = iter(jax.random.split(key, 40))
    D = D_MODEL
    f32 = jnp.float32

    def nrm(shape, scale):
        return jax.random.normal(next(ks), shape, f32) * scale

    def gain(shape):
        return 1.0 + 0.05 * jax.random.normal(next(ks), shape, f32)

    x = jax.random.normal(next(ks), (BATCH, SEQ, D), f32)
    c = jax.random.normal(next(ks), (BATCH, D), f32)
    offs = jax.random.randint(next(ks), (BATCH, 1), 0, 1024, dtype=jnp.int32)
    positions = offs + jnp.arange(SEQ, dtype=jnp.int32)[None, :]
    return {
        "x": x, "c": c, "positions": positions,
        "ada_w": nrm((DEPTH, D, 6 * D), 0.5 * D ** -0.5),
        "ada_b": nrm((DEPTH, 6 * D), 0.02),
        "norm1_g": gain((DEPTH, D)),
        "norm2_g": gain((DEPTH, D)),
        "conv_w_in": nrm((N_A, D, 2 * CONV_CH), D ** -0.5),
        "conv_b_in": nrm((N_A, 2 * CONV_CH), 0.02),
        "conv_w_dw": nrm((N_A, CONV_WIDTH, CONV_CH), CONV_WIDTH ** -0.5),
        "conv_b_dw": nrm((N_A, CONV_CH), 0.02),
        "conv_ln_g": gain((N_A, CONV_CH)),
        "conv_ln_b": nrm((N_A, CONV_CH), 0.02),
        "conv_w_out": nrm((N_A, CONV_CH, D), CONV_CH ** -0.5),
        "kv_in_g": gain((D,)),
        "w_dkv": nrm((D, KV_LORA_RANK + QK_ROPE_DIM), D ** -0.5),
        "kv_norm_g": gain((KV_LORA_RANK,)),
        "w_ukv": nrm((KV_LORA_RANK, N_HEADS * (QK_NOPE_DIM + V_HEAD_DIM)), KV_LORA_RANK ** -0.5),
        "w_dq": nrm((N_B, D, Q_LORA_RANK), D ** -0.5),
        "q_norm_g": gain((N_B, Q_LORA_RANK)),
        "w_uq": nrm((N_B, Q_LORA_RANK, N_HEADS * (QK_NOPE_DIM + QK_ROPE_DIM)), Q_LORA_RANK ** -0.5),
        "w_o": nrm((N_B, N_HEADS * V_HEAD_DIM, D), (N_HEADS * V_HEAD_DIM) ** -0.5),
        "router_w": nrm((D, N_EXPERTS), D ** -0.5),
        "router_b": nrm((N_EXPERTS,), 0.01),
        "exp_w_gate": nrm((DEPTH, N_EXPERTS, D, D_EXPERT), D ** -0.5),
        "exp_w_up": nrm((DEPTH, N_EXPERTS, D, D_EXPERT), D ** -0.5),
        "exp_w_down": nrm((DEPTH, N_EXPERTS, D_EXPERT, D), D_EXPERT ** -0.5),
        "final_g": gain((D,)),
    }


def reference(x, c, positions, ada_w, ada_b, norm1_g, norm2_g,
              conv_w_in, conv_b_in, conv_w_dw, conv_b_dw, conv_ln_g, conv_ln_b, conv_w_out,
              kv_in_g, w_dkv, kv_norm_g, w_ukv,
              w_dq, q_norm_g, w_uq, w_o,
              router_w, router_b, exp_w_gate, exp_w_up, exp_w_down, final_g):
    Bsz, S, _ = x.shape
    cos, sin = rope_tables(positions, x.dtype)
    c_act = jax.nn.silu(c)
    k_nope = k_rope = v = None
    for l in range(DEPTH):
        mod = c_act @ ada_w[l] + ada_b[l]
        sh1, sc1, g1, sh2, sc2, g2 = jnp.split(mod, 6, axis=-1)
        h = modulate(x, norm1_g[l], sh1, sc1)
        if l < N_A:
            mix = conformer_conv(h, conv_w_in[l], conv_b_in[l], conv_w_dw[l], conv_b_dw[l],
                                 conv_ln_g[l], conv_ln_b[l], conv_w_out[l])
        else:
            j = l - N_A
            if j == 0:
                k_nope, k_rope, v = shared_kv(x, kv_in_g, w_dkv, kv_norm_g, w_ukv, cos, sin)
            cq = rms_norm(h @ w_dq[j], q_norm_g[j])
            q = (cq @ w_uq[j]).reshape(Bsz, S, N_HEADS, QK_NOPE_DIM + QK_ROPE_DIM)
            q_nope = q[..., :QK_NOPE_DIM]
            q_rope = apply_rope(q[..., QK_NOPE_DIM:], cos[:, :, None, :], sin[:, :, None, :])
            mix = mla_attention(q_nope, q_rope, k_nope, k_rope, v) @ w_o[j]
        x = x + g1[:, None, :] * mix
        h = modulate(x, norm2_g[l], sh2, sc2)
        x = x + g2[:, None, :] * moe_ffn(h, router_w, router_b,
                                         exp_w_gate[l], exp_w_up[l], exp_w_down[l])
    return rms_norm(x, final_g)
```

```python
import functools
import math

import jax
import jax.numpy as jnp
from jax import lax
from jax.experimental import pallas as pl
from jax.experimental.pallas import tpu as pltpu

f32 = jnp.float32
bf16 = jnp.bfloat16
u32 = jnp.uint32
i32 = jnp.int32

D = 1024
B = 4
S = 4096
T = B * S
DEPTH = 4
N_A = 2
CONV_W = 31
C = 1024
H = 8
DN = 128
DR = 64
DV = 128
DQK = DN + DR
QR = 384
KVR = 256
ROPE_THETA = 10000.0
ATTN_SCALE = 1.0 / math.sqrt(DN + DR)
E = 16
NG = 4
EG = 4
TOPK = 2
DE = 512
EPS = 1e-6
NEG = -1e30

TM = 512
NT = S // TM
HALO = 32
RC = 64
TQ = 512
TK = 512
RB = 256
N_ROWS = T * TOPK + E * RB
NB = N_ROWS // RB
GR = 512
HP = D // 2

VMEM_LIMIT = 56 * 1024 * 1024


def _cp(n_axes):
    return pltpu.CompilerParams(
        dimension_semantics=("arbitrary",) * n_axes, vmem_limit_bytes=VMEM_LIMIT)


def _full(shape):
    n = len(shape)
    return pl.BlockSpec(shape, lambda *_: (0,) * n)


def _rms(x, g):
    return x * lax.rsqrt(jnp.mean(x * x, axis=-1, keepdims=True) + EPS) * g


def _pack_pair(a, b):
    ua = lax.bitcast_convert_type(a.astype(bf16).astype(f32), u32)
    ub = lax.bitcast_convert_type(b.astype(bf16).astype(f32), u32)
    return (ua >> 16) | ub


def _unpack_pair(w):
    a = lax.bitcast_convert_type(w << 16, f32)
    b = lax.bitcast_convert_type(w & jnp.uint32(0xFFFF0000), f32)
    return a, b


def _unpack_rows(w):
    a, b = _unpack_pair(w)
    return jnp.concatenate([a, b], axis=1)


def _dot_nt(a, b):
    return lax.dot_general(a, b, (((1,), (1,)), ((), ())), preferred_element_type=f32)


def _top2sum4(a, b, c, d):
    lo1, hi1 = jnp.minimum(a, b), jnp.maximum(a, b)
    lo2, hi2 = jnp.minimum(c, d), jnp.maximum(c, d)
    return jnp.maximum(hi1, hi2) + jnp.maximum(jnp.minimum(hi1, hi2), jnp.maximum(lo1, lo2))


def _route_and_pack(x_new, mod_ref, n2g_ref, rw_ref, rb_ref, h2p_ref, eidx_ref, ew_ref):
    sh2 = mod_ref[0, 3:4, :]
    sc2 = mod_ref[0, 4:5, :]
    h2 = _rms(x_new, n2g_ref[...]) * (1.0 + sc2) + sh2
    h2p_ref[...] = _pack_pair(h2[:, :HP], h2[:, HP:])

    hi = h2.astype(bf16)
    lo = (h2 - hi.astype(f32)).astype(bf16)
    rw = rw_ref[...]
    p = _dot_nt(rw, hi) + _dot_nt(rw, lo)
    logits = p[:E] + p[E:]
    s = jax.nn.sigmoid(logits)
    sb = s + rb_ref[...]

    srow = [s[e:e + 1] for e in range(E)]
    brow = [sb[e:e + 1] for e in range(E)]
    gs = [_top2sum4(*brow[EG * g:EG * g + EG]) for g in range(NG)]
    gidx = jnp.zeros_like(gs[0], dtype=i32)
    best = gs[0]
    for g in range(1, NG):
        upd = gs[g] > best
        gidx = jnp.where(upd, g, gidx)
        best = jnp.where(upd, gs[g], best)

    def pick(rows, i):
        out = rows[i]
        for g in range(1, NG):
            out = jnp.where(gidx == g, rows[EG * g + i], out)
        return out

    wv = [pick(brow, i) for i in range(EG)]
    sv = [pick(srow, i) for i in range(EG)]

    l1 = jnp.zeros_like(gidx)
    b1 = wv[0]
    for i in range(1, EG):
        upd = wv[i] > b1
        l1 = jnp.where(upd, i, l1)
        b1 = jnp.where(upd, wv[i], b1)
    l2 = jnp.zeros_like(gidx)
    b2 = jnp.where(l1 == 0, -jnp.inf, wv[0])
    for i in range(1, EG):
        cand = jnp.where(l1 == i, -jnp.inf, wv[i])
        upd = cand > b2
        l2 = jnp.where(upd, i, l2)
        b2 = jnp.where(upd, cand, b2)

    def sel(loc):
        out = sv[0]
        for i in range(1, EG):
            out = jnp.where(loc == i, sv[i], out)
        return out

    w1 = sel(l1)
    w2 = sel(l2)
    tot = w1 + w2
    eidx_ref[...] = jnp.concatenate([gidx * EG + l1, gidx * EG + l2], axis=0)
    ew_ref[...] = jnp.concatenate([w1 / tot, w2 / tot], axis=0)


def _moe_combine(x_ref, yg_ref, modp_ref):
    y0 = _unpack_rows(yg_ref[0])
    y1 = _unpack_rows(yg_ref[1])
    g2 = modp_ref[0, 5:6, :]
    return x_ref[0] + g2 * (y0 + y1)


def _ada_kernel(c_ref, w_ref, b_ref, o_ref):
    c = c_ref[...]
    ca = (c * jax.nn.sigmoid(c)).astype(bf16)
    o_ref[0] = jnp.dot(ca, w_ref[0].astype(bf16), preferred_element_type=f32) + b_ref[0]


def _ada(c8, ada_w, ada_b3):
    tn = 1536
    return pl.pallas_call(
        _ada_kernel,
        grid=(DEPTH, 6 * D // tn),
        in_specs=[
            pl.BlockSpec((8, D), lambda l, n: (0, 0)),
            pl.BlockSpec((1, D, tn), lambda l, n: (l, 0, n)),
            pl.BlockSpec((1, 1, tn), lambda l, n: (l, 0, n)),
        ],
        out_specs=pl.BlockSpec((1, 8, tn), lambda l, n: (l, 0, n)),
        out_shape=jax.ShapeDtypeStruct((DEPTH, 8, 6 * D), f32),
        compiler_params=_cp(2),
        name="ada",
    )(c8, ada_w, ada_b3)


def _rope_kernel(pos_ref, invf_ref, cos_ref, sin_ref):
    ang = pos_ref[0].astype(f32) * invf_ref[...]
    cos_ref[0] = jnp.cos(ang)
    sin_ref[0] = jnp.sin(ang)


def _rope_tables(pos3, invf):
    return pl.pallas_call(
        _rope_kernel,
        grid=(B,),
        in_specs=[pl.BlockSpec((1, 1, S), lambda b: (b, 0, 0)), _full((DR // 2, 1))],
        out_specs=[pl.BlockSpec((1, DR // 2, S), lambda b: (b, 0, 0))] * 2,
        out_shape=[jax.ShapeDtypeStruct((B, DR // 2, S), f32)] * 2,
        compiler_params=_cp(1),
        name="rope",
    )(pos3, invf)


def _conv_kernel(has_moe_in, *refs):
    if has_moe_in:
        x_ref, yg_ref, modp_ref = refs[:3]
        refs = refs[3:]
    else:
        x_ref = refs[0]
        refs = refs[1:]
    (mod_ref, n1g_ref, win_ref, bin_ref, wdw_ref, bdw_ref, lng_ref, lnb_ref, wout_ref,
     n2g_ref, rw_ref, rb_ref,
     xo_ref, h2p_ref, eidx_ref, ew_ref, ubuf, cbuf) = refs
    si = pl.program_id(1)

    x = _moe_combine(x_ref, yg_ref, modp_ref) if has_moe_in else x_ref[0]
    sh1 = mod_ref[0, 0:1, :]
    sc1 = mod_ref[0, 1:2, :]
    g1 = mod_ref[0, 2:3, :]
    h = (_rms(x, n1g_ref[...]) * (1.0 + sc1) + sh1).astype(bf16)
    u = jnp.dot(h, win_ref[...], preferred_element_type=f32) + bin_ref[...]
    glu = u[:, :C] * jax.nn.sigmoid(u[:, C:])

    @pl.when(si == 0)
    def _():
        ubuf[0:HALO, :] = jnp.zeros((HALO, C), f32)

    @pl.when(si > 0)
    def _():
        ubuf[0:HALO, :] = ubuf[TM:TM + HALO, :]

    ubuf[HALO:HALO + TM, :] = glu

    off = HALO - (CONV_W - 1)

    def chunk(ci, carry):
        c0 = pl.multiple_of(ci * 128, 128)
        for r in range(TM // RC):
            acc = None
            for j in range(CONV_W):
                win = ubuf[r * RC + off + j:r * RC + off + j + RC, pl.ds(c0, 128)]
                term = win * wdw_ref[j:j + 1, pl.ds(c0, 128)]
                acc = term if acc is None else acc + term
            cbuf[r * RC:(r + 1) * RC, pl.ds(c0, 128)] = acc + bdw_ref[:, pl.ds(c0, 128)]
        return carry

    lax.fori_loop(0, C // 128, chunk, 0)

    v = cbuf[...]
    mu = jnp.mean(v, axis=-1, keepdims=True)
    vc = v - mu
    var = jnp.mean(vc * vc, axis=-1, keepdims=True)
    y = vc * lax.rsqrt(var + EPS) * lng_ref[...] + lnb_ref[...]
    y = (y * jax.nn.sigmoid(y)).astype(bf16)
    mix = jnp.dot(y, wout_ref[...], preferred_element_type=f32)
    x_new = x + g1 * mix
    xo_ref[0] = x_new
    _route_and_pack(x_new, mod_ref, n2g_ref, rw_ref, rb_ref, h2p_ref, eidx_ref, ew_ref)


def _tok_specs():
    specs = [
        pl.BlockSpec((1, TM, D), lambda b, s: (b, s, 0)),
        pl.BlockSpec((TM, HP), lambda b, s: (b * NT + s, 0)),
        pl.BlockSpec((TOPK, TM), lambda b, s: (0, b * NT + s)),
        pl.BlockSpec((TOPK, TM), lambda b, s: (0, b * NT + s)),
    ]
    shapes = [
        jax.ShapeDtypeStruct((B, S, D), f32),
        jax.ShapeDtypeStruct((T, HP), u32),
        jax.ShapeDtypeStruct((TOPK, T), i32),
        jax.ShapeDtypeStruct((TOPK, T), f32),
    ]
    return specs, shapes


def _moe_in_specs():
    return [
        pl.BlockSpec((TOPK, TM, HP), lambda b, s: (0, b * NT + s, 0)),
        pl.BlockSpec((1, 6, D), lambda b, s: (b, 0, 0)),
    ]


def _conv_layer(x, moe_in, mod_l, n1g, win, b_in, wdw, bdw, lng, lnb, wout, n2g, rw, rb):
    has_moe_in = moe_in is not None
    in_specs = [pl.BlockSpec((1, TM, D), lambda b, s: (b, s, 0))]
    args = [x]
    if has_moe_in:
        in_specs += _moe_in_specs()
        args += list(moe_in)
    in_specs += [
        pl.BlockSpec((1, 6, D), lambda b, s: (b, 0, 0)),
        _full((1, D)), _full((D, 2 * C)), _full((1, 2 * C)), _full((HALO, C)), _full((1, C)),
        _full((1, C)), _full((1, C)), _full((C, D)), _full((1, D)), _full((2 * E, D)),
        _full((E, 1)),
    ]
    args += [mod_l, n1g, win, b_in, wdw, bdw, lng, lnb, wout, n2g, rw, rb]
    out_specs, out_shape = _tok_specs()
    return pl.pallas_call(
        functools.partial(_conv_kernel, has_moe_in),
        grid=(B, NT),
        in_specs=in_specs,
        out_specs=out_specs,
        out_shape=out_shape,
        scratch_shapes=[pltpu.VMEM((TM + HALO, C), f32), pltpu.VMEM((TM, C), f32)],
        compiler_params=_cp(2),
        name="conv_layer",
    )(*args)


def _preattn_kernel(with_kv, *refs):
    (x_ref, yg_ref, modp_ref, mod_ref, n1g_ref, wdq_ref, qng_ref, wuqt_ref, cos_ref,
     sin_ref) = refs[:10]
    refs = refs[10:]
    if with_kv:
        kvg_ref, wlat_ref, wropet_ref, kvng_ref, wuk_ref, wuvt_ref = refs[:6]
        refs = refs[6:]
        xo_ref, qt_ref, kc_ref, vt_ref = refs
    else:
        xo_ref, qt_ref = refs

    x = _moe_combine(x_ref, yg_ref, modp_ref)
    xo_ref[0] = x
    cos = cos_ref[0]
    sin = sin_ref[0]
    sh1 = mod_ref[0, 0:1, :]
    sc1 = mod_ref[0, 1:2, :]
    h = (_rms(x, n1g_ref[...]) * (1.0 + sc1) + sh1).astype(bf16)
    cq = jnp.dot(h, wdq_ref[...], preferred_element_type=f32)
    cq = _rms(cq, qng_ref[...]).astype(bf16)
    qt = _dot_nt(wuqt_ref[...], cq)
    hr = DR // 2
    for hh in range(H):
        r0 = hh * DQK
        x1 = qt[r0 + DN:r0 + DN + hr]
        x2 = qt[r0 + DN + hr:r0 + DQK]
        qt_ref[0, hh, 0:DN, :] = (qt[r0:r0 + DN] * ATTN_SCALE).astype(bf16)
        qt_ref[0, hh, DN:DN + hr, :] = ((x1 * cos - x2 * sin) * ATTN_SCALE).astype(bf16)
        qt_ref[0, hh, DN + hr:DQK, :] = ((x1 * sin + x2 * cos) * ATTN_SCALE).astype(bf16)

    if with_kv:
        xn = _rms(x, kvg_ref[...]).astype(bf16)
        lat = jnp.dot(xn, wlat_ref[...], preferred_element_type=f32)
        latn = _rms(lat, kvng_ref[...]).astype(bf16)
        krt = _dot_nt(wropet_ref[...], xn)
        k1 = krt[:hr]
        k2 = krt[hr:]
        kr = jnp.concatenate([k1 * cos - k2 * sin, k1 * sin + k2 * cos], axis=0)
        kr = kr.T.astype(bf16)
        kn = jnp.dot(latn, wuk_ref[...], preferred_element_type=f32).astype(bf16)
        vt = _dot_nt(wuvt_ref[...], latn).astype(bf16)
        for hh in range(H):
            kc_ref[0, hh, :, 0:DN] = kn[:, hh * DN:(hh + 1) * DN]
            kc_ref[0, hh, :, DN:DQK] = kr
            vt_ref[0, hh] = vt[hh * DV:(hh + 1) * DV]


def _preattn(x, moe_in, mod_l, n1g, wdq, qng, wuqt, cos_t, sin_t, kv=None):
    with_kv = kv is not None
    in_specs = [pl.BlockSpec((1, TM, D), lambda b, s: (b, s, 0))] + _moe_in_specs() + [
        pl.BlockSpec((1, 6, D), lambda b, s: (b, 0, 0)),
        _full((1, D)), _full((D, QR)), _full((1, QR)), _full((H * DQK, QR)),
        pl.BlockSpec((1, DR // 2, TM), lambda b, s: (b, 0, s)),
        pl.BlockSpec((1, DR // 2, TM), lambda b, s: (b, 0, s)),
    ]
    args = [x, *moe_in, mod_l, n1g, wdq, qng, wuqt, cos_t, sin_t]
    out_specs = [
        pl.BlockSpec((1, TM, D), lambda b, s: (b, s, 0)),
        pl.BlockSpec((1, H, DQK, TM), lambda b, s: (b, 0, 0, s)),
    ]
    out_shape = [
        jax.ShapeDtypeStruct((B, S, D), f32),
        jax.ShapeDtypeStruct((B, H, DQK, S), bf16),
    ]
    if with_kv:
        in_specs += [_full((1, D)), _full((D, KVR)), _full((DR, D)), _full((1, KVR)),
                     _full((KVR, H * DN)), _full((H * DV, KVR))]
        args += list(kv)
        out_specs += [
            pl.BlockSpec((1, H, TM, DQK), lambda b, s: (b, 0, s, 0)),
            pl.BlockSpec((1, H, DV, TM), lambda b, s: (b, 0, 0, s)),
        ]
        out_shape += [
            jax.ShapeDtypeStruct((B, H, S, DQK), bf16),
            jax.ShapeDtypeStruct((B, H, DV, S), bf16),
        ]
    return pl.pallas_call(
        functools.partial(_preattn_kernel, with_kv),
        grid=(B, NT),
        in_specs=in_specs,
        out_specs=out_specs,
        out_shape=out_shape,
        compiler_params=_cp(2),
        name="preattn_kv" if with_kv else "preattn",
    )(*args)


def _attn_kernel(qt_ref, kc_ref, vt_ref, o_ref, m_ref, l_ref, acc_ref):
    qi = pl.program_id(2)
    q = qt_ref[0, 0]
    m_ref[...] = jnp.full((1, TQ), NEG, f32)
    l_ref[...] = jnp.zeros((1, TQ), f32)
    acc_ref[...] = jnp.zeros((DV, TQ), f32)

    def step(j, masked):
        k0 = pl.multiple_of(j * TK, TK)
        k = kc_ref[0, 0, pl.ds(k0, TK), :]
        s = jnp.dot(k, q, preferred_element_type=f32)
        if masked:
            kpos = k0 + lax.broadcasted_iota(i32, (TK, TQ), 0)
            qpos = qi * TQ + lax.broadcasted_iota(i32, (TK, TQ), 1)
            s = jnp.where(kpos <= qpos, s, NEG)
        m_old = m_ref[...]
        m_new = jnp.maximum(m_old, jnp.max(s, axis=0, keepdims=True))
        alpha = jnp.exp(m_old - m_new)
        p = jnp.exp(s - m_new)
        l_ref[...] = alpha * l_ref[...] + jnp.sum(p, axis=0, keepdims=True)
        v = vt_ref[0, 0, :, pl.ds(k0, TK)]
        acc_ref[...] = alpha * acc_ref[...] + jnp.dot(
            v, p.astype(bf16), preferred_element_type=f32)
        m_ref[...] = m_new

    def body(j, carry):
        step(j, False)
        return carry

    lax.fori_loop(0, qi, body, 0)
    step(qi, True)
    o = acc_ref[...] / l_ref[...]
    o_ref[0] = o.T.astype(bf16)


def _attention(qt, kc, vt):
    return pl.pallas_call(
        _attn_kernel,
        grid=(B, H, S // TQ),
        in_specs=[
            pl.BlockSpec((1, 1, DQK, TQ), lambda b, h, q: (b, h, 0, q)),
            pl.BlockSpec((1, 1, S, DQK), lambda b, h, q: (b, h, 0, 0)),
            pl.BlockSpec((1, 1, DV, S), lambda b, h, q: (b, h, 0, 0)),
        ],
        out_specs=pl.BlockSpec((1, TQ, DV), lambda b, h, q: (b, q, h)),
        out_shape=jax.ShapeDtypeStruct((B, S, H * DV), bf16),
        scratch_shapes=[pltpu.VMEM((1, TQ), f32), pltpu.VMEM((1, TQ), f32),
                        pltpu.VMEM((DV, TQ), f32)],
        compiler_params=_cp(3),
        name="attn",
    )(qt, kc, vt)


def _postattn_kernel(x_ref, o_ref, mod_ref, wo_ref, n2g_ref, rw_ref, rb_ref,
                     xo_ref, h2p_ref, eidx_ref, ew_ref):
    g1 = mod_ref[0, 2:3, :]
    mix = jnp.dot(o_ref[0], wo_ref[...], preferred_element_type=f32)
    x_new = x_ref[0] + g1 * mix
    xo_ref[0] = x_new
    _route_and_pack(x_new, mod_ref, n2g_ref, rw_ref, rb_ref, h2p_ref, eidx_ref, ew_ref)


def _postattn(x, o, mod_l, wo, n2g, rw, rb):
    out_specs, out_shape = _tok_specs()
    return pl.pallas_call(
        _postattn_kernel,
        grid=(B, NT),
        in_specs=[
            pl.BlockSpec((1, TM, D), lambda b, s: (b, s, 0)),
            pl.BlockSpec((1, TM, H * DV), lambda b, s: (b, s, 0)),
            pl.BlockSpec((1, 6, D), lambda b, s: (b, 0, 0)),
            _full((H * DV, D)), _full((1, D)), _full((2 * E, D)), _full((E, 1)),
        ],
        out_specs=out_specs,
        out_shape=out_shape,
        compiler_params=_cp(2),
        name="postattn",
    )(x, o, mod_l, wo, n2g, rw, rb)


def _row_copy(src_ref, dst_ref, sem, src_row, dst_row):
    return pltpu.make_async_copy(
        src_ref.at[pl.ds(src_row, 1)], dst_ref.at[pl.ds(dst_row, 1)], sem)


def _gather_kernel(idx_ref, src_ref, dst_ref, sem):
    base = pl.program_id(0) * GR

    def issue(r, carry):
        _row_copy(src_ref, dst_ref, sem, idx_ref[r], base + r).start()
        return carry

    lax.fori_loop(0, GR, issue, 0, unroll=8)

    def drain(r, carry):
        _row_copy(src_ref, dst_ref, sem, 0, base + r).wait()
        return carry

    lax.fori_loop(0, GR, drain, 0, unroll=8)


def _gather_rows(src, idx):
    n = idx.shape[0]
    return pl.pallas_call(
        _gather_kernel,
        grid=(n // GR,),
        in_specs=[
            pl.BlockSpec((GR,), lambda i: (i,), memory_space=pltpu.SMEM),
            pl.BlockSpec(memory_space=pl.ANY),
        ],
        out_specs=pl.BlockSpec(memory_space=pl.ANY),
        out_shape=jax.ShapeDtypeStruct((n, src.shape[1]), src.dtype),
        scratch_shapes=[pltpu.SemaphoreType.DMA],
        compiler_params=_cp(1),
        name="gather_rows",
    )(idx, src)


def _expert_kernel(be_ref, first_ref, xb_ref, bw_ref, wg_ref, wu_ref, wd_ref, yb_ref,
                   wgu_bf, wd_bf):
    i = pl.program_id(0)

    @pl.when(first_ref[i] == 1)
    def _():
        wgu_bf[:, :DE] = wg_ref[0].astype(bf16)
        wgu_bf[:, DE:] = wu_ref[0].astype(bf16)
        wd_bf[...] = wd_ref[0].astype(bf16)

    x = _unpack_rows(xb_ref[...]).astype(bf16)
    gu = jnp.dot(x, wgu_bf[...], preferred_element_type=f32)
    g = gu[:, :DE]
    hmid = (g * jax.nn.sigmoid(g) * gu[:, DE:]).astype(bf16)
    y = jnp.dot(hmid, wd_bf[...], preferred_element_type=f32) * bw_ref[...]
    yb_ref[...] = _pack_pair(y[:, :HP], y[:, HP:])


def _experts(blk_e, blk_first, xb, bw, wg, wu, wd):
    grid_spec = pltpu.PrefetchScalarGridSpec(
        num_scalar_prefetch=2,
        grid=(NB,),
        in_specs=[
            pl.BlockSpec((RB, HP), lambda i, be, bf: (i, 0)),
            pl.BlockSpec((RB, 1), lambda i, be, bf: (i, 0)),
            pl.BlockSpec((1, D, DE), lambda i, be, bf: (be[i], 0, 0)),
            pl.BlockSpec((1, D, DE), lambda i, be, bf: (be[i], 0, 0)),
            pl.BlockSpec((1, DE, D), lambda i, be, bf: (be[i], 0, 0)),
        ],
        out_specs=pl.BlockSpec((RB, HP), lambda i, be, bf: (i, 0)),
        scratch_shapes=[pltpu.VMEM((D, 2 * DE), bf16), pltpu.VMEM((DE, D), bf16)],
    )
    return pl.pallas_call(
        _expert_kernel,
        grid_spec=grid_spec,
        out_shape=jax.ShapeDtypeStruct((N_ROWS, HP), u32),
        compiler_params=_cp(1),
        name="experts",
    )(blk_e, blk_first, xb, bw, wg, wu, wd)


def _final_kernel(x_ref, yg_ref, modp_ref, g_ref, o_ref):
    o_ref[0] = _rms(_moe_combine(x_ref, yg_ref, modp_ref), g_ref[...])


def _final(x, moe_in, g):
    return pl.pallas_call(
        _final_kernel,
        grid=(B, NT),
        in_specs=[pl.BlockSpec((1, TM, D), lambda b, s: (b, s, 0))] + _moe_in_specs()
        + [_full((1, D))],
        out_specs=pl.BlockSpec((1, TM, D), lambda b, s: (b, s, 0)),
        out_shape=jax.ShapeDtypeStruct((B, S, D), f32),
        compiler_params=_cp(2),
        name="final_norm",
    )(x, *moe_in, g)


def _moe(h2p, eidx, ew, wg, wu, wd):
    flat_e = eidx.reshape(-1)
    onehot = (flat_e[:, None] == jnp.arange(E, dtype=i32)[None, :]).astype(i32)
    csum = jnp.cumsum(onehot, axis=0)
    counts = csum[-1]
    rank = jnp.take_along_axis(csum, flat_e[:, None], axis=1)[:, 0] - 1
    padded = ((counts + RB - 1) // RB) * RB
    pad_end = jnp.cumsum(padded)
    pad_start = pad_end - padded
    dest = (pad_start[flat_e] + rank).astype(i32)
    tok = jnp.tile(jnp.arange(T, dtype=i32), TOPK)
    buf_tok = jnp.zeros((N_ROWS,), i32).at[dest].set(tok)
    buf_w = jnp.zeros((N_ROWS,), f32).at[dest].set(ew.reshape(-1))
    blk_start = jnp.arange(NB, dtype=i32) * RB
    blk_e = jnp.minimum(jnp.searchsorted(pad_end, blk_start, side="right"), E - 1).astype(i32)
    blk_first = jnp.concatenate(
        [jnp.ones((1,), i32), (blk_e[1:] != blk_e[:-1]).astype(i32)])

    xb = _gather_rows(h2p, buf_tok)
    yb = _experts(blk_e, blk_first, xb, buf_w[:, None], wg, wu, wd)
    yg = _gather_rows(yb, dest)
    return yg.reshape(TOPK, T, HP)


def kernel(x, c, positions, ada_w, ada_b, norm1_g, norm2_g, conv_w_in, conv_b_in, conv_w_dw,
           conv_b_dw, conv_ln_g, conv_ln_b, conv_w_out, kv_in_g, w_dkv, kv_norm_g, w_ukv, w_dq,
           q_norm_g, w_uq, w_o, router_w, router_b, exp_w_gate, exp_w_up, exp_w_down, final_g):
    c8 = jnp.pad(c, ((0, 8 - B), (0, 0)))
    mod = _ada(c8, ada_w, ada_b[:, None, :])[:, :B].reshape(DEPTH, B, 6, D)

    invf = (ROPE_THETA ** (-jnp.arange(0, DR, 2, dtype=f32) / DR))[:, None]
    cos_t, sin_t = _rope_tables(positions[:, None, :], invf)

    rwt = router_w.astype(f32).T
    rw_hi = rwt.astype(bf16)
    rw_lo = (rwt - rw_hi.astype(f32)).astype(bf16)
    rw = jnp.concatenate([rw_hi, rw_lo], axis=0)
    rb = router_b.astype(f32)[:, None]

    w_ukv3 = w_ukv.reshape(KVR, H, DN + DV)
    wuk = w_ukv3[:, :, :DN].reshape(KVR, H * DN).astype(bf16)
    wuvt = w_ukv3[:, :, DN:].reshape(KVR, H * DV).T.astype(bf16)
    kv_w = (kv_in_g[None, :], w_dkv[:, :KVR].astype(bf16), w_dkv[:, KVR:].T.astype(bf16),
            kv_norm_g[None, :], wuk, wuvt)

    moe_in = None
    kc = vt = None
    for l in range(DEPTH):
        if l < N_A:
            wdw = jnp.pad(conv_w_dw[l], ((0, HALO - CONV_W), (0, 0)))
            x, h2p, eidx, ew = _conv_layer(
                x, moe_in, mod[l], norm1_g[l][None, :], conv_w_in[l].astype(bf16),
                conv_b_in[l][None, :], wdw, conv_b_dw[l][None, :], conv_ln_g[l][None, :],
                conv_ln_b[l][None, :], conv_w_out[l].astype(bf16), norm2_g[l][None, :], rw, rb)
        else:
            j = l - N_A
            outs = _preattn(
                x, moe_in, mod[l], norm1_g[l][None, :], w_dq[j].astype(bf16),
                q_norm_g[j][None, :], w_uq[j].T.astype(bf16), cos_t, sin_t,
                kv=kv_w if j == 0 else None)
            if j == 0:
                x, qt, kc, vt = outs
            else:
                x, qt = outs
            o = _attention(qt, kc, vt)
            x, h2p, eidx, ew = _postattn(
                x, o, mod[l], w_o[j].astype(bf16), norm2_g[l][None, :], rw, rb)
        yg = _moe(h2p, eidx, ew, exp_w_gate[l], exp_w_up[l], exp_w_down[l])
        moe_in = (yg, mod[l])
    return _final(x, moe_in, final_g[None, :])
```

```python
import functools
import math

import jax
import jax.numpy as jnp
from jax import lax
from jax.experimental import pallas as pl
from jax.experimental.pallas import tpu as pltpu

f32 = jnp.float32
bf16 = jnp.bfloat16
u32 = jnp.uint32
i32 = jnp.int32

D = 1024
B = 4
S = 4096
T = B * S
DEPTH = 4
N_A = 2
CONV_W = 31
C = 1024
H = 8
DN = 128
DR = 64
DV = 128
DQK = DN + DR
QR = 384
KVR = 256
ROPE_THETA = 10000.0
ATTN_SCALE = 1.0 / math.sqrt(DN + DR)
E = 16
NG = 4
EG = 4
TOPK = 2
DE = 512
EPS = 1e-6
NEG = -1e30

TM = 512
NT = S // TM
HALO = 32
RC = 64
TQ = 512
TK = 512
RB = 256
N_ROWS = T * TOPK + E * RB
NB = N_ROWS // RB
GR = 2048
HP = D // 2

VMEM_LIMIT = 56 * 1024 * 1024


def _cp(n_axes):
    return pltpu.CompilerParams(
        dimension_semantics=("arbitrary",) * n_axes, vmem_limit_bytes=VMEM_LIMIT)


def _full(shape):
    n = len(shape)
    return pl.BlockSpec(shape, lambda *_: (0,) * n)


def _rms(x, g):
    return x * lax.rsqrt(jnp.mean(x * x, axis=-1, keepdims=True) + EPS) * g


def _pack_pair(a, b):
    ua = lax.bitcast_convert_type(a.astype(bf16).astype(f32), u32)
    ub = lax.bitcast_convert_type(b.astype(bf16).astype(f32), u32)
    return (ua >> 16) | ub


def _unpack_pair(w):
    a = lax.bitcast_convert_type(w << 16, f32)
    b = lax.bitcast_convert_type(w & jnp.uint32(0xFFFF0000), f32)
    return a, b


def _unpack_rows(w):
    a, b = _unpack_pair(w)
    return jnp.concatenate([a, b], axis=1)


def _dot_nt(a, b):
    return lax.dot_general(a, b, (((1,), (1,)), ((), ())), preferred_element_type=f32)


def _top2sum4(a, b, c, d):
    lo1, hi1 = jnp.minimum(a, b), jnp.maximum(a, b)
    lo2, hi2 = jnp.minimum(c, d), jnp.maximum(c, d)
    return jnp.maximum(hi1, hi2) + jnp.maximum(jnp.minimum(hi1, hi2), jnp.maximum(lo1, lo2))


def _route_and_pack(x_new, mod_ref, n2g_ref, rw_ref, rb_ref, h2p_ref, eidx_ref, ew_ref):
    sh2 = mod_ref[0, 3:4, :]
    sc2 = mod_ref[0, 4:5, :]
    h2 = _rms(x_new, n2g_ref[...]) * (1.0 + sc2) + sh2
    h2p_ref[...] = _pack_pair(h2[:, :HP], h2[:, HP:])

    hi = h2.astype(bf16)
    lo = (h2 - hi.astype(f32)).astype(bf16)
    rw = rw_ref[...]
    p = _dot_nt(rw, hi) + _dot_nt(rw, lo)
    logits = p[:E] + p[E:]
    s = jax.nn.sigmoid(logits)
    sb = s + rb_ref[...]

    srow = [s[e:e + 1] for e in range(E)]
    brow = [sb[e:e + 1] for e in range(E)]
    gs = [_top2sum4(*brow[EG * g:EG * g + EG]) for g in range(NG)]
    gidx = jnp.zeros_like(gs[0], dtype=i32)
    best = gs[0]
    for g in range(1, NG):
        upd = gs[g] > best
        gidx = jnp.where(upd, g, gidx)
        best = jnp.where(upd, gs[g], best)

    def pick(rows, i):
        out = rows[i]
        for g in range(1, NG):
            out = jnp.where(gidx == g, rows[EG * g + i], out)
        return out

    wv = [pick(brow, i) for i in range(EG)]
    sv = [pick(srow, i) for i in range(EG)]

    l1 = jnp.zeros_like(gidx)
    b1 = wv[0]
    for i in range(1, EG):
        upd = wv[i] > b1
        l1 = jnp.where(upd, i, l1)
        b1 = jnp.where(upd, wv[i], b1)
    l2 = jnp.zeros_like(gidx)
    b2 = jnp.where(l1 == 0, -jnp.inf, wv[0])
    for i in range(1, EG):
        cand = jnp.where(l1 == i, -jnp.inf, wv[i])
        upd = cand > b2
        l2 = jnp.where(upd, i, l2)
        b2 = jnp.where(upd, cand, b2)

    def sel(loc):
        out = sv[0]
        for i in range(1, EG):
            out = jnp.where(loc == i, sv[i], out)
        return out

    w1 = sel(l1)
    w2 = sel(l2)
    tot = w1 + w2
    eidx_ref[...] = jnp.concatenate([gidx * EG + l1, gidx * EG + l2], axis=0)
    ew_ref[...] = jnp.concatenate([w1 / tot, w2 / tot], axis=0)


def _moe_combine(x_ref, yg_ref, modp_ref):
    y0 = _unpack_rows(yg_ref[0])
    y1 = _unpack_rows(yg_ref[1])
    g2 = modp_ref[0, 5:6, :]
    return x_ref[0] + g2 * (y0 + y1)


def _ada_kernel(c_ref, w_ref, b_ref, o_ref):
    c = c_ref[...]
    ca = (c * jax.nn.sigmoid(c)).astype(bf16)
    o_ref[0] = jnp.dot(ca, w_ref[0].astype(bf16), preferred_element_type=f32) + b_ref[0]


def _ada(c8, ada_w, ada_b3):
    tn = 1536
    return pl.pallas_call(
        _ada_kernel,
        grid=(DEPTH, 6 * D // tn),
        in_specs=[
            pl.BlockSpec((8, D), lambda l, n: (0, 0)),
            pl.BlockSpec((1, D, tn), lambda l, n: (l, 0, n)),
            pl.BlockSpec((1, 1, tn), lambda l, n: (l, 0, n)),
        ],
        out_specs=pl.BlockSpec((1, 8, tn), lambda l, n: (l, 0, n)),
        out_shape=jax.ShapeDtypeStruct((DEPTH, 8, 6 * D), f32),
        compiler_params=_cp(2),
        name="ada",
    )(c8, ada_w, ada_b3)


def _rope_kernel(pos_ref, invf_ref, cos_ref, sin_ref):
    ang = pos_ref[0].astype(f32) * invf_ref[...]
    cos_ref[0] = jnp.cos(ang)
    sin_ref[0] = jnp.sin(ang)


def _rope_tables(pos3, invf):
    return pl.pallas_call(
        _rope_kernel,
        grid=(B,),
        in_specs=[pl.BlockSpec((1, 1, S), lambda b: (b, 0, 0)), _full((DR // 2, 1))],
        out_specs=[pl.BlockSpec((1, DR // 2, S), lambda b: (b, 0, 0))] * 2,
        out_shape=[jax.ShapeDtypeStruct((B, DR // 2, S), f32)] * 2,
        compiler_params=_cp(1),
        name="rope",
    )(pos3, invf)


def _conv_kernel(has_moe_in, *refs):
    if has_moe_in:
        x_ref, yg_ref, modp_ref = refs[:3]
        refs = refs[3:]
    else:
        x_ref = refs[0]
        refs = refs[1:]
    (mod_ref, n1g_ref, win_ref, bin_ref, wdw_ref, bdw_ref, lng_ref, lnb_ref, wout_ref,
     n2g_ref, rw_ref, rb_ref,
     xo_ref, h2p_ref, eidx_ref, ew_ref, ubuf, cbuf) = refs
    si = pl.program_id(1)

    x = _moe_combine(x_ref, yg_ref, modp_ref) if has_moe_in else x_ref[0]
    sh1 = mod_ref[0, 0:1, :]
    sc1 = mod_ref[0, 1:2, :]
    g1 = mod_ref[0, 2:3, :]
    h = (_rms(x, n1g_ref[...]) * (1.0 + sc1) + sh1).astype(bf16)
    u = jnp.dot(h, win_ref[...], preferred_element_type=f32) + bin_ref[...]
    glu = u[:, :C] * jax.nn.sigmoid(u[:, C:])

    @pl.when(si == 0)
    def _():
        ubuf[0:HALO, :] = jnp.zeros((HALO, C), f32)

    @pl.when(si > 0)
    def _():
        ubuf[0:HALO, :] = ubuf[TM:TM + HALO, :]

    ubuf[HALO:HALO + TM, :] = glu

    off = HALO - (CONV_W - 1)

    def chunk(ci, carry):
        c0 = pl.multiple_of(ci * 128, 128)
        for r in range(TM // RC):
            acc = None
            for j in range(CONV_W):
                win = ubuf[r * RC + off + j:r * RC + off + j + RC, pl.ds(c0, 128)]
                term = win * wdw_ref[j:j + 1, pl.ds(c0, 128)]
                acc = term if acc is None else acc + term
            cbuf[r * RC:(r + 1) * RC, pl.ds(c0, 128)] = acc + bdw_ref[:, pl.ds(c0, 128)]
        return carry

    lax.fori_loop(0, C // 128, chunk, 0)

    v = cbuf[...]
    mu = jnp.mean(v, axis=-1, keepdims=True)
    vc = v - mu
    var = jnp.mean(vc * vc, axis=-1, keepdims=True)
    y = vc * lax.rsqrt(var + EPS) * lng_ref[...] + lnb_ref[...]
    y = (y * jax.nn.sigmoid(y)).astype(bf16)
    mix = jnp.dot(y, wout_ref[...], preferred_element_type=f32)
    x_new = x + g1 * mix
    xo_ref[0] = x_new
    _route_and_pack(x_new, mod_ref, n2g_ref, rw_ref, rb_ref, h2p_ref, eidx_ref, ew_ref)


def _tok_specs():
    specs = [
        pl.BlockSpec((1, TM, D), lambda b, s: (b, s, 0)),
        pl.BlockSpec((TM, HP), lambda b, s: (b * NT + s, 0)),
        pl.BlockSpec((TOPK, TM), lambda b, s: (0, b * NT + s)),
        pl.BlockSpec((TOPK, TM), lambda b, s: (0, b * NT + s)),
    ]
    shapes = [
        jax.ShapeDtypeStruct((B, S, D), f32),
        jax.ShapeDtypeStruct((T, HP), u32),
        jax.ShapeDtypeStruct((TOPK, T), i32),
        jax.ShapeDtypeStruct((TOPK, T), f32),
    ]
    return specs, shapes


def _moe_in_specs():
    return [
        pl.BlockSpec((TOPK, TM, HP), lambda b, s: (0, b * NT + s, 0)),
        pl.BlockSpec((1, 6, D), lambda b, s: (b, 0, 0)),
    ]


def _conv_layer(x, moe_in, mod_l, n1g, win, b_in, wdw, bdw, lng, lnb, wout, n2g, rw, rb):
    has_moe_in = moe_in is not None
    in_specs = [pl.BlockSpec((1, TM, D), lambda b, s: (b, s, 0))]
    args = [x]
    if has_moe_in:
        in_specs += _moe_in_specs()
        args += list(moe_in)
    in_specs += [
        pl.BlockSpec((1, 6, D), lambda b, s: (b, 0, 0)),
        _full((1, D)), _full((D, 2 * C)), _full((1, 2 * C)), _full((HALO, C)), _full((1, C)),
        _full((1, C)), _full((1, C)), _full((C, D)), _full((1, D)), _full((2 * E, D)),
        _full((E, 1)),
    ]
    args += [mod_l, n1g, win, b_in, wdw, bdw, lng, lnb, wout, n2g, rw, rb]
    out_specs, out_shape = _tok_specs()
    return pl.pallas_call(
        functools.partial(_conv_kernel, has_moe_in),
        grid=(B, NT),
        in_specs=in_specs,
        out_specs=out_specs,
        out_shape=out_shape,
        scratch_shapes=[pltpu.VMEM((TM + HALO, C), f32), pltpu.VMEM((TM, C), f32)],
        compiler_params=_cp(2),
        name="conv_layer",
    )(*args)


def _preattn_kernel(with_kv, *refs):
    (x_ref, yg_ref, modp_ref, mod_ref, n1g_ref, wdq_ref, qng_ref, wuqt_ref, cos_ref,
     sin_ref) = refs[:10]
    refs = refs[10:]
    if with_kv:
        kvg_ref, wlat_ref, wropet_ref, kvng_ref, wuk_ref, wuvt_ref = refs[:6]
        refs = refs[6:]
        xo_ref, qt_ref, kc_ref, vt_ref = refs
    else:
        xo_ref, qt_ref = refs

    x = _moe_combine(x_ref, yg_ref, modp_ref)
    xo_ref[0] = x
    cos = cos_ref[0]
    sin = sin_ref[0]
    sh1 = mod_ref[0, 0:1, :]
    sc1 = mod_ref[0, 1:2, :]
    h = (_rms(x, n1g_ref[...]) * (1.0 + sc1) + sh1).astype(bf16)
    cq = jnp.dot(h, wdq_ref[...], preferred_element_type=f32)
    cq = _rms(cq, qng_ref[...]).astype(bf16)
    qt = _dot_nt(wuqt_ref[...], cq)
    hr = DR // 2
    for hh in range(H):
        r0 = hh * DQK
        x1 = qt[r0 + DN:r0 + DN + hr]
        x2 = qt[r0 + DN + hr:r0 + DQK]
        qt_ref[0, hh, 0:DN, :] = (qt[r0:r0 + DN] * ATTN_SCALE).astype(bf16)
        qt_ref[0, hh, DN:DN + hr, :] = ((x1 * cos - x2 * sin) * ATTN_SCALE).astype(bf16)
        qt_ref[0, hh, DN + hr:DQK, :] = ((x1 * sin + x2 * cos) * ATTN_SCALE).astype(bf16)

    if with_kv:
        xn = _rms(x, kvg_ref[...]).astype(bf16)
        lat = jnp.dot(xn, wlat_ref[...], preferred_element_type=f32)
        latn = _rms(lat, kvng_ref[...]).astype(bf16)
        krt = _dot_nt(wropet_ref[...], xn)
        k1 = krt[:hr]
        k2 = krt[hr:]
        kr = jnp.concatenate([k1 * cos - k2 * sin, k1 * sin + k2 * cos], axis=0)
        kr = kr.T.astype(bf16)
        kn = jnp.dot(latn, wuk_ref[...], preferred_element_type=f32).astype(bf16)
        vt = _dot_nt(wuvt_ref[...], latn).astype(bf16)
        for hh in range(H):
            kc_ref[0, hh, :, 0:DN] = kn[:, hh * DN:(hh + 1) * DN]
            kc_ref[0, hh, :, DN:DQK] = kr
            vt_ref[0, hh] = vt[hh * DV:(hh + 1) * DV]


def _preattn(x, moe_in, mod_l, n1g, wdq, qng, wuqt, cos_t, sin_t, kv=None):
    with_kv = kv is not None
    in_specs = [pl.BlockSpec((1, TM, D), lambda b, s: (b, s, 0))] + _moe_in_specs() + [
        pl.BlockSpec((1, 6, D), lambda b, s: (b, 0, 0)),
        _full((1, D)), _full((D, QR)), _full((1, QR)), _full((H * DQK, QR)),
        pl.BlockSpec((1, DR // 2, TM), lambda b, s: (b, 0, s)),
        pl.BlockSpec((1, DR // 2, TM), lambda b, s: (b, 0, s)),
    ]
    args = [x, *moe_in, mod_l, n1g, wdq, qng, wuqt, cos_t, sin_t]
    out_specs = [
        pl.BlockSpec((1, TM, D), lambda b, s: (b, s, 0)),
        pl.BlockSpec((1, H, DQK, TM), lambda b, s: (b, 0, 0, s)),
    ]
    out_shape = [
        jax.ShapeDtypeStruct((B, S, D), f32),
        jax.ShapeDtypeStruct((B, H, DQK, S), bf16),
    ]
    if with_kv:
        in_specs += [_full((1, D)), _full((D, KVR)), _full((DR, D)), _full((1, KVR)),
                     _full((KVR, H * DN)), _full((H * DV, KVR))]
        args += list(kv)
        out_specs += [
            pl.BlockSpec((1, H, TM, DQK), lambda b, s: (b, 0, s, 0)),
            pl.BlockSpec((1, H, DV, TM), lambda b, s: (b, 0, 0, s)),
        ]
        out_shape += [
            jax.ShapeDtypeStruct((B, H, S, DQK), bf16),
            jax.ShapeDtypeStruct((B, H, DV, S), bf16),
        ]
    return pl.pallas_call(
        functools.partial(_preattn_kernel, with_kv),
        grid=(B, NT),
        in_specs=in_specs,
        out_specs=out_specs,
        out_shape=out_shape,
        compiler_params=_cp(2),
        name="preattn_kv" if with_kv else "preattn",
    )(*args)


def _attn_kernel(qt_ref, kc_ref, vt_ref, o_ref, m_ref, l_ref, acc_ref):
    qi = pl.program_id(2)
    q = qt_ref[0, 0]
    m_ref[...] = jnp.full((1, TQ), NEG, f32)
    l_ref[...] = jnp.zeros((1, TQ), f32)
    acc_ref[...] = jnp.zeros((DV, TQ), f32)

    def step(j, masked):
        k0 = pl.multiple_of(j * TK, TK)
        k = kc_ref[0, 0, pl.ds(k0, TK), :]
        s = jnp.dot(k, q, preferred_element_type=f32)
        if masked:
            kpos = k0 + lax.broadcasted_iota(i32, (TK, TQ), 0)
            qpos = qi * TQ + lax.broadcasted_iota(i32, (TK, TQ), 1)
            s = jnp.where(kpos <= qpos, s, NEG)
        m_old = m_ref[...]
        m_new = jnp.maximum(m_old, jnp.max(s, axis=0, keepdims=True))
        alpha = jnp.exp(m_old - m_new)
        p = jnp.exp(s - m_new)
        l_ref[...] = alpha * l_ref[...] + jnp.sum(p, axis=0, keepdims=True)
        v = vt_ref[0, 0, :, pl.ds(k0, TK)]
        acc_ref[...] = alpha * acc_ref[...] + jnp.dot(
            v, p.astype(bf16), preferred_element_type=f32)
        m_ref[...] = m_new

    def body(j, carry):
        step(j, False)
        return carry

    lax.fori_loop(0, qi, body, 0)
    step(qi, True)
    o = acc_ref[...] / l_ref[...]
    o_ref[0] = o.T.astype(bf16)


def _attention(qt, kc, vt):
    return pl.pallas_call(
        _attn_kernel,
        grid=(B, H, S // TQ),
        in_specs=[
            pl.BlockSpec((1, 1, DQK, TQ), lambda b, h, q: (b, h, 0, q)),
            pl.BlockSpec((1, 1, S, DQK), lambda b, h, q: (b, h, 0, 0)),
            pl.BlockSpec((1, 1, DV, S), lambda b, h, q: (b, h, 0, 0)),
        ],
        out_specs=pl.BlockSpec((1, TQ, DV), lambda b, h, q: (b, q, h)),
        out_shape=jax.ShapeDtypeStruct((B, S, H * DV), bf16),
        scratch_shapes=[pltpu.VMEM((1, TQ), f32), pltpu.VMEM((1, TQ), f32),
                        pltpu.VMEM((DV, TQ), f32)],
        compiler_params=_cp(3),
        name="attn",
    )(qt, kc, vt)


def _postattn_kernel(x_ref, o_ref, mod_ref, wo_ref, n2g_ref, rw_ref, rb_ref,
                     xo_ref, h2p_ref, eidx_ref, ew_ref):
    g1 = mod_ref[0, 2:3, :]
    mix = jnp.dot(o_ref[0], wo_ref[...], preferred_element_type=f32)
    x_new = x_ref[0] + g1 * mix
    xo_ref[0] = x_new
    _route_and_pack(x_new, mod_ref, n2g_ref, rw_ref, rb_ref, h2p_ref, eidx_ref, ew_ref)


def _postattn(x, o, mod_l, wo, n2g, rw, rb):
    out_specs, out_shape = _tok_specs()
    return pl.pallas_call(
        _postattn_kernel,
        grid=(B, NT),
        in_specs=[
            pl.BlockSpec((1, TM, D), lambda b, s: (b, s, 0)),
            pl.BlockSpec((1, TM, H * DV), lambda b, s: (b, s, 0)),
            pl.BlockSpec((1, 6, D), lambda b, s: (b, 0, 0)),
            _full((H * DV, D)), _full((1, D)), _full((2 * E, D)), _full((E, 1)),
        ],
        out_specs=out_specs,
        out_shape=out_shape,
        compiler_params=_cp(2),
        name="postattn",
    )(x, o, mod_l, wo, n2g, rw, rb)


def _row_copy(src_ref, dst_ref, sem, src_row, dst_row):
    return pltpu.make_async_copy(
        src_ref.at[pl.ds(src_row, 1)], dst_ref.at[pl.ds(dst_row, 1)], sem)


def _gather_kernel(idx_ref, src_ref, dst_ref, sem):
    def issue(r, carry):
        _row_copy(src_ref, dst_ref, sem, idx_ref[r], r).start()
        return carry

    lax.fori_loop(0, GR, issue, 0, unroll=8)

    def drain(r, carry):
        _row_copy(src_ref, dst_ref, sem, 0, r).wait()
        return carry

    lax.fori_loop(0, GR, drain, 0, unroll=8)


def _gather_rows(src, idx):
    n = idx.shape[0]
    return pl.pallas_call(
        _gather_kernel,
        grid=(n // GR,),
        in_specs=[
            pl.BlockSpec((GR,), lambda i: (i,), memory_space=pltpu.SMEM),
            pl.BlockSpec(memory_space=pl.ANY),
        ],
        out_specs=pl.BlockSpec((GR, src.shape[1]), lambda i: (i, 0)),
        out_shape=jax.ShapeDtypeStruct((n, src.shape[1]), src.dtype),
        scratch_shapes=[pltpu.SemaphoreType.DMA],
        compiler_params=_cp(1),
        name="gather_rows",
    )(idx, src)


def _expert_kernel(be_ref, first_ref, xb_ref, bw_ref, wg_ref, wu_ref, wd_ref, yb_ref,
                   wgu_bf, wd_bf):
    i = pl.program_id(0)

    @pl.when(first_ref[i] == 1)
    def _():
        wgu_bf[:, :DE] = wg_ref[0].astype(bf16)
        wgu_bf[:, DE:] = wu_ref[0].astype(bf16)
        wd_bf[...] = wd_ref[0].astype(bf16)

    x = _unpack_rows(xb_ref[...]).astype(bf16)
    gu = jnp.dot(x, wgu_bf[...], preferred_element_type=f32)
    g = gu[:, :DE]
    hmid = (g * jax.nn.sigmoid(g) * gu[:, DE:]).astype(bf16)
    y = jnp.dot(hmid, wd_bf[...], preferred_element_type=f32) * bw_ref[...]
    yb_ref[...] = _pack_pair(y[:, :HP], y[:, HP:])


def _experts(blk_e, blk_first, xb, bw, wg, wu, wd):
    grid_spec = pltpu.PrefetchScalarGridSpec(
        num_scalar_prefetch=2,
        grid=(NB,),
        in_specs=[
            pl.BlockSpec((RB, HP), lambda i, be, bf: (i, 0)),
            pl.BlockSpec((RB, 1), lambda i, be, bf: (i, 0)),
            pl.BlockSpec((1, D, DE), lambda i, be, bf: (be[i], 0, 0)),
            pl.BlockSpec((1, D, DE), lambda i, be, bf: (be[i], 0, 0)),
            pl.BlockSpec((1, DE, D), lambda i, be, bf: (be[i], 0, 0)),
        ],
        out_specs=pl.BlockSpec((RB, HP), lambda i, be, bf: (i, 0)),
        scratch_shapes=[pltpu.VMEM((D, 2 * DE), bf16), pltpu.VMEM((DE, D), bf16)],
    )
    return pl.pallas_call(
        _expert_kernel,
        grid_spec=grid_spec,
        out_shape=jax.ShapeDtypeStruct((N_ROWS, HP), u32),
        compiler_params=_cp(1),
        name="experts",
    )(blk_e, blk_first, xb, bw, wg, wu, wd)


def _final_kernel(x_ref, yg_ref, modp_ref, g_ref, o_ref):
    o_ref[0] = _rms(_moe_combine(x_ref, yg_ref, modp_ref), g_ref[...])


def _final(x, moe_in, g):
    return pl.pallas_call(
        _final_kernel,
        grid=(B, NT),
        in_specs=[pl.BlockSpec((1, TM, D), lambda b, s: (b, s, 0))] + _moe_in_specs()
        + [_full((1, D))],
        out_specs=pl.BlockSpec((1, TM, D), lambda b, s: (b, s, 0)),
        out_shape=jax.ShapeDtypeStruct((B, S, D), f32),
        compiler_params=_cp(2),
        name="final_norm",
    )(x, *moe_in, g)


def _moe(h2p, eidx, ew, wg, wu, wd):
    flat_e = eidx.reshape(-1)
    onehot = (flat_e[:, None] == jnp.arange(E, dtype=i32)[None, :]).astype(i32)
    csum = jnp.cumsum(onehot, axis=0)
    counts = csum[-1]
    rank = jnp.take_along_axis(csum, flat_e[:, None], axis=1)[:, 0] - 1
    padded = ((counts + RB - 1) // RB) * RB
    pad_end = jnp.cumsum(padded)
    pad_start = pad_end - padded
    dest = (pad_start[flat_e] + rank).astype(i32)
    tok = jnp.tile(jnp.arange(T, dtype=i32), TOPK)
    buf_tok = jnp.zeros((N_ROWS,), i32).at[dest].set(tok)
    buf_w = jnp.zeros((N_ROWS,), f32).at[dest].set(ew.reshape(-1))
    blk_start = jnp.arange(NB, dtype=i32) * RB
    blk_e = jnp.minimum(jnp.searchsorted(pad_end, blk_start, side="right"), E - 1).astype(i32)
    blk_first = jnp.concatenate(
        [jnp.ones((1,), i32), (blk_e[1:] != blk_e[:-1]).astype(i32)])

    xb = _gather_rows(h2p, buf_tok)
    yb = _experts(blk_e, blk_first, xb, buf_w[:, None], wg, wu, wd)
    yg = _gather_rows(yb, dest)
    return yg.reshape(TOPK, T, HP)


def kernel(x, c, positions, ada_w, ada_b, norm1_g, norm2_g, conv_w_in, conv_b_in, conv_w_dw,
           conv_b_dw, conv_ln_g, conv_ln_b, conv_w_out, kv_in_g, w_dkv, kv_norm_g, w_ukv, w_dq,
           q_norm_g, w_uq, w_o, router_w, router_b, exp_w_gate, exp_w_up, exp_w_down, final_g):
    c8 = jnp.pad(c, ((0, 8 - B), (0, 0)))
    mod = _ada(c8, ada_w, ada_b[:, None, :])[:, :B].reshape(DEPTH, B, 6, D)

    invf = (ROPE_THETA ** (-jnp.arange(0, DR, 2, dtype=f32) / DR))[:, None]
    cos_t, sin_t = _rope_tables(positions[:, None, :], invf)

    rwt = router_w.astype(f32).T
    rw_hi = rwt.astype(bf16)
    rw_lo = (rwt - rw_hi.astype(f32)).astype(bf16)
    rw = jnp.concatenate([rw_hi, rw_lo], axis=0)
    rb = router_b.astype(f32)[:, None]

    w_ukv3 = w_ukv.reshape(KVR, H, DN + DV)
    wuk = w_ukv3[:, :, :DN].reshape(KVR, H * DN).astype(bf16)
    wuvt = w_ukv3[:, :, DN:].reshape(KVR, H * DV).T.astype(bf16)
    kv_w = (kv_in_g[None, :], w_dkv[:, :KVR].astype(bf16), w_dkv[:, KVR:].T.astype(bf16),
            kv_norm_g[None, :], wuk, wuvt)

    moe_in = None
    kc = vt = None
    for l in range(DEPTH):
        if l < N_A:
            wdw = jnp.pad(conv_w_dw[l], ((0, HALO - CONV_W), (0, 0)))
            x, h2p, eidx, ew = _conv_layer(
                x, moe_in, mod[l], norm1_g[l][None, :], conv_w_in[l].astype(bf16),
                conv_b_in[l][None, :], wdw, conv_b_dw[l][None, :], conv_ln_g[l][None, :],
                conv_ln_b[l][None, :], conv_w_out[l].astype(bf16), norm2_g[l][None, :], rw, rb)
        else:
            j = l - N_A
            outs = _preattn(
                x, moe_in, mod[l], norm1_g[l][None, :], w_dq[j].astype(bf16),
                q_norm_g[j][None, :], w_uq[j].T.astype(bf16), cos_t, sin_t,
                kv=kv_w if j == 0 else None)
            if j == 0:
                x, qt, kc, vt = outs
            else:
                x, qt = outs
            o = _attention(qt, kc, vt)
            x, h2p, eidx, ew = _postattn(
                x, o, mod[l], w_o[j].astype(bf16), norm2_g[l][None, :], rw, rb)
        yg = _moe(h2p, eidx, ew, exp_w_gate[l], exp_w_up[l], exp_w_down[l])
        moe_in = (yg, mod[l])
    return _final(x, moe_in, final_g[None, :])
```

```python
import functools
import math

import jax
import jax.numpy as jnp
from jax import lax
from jax.experimental import pallas as pl
from jax.experimental.pallas import tpu as pltpu

f32 = jnp.float32
bf16 = jnp.bfloat16
u32 = jnp.uint32
i32 = jnp.int32

D = 1024
B = 4
S = 4096
T = B * S
DEPTH = 4
N_A = 2
CONV_W = 31
C = 1024
H = 8
DN = 128
DR = 64
DV = 128
DQK = DN + DR
QR = 384
KVR = 256
ROPE_THETA = 10000.0
ATTN_SCALE = 1.0 / math.sqrt(DN + DR)
E = 16
NG = 4
EG = 4
TOPK = 2
DE = 512
EPS = 1e-6
NEG = -1e30

TM = 512
NT = S // TM
HALO = 32
RC = 64
TQ = 512
TK = 512
RB = 256
N_ROWS = T * TOPK + E * RB
NB = N_ROWS // RB
GR = 2048
HP = D // 2

VMEM_LIMIT = 56 * 1024 * 1024


def _cp(n_axes):
    return pltpu.CompilerParams(
        dimension_semantics=("arbitrary",) * n_axes, vmem_limit_bytes=VMEM_LIMIT)


def _full(shape):
    n = len(shape)
    return pl.BlockSpec(shape, lambda *_: (0,) * n)


def _rms(x, g):
    return x * lax.rsqrt(jnp.mean(x * x, axis=-1, keepdims=True) + EPS) * g


def _pack_pair(a, b):
    ua = lax.bitcast_convert_type(a.astype(bf16).astype(f32), u32)
    ub = lax.bitcast_convert_type(b.astype(bf16).astype(f32), u32)
    return (ua >> 16) | ub


def _unpack_pair(w):
    a = lax.bitcast_convert_type(w << 16, f32)
    b = lax.bitcast_convert_type(w & jnp.uint32(0xFFFF0000), f32)
    return a, b


def _unpack_rows(w):
    a, b = _unpack_pair(w)
    return jnp.concatenate([a, b], axis=1)


def _dot_nt(a, b):
    return lax.dot_general(a, b, (((1,), (1,)), ((), ())), preferred_element_type=f32)


def _top2sum4(a, b, c, d):
    lo1, hi1 = jnp.minimum(a, b), jnp.maximum(a, b)
    lo2, hi2 = jnp.minimum(c, d), jnp.maximum(c, d)
    return jnp.maximum(hi1, hi2) + jnp.maximum(jnp.minimum(hi1, hi2), jnp.maximum(lo1, lo2))


def _route_and_pack(x_new, mod_ref, n2g_ref, rw_ref, rb_ref, h2p_ref, eidx_ref, ewt_ref):
    sh2 = mod_ref[0, 3:4, :]
    sc2 = mod_ref[0, 4:5, :]
    h2 = _rms(x_new, n2g_ref[...]) * (1.0 + sc2) + sh2
    h2p_ref[...] = _pack_pair(h2[:, :HP], h2[:, HP:])

    hi = h2.astype(bf16)
    lo = (h2 - hi.astype(f32)).astype(bf16)
    rw = rw_ref[...]
    p = _dot_nt(rw, hi) + _dot_nt(rw, lo)
    logits = p[:E] + p[E:]
    s = jax.nn.sigmoid(logits)
    sb = s + rb_ref[...]

    srow = [s[e:e + 1] for e in range(E)]
    brow = [sb[e:e + 1] for e in range(E)]
    gs = [_top2sum4(*brow[EG * g:EG * g + EG]) for g in range(NG)]
    gidx = jnp.zeros_like(gs[0], dtype=i32)
    best = gs[0]
    for g in range(1, NG):
        upd = gs[g] > best
        gidx = jnp.where(upd, g, gidx)
        best = jnp.where(upd, gs[g], best)

    def pick(rows, i):
        out = rows[i]
        for g in range(1, NG):
            out = jnp.where(gidx == g, rows[EG * g + i], out)
        return out

    wv = [pick(brow, i) for i in range(EG)]
    sv = [pick(srow, i) for i in range(EG)]

    l1 = jnp.zeros_like(gidx)
    b1 = wv[0]
    for i in range(1, EG):
        upd = wv[i] > b1
        l1 = jnp.where(upd, i, l1)
        b1 = jnp.where(upd, wv[i], b1)
    l2 = jnp.zeros_like(gidx)
    b2 = jnp.where(l1 == 0, -jnp.inf, wv[0])
    for i in range(1, EG):
        cand = jnp.where(l1 == i, -jnp.inf, wv[i])
        upd = cand > b2
        l2 = jnp.where(upd, i, l2)
        b2 = jnp.where(upd, cand, b2)

    def sel(loc):
        out = sv[0]
        for i in range(1, EG):
            out = jnp.where(loc == i, sv[i], out)
        return out

    w1 = sel(l1)
    w2 = sel(l2)
    tot = w1 + w2
    eidx_ref[...] = jnp.concatenate([gidx * EG + l1, gidx * EG + l2], axis=0)
    wt = jnp.concatenate([w1 / tot, w2 / tot, jnp.zeros((128 - TOPK, TM), f32)], axis=0)
    ewt_ref[...] = wt.T


def _moe_combine(x_ref, yg_ref, ewt_ref, modp_ref):
    y0 = _unpack_rows(yg_ref[0])
    y1 = _unpack_rows(yg_ref[1])
    w = ewt_ref[...]
    g2 = modp_ref[0, 5:6, :]
    return x_ref[0] + g2 * (w[:, 0:1] * y0 + w[:, 1:2] * y1)


def _ada_kernel(c_ref, w_ref, b_ref, o_ref):
    c = c_ref[...]
    ca = (c * jax.nn.sigmoid(c)).astype(bf16)
    o_ref[0] = jnp.dot(ca, w_ref[0].astype(bf16), preferred_element_type=f32) + b_ref[0]


def _ada(c8, ada_w, ada_b3):
    tn = 1536
    return pl.pallas_call(
        _ada_kernel,
        grid=(DEPTH, 6 * D // tn),
        in_specs=[
            pl.BlockSpec((8, D), lambda l, n: (0, 0)),
            pl.BlockSpec((1, D, tn), lambda l, n: (l, 0, n)),
            pl.BlockSpec((1, 1, tn), lambda l, n: (l, 0, n)),
        ],
        out_specs=pl.BlockSpec((1, 8, tn), lambda l, n: (l, 0, n)),
        out_shape=jax.ShapeDtypeStruct((DEPTH, 8, 6 * D), f32),
        compiler_params=_cp(2),
        name="ada",
    )(c8, ada_w, ada_b3)


def _rope_kernel(pos_ref, invf_ref, cos_ref, sin_ref):
    ang = pos_ref[0].astype(f32) * invf_ref[...]
    cos_ref[0] = jnp.cos(ang)
    sin_ref[0] = jnp.sin(ang)


def _rope_tables(pos3, invf):
    return pl.pallas_call(
        _rope_kernel,
        grid=(B,),
        in_specs=[pl.BlockSpec((1, 1, S), lambda b: (b, 0, 0)), _full((DR // 2, 1))],
        out_specs=[pl.BlockSpec((1, DR // 2, S), lambda b: (b, 0, 0))] * 2,
        out_shape=[jax.ShapeDtypeStruct((B, DR // 2, S), f32)] * 2,
        compiler_params=_cp(1),
        name="rope",
    )(pos3, invf)


def _conv_kernel(has_moe_in, *refs):
    if has_moe_in:
        x_ref, yg_ref, ewp_ref, modp_ref = refs[:4]
        refs = refs[4:]
    else:
        x_ref = refs[0]
        refs = refs[1:]
    (mod_ref, n1g_ref, win_ref, bin_ref, wdw_ref, bdw_ref, lng_ref, lnb_ref, wout_ref,
     n2g_ref, rw_ref, rb_ref,
     xo_ref, h2p_ref, eidx_ref, ewt_ref, ubuf, cbuf) = refs
    si = pl.program_id(1)

    x = _moe_combine(x_ref, yg_ref, ewp_ref, modp_ref) if has_moe_in else x_ref[0]
    sh1 = mod_ref[0, 0:1, :]
    sc1 = mod_ref[0, 1:2, :]
    g1 = mod_ref[0, 2:3, :]
    h = (_rms(x, n1g_ref[...]) * (1.0 + sc1) + sh1).astype(bf16)
    u = jnp.dot(h, win_ref[...], preferred_element_type=f32) + bin_ref[...]
    glu = u[:, :C] * jax.nn.sigmoid(u[:, C:])

    @pl.when(si == 0)
    def _():
        ubuf[0:HALO, :] = jnp.zeros((HALO, C), f32)

    @pl.when(si > 0)
    def _():
        ubuf[0:HALO, :] = ubuf[TM:TM + HALO, :]

    ubuf[HALO:HALO + TM, :] = glu

    off = HALO - (CONV_W - 1)

    def chunk(ci, carry):
        c0 = pl.multiple_of(ci * 128, 128)
        for r in range(TM // RC):
            acc = None
            for j in range(CONV_W):
                win = ubuf[r * RC + off + j:r * RC + off + j + RC, pl.ds(c0, 128)]
                term = win * wdw_ref[j:j + 1, pl.ds(c0, 128)]
                acc = term if acc is None else acc + term
            cbuf[r * RC:(r + 1) * RC, pl.ds(c0, 128)] = acc + bdw_ref[:, pl.ds(c0, 128)]
        return carry

    lax.fori_loop(0, C // 128, chunk, 0)

    v = cbuf[...]
    mu = jnp.mean(v, axis=-1, keepdims=True)
    vc = v - mu
    var = jnp.mean(vc * vc, axis=-1, keepdims=True)
    y = vc * lax.rsqrt(var + EPS) * lng_ref[...] + lnb_ref[...]
    y = (y * jax.nn.sigmoid(y)).astype(bf16)
    mix = jnp.dot(y, wout_ref[...], preferred_element_type=f32)
    x_new = x + g1 * mix
    xo_ref[0] = x_new
    _route_and_pack(x_new, mod_ref, n2g_ref, rw_ref, rb_ref, h2p_ref, eidx_ref, ewt_ref)


def _tok_specs():
    specs = [
        pl.BlockSpec((1, TM, D), lambda b, s: (b, s, 0)),
        pl.BlockSpec((TM, HP), lambda b, s: (b * NT + s, 0)),
        pl.BlockSpec((TOPK, TM), lambda b, s: (0, b * NT + s)),
        pl.BlockSpec((TM, 128), lambda b, s: (b * NT + s, 0)),
    ]
    shapes = [
        jax.ShapeDtypeStruct((B, S, D), f32),
        jax.ShapeDtypeStruct((T, HP), u32),
        jax.ShapeDtypeStruct((TOPK, T), i32),
        jax.ShapeDtypeStruct((T, 128), f32),
    ]
    return specs, shapes


def _moe_in_specs():
    return [
        pl.BlockSpec((TOPK, TM, HP), lambda b, s: (0, b * NT + s, 0)),
        pl.BlockSpec((TM, 128), lambda b, s: (b * NT + s, 0)),
        pl.BlockSpec((1, 6, D), lambda b, s: (b, 0, 0)),
    ]


def _conv_layer(x, moe_in, mod_l, n1g, win, b_in, wdw, bdw, lng, lnb, wout, n2g, rw, rb):
    has_moe_in = moe_in is not None
    in_specs = [pl.BlockSpec((1, TM, D), lambda b, s: (b, s, 0))]
    args = [x]
    if has_moe_in:
        in_specs += _moe_in_specs()
        args += list(moe_in)
    in_specs += [
        pl.BlockSpec((1, 6, D), lambda b, s: (b, 0, 0)),
        _full((1, D)), _full((D, 2 * C)), _full((1, 2 * C)), _full((HALO, C)), _full((1, C)),
        _full((1, C)), _full((1, C)), _full((C, D)), _full((1, D)), _full((2 * E, D)),
        _full((E, 1)),
    ]
    args += [mod_l, n1g, win, b_in, wdw, bdw, lng, lnb, wout, n2g, rw, rb]
    out_specs, out_shape = _tok_specs()
    return pl.pallas_call(
        functools.partial(_conv_kernel, has_moe_in),
        grid=(B, NT),
        in_specs=in_specs,
        out_specs=out_specs,
        out_shape=out_shape,
        scratch_shapes=[pltpu.VMEM((TM + HALO, C), f32), pltpu.VMEM((TM, C), f32)],
        compiler_params=_cp(2),
        name="conv_layer",
    )(*args)


def _preattn_kernel(with_kv, *refs):
    (x_ref, yg_ref, ewp_ref, modp_ref, mod_ref, n1g_ref, wdq_ref, qng_ref, wuqt_ref, cos_ref,
     sin_ref) = refs[:11]
    refs = refs[11:]
    if with_kv:
        kvg_ref, wlat_ref, wropet_ref, kvng_ref, wuk_ref, wuvt_ref = refs[:6]
        refs = refs[6:]
        xo_ref, qt_ref, kc_ref, vt_ref = refs
    else:
        xo_ref, qt_ref = refs

    x = _moe_combine(x_ref, yg_ref, ewp_ref, modp_ref)
    xo_ref[0] = x
    cos = cos_ref[0]
    sin = sin_ref[0]
    sh1 = mod_ref[0, 0:1, :]
    sc1 = mod_ref[0, 1:2, :]
    h = (_rms(x, n1g_ref[...]) * (1.0 + sc1) + sh1).astype(bf16)
    cq = jnp.dot(h, wdq_ref[...], preferred_element_type=f32)
    cq = _rms(cq, qng_ref[...]).astype(bf16)
    qt = _dot_nt(wuqt_ref[...], cq)
    hr = DR // 2
    for hh in range(H):
        r0 = hh * DQK
        x1 = qt[r0 + DN:r0 + DN + hr]
        x2 = qt[r0 + DN + hr:r0 + DQK]
        qt_ref[0, hh, 0:DN, :] = (qt[r0:r0 + DN] * ATTN_SCALE).astype(bf16)
        qt_ref[0, hh, DN:DN + hr, :] = ((x1 * cos - x2 * sin) * ATTN_SCALE).astype(bf16)
        qt_ref[0, hh, DN + hr:DQK, :] = ((x1 * sin + x2 * cos) * ATTN_SCALE).astype(bf16)

    if with_kv:
        xn = _rms(x, kvg_ref[...]).astype(bf16)
        lat = jnp.dot(xn, wlat_ref[...], preferred_element_type=f32)
        latn = _rms(lat, kvng_ref[...]).astype(bf16)
        krt = _dot_nt(wropet_ref[...], xn)
        k1 = krt[:hr]
        k2 = krt[hr:]
        kr = jnp.concatenate([k1 * cos - k2 * sin, k1 * sin + k2 * cos], axis=0)
        kr = kr.T.astype(bf16)
        kn = jnp.dot(latn, wuk_ref[...], preferred_element_type=f32).astype(bf16)
        vt = _dot_nt(wuvt_ref[...], latn).astype(bf16)
        for hh in range(H):
            kc_ref[0, hh, :, 0:DN] = kn[:, hh * DN:(hh + 1) * DN]
            kc_ref[0, hh, :, DN:DQK] = kr
            vt_ref[0, hh] = vt[hh * DV:(hh + 1) * DV]


def _preattn(x, moe_in, mod_l, n1g, wdq, qng, wuqt, cos_t, sin_t, kv=None):
    with_kv = kv is not None
    in_specs = [pl.BlockSpec((1, TM, D), lambda b, s: (b, s, 0))] + _moe_in_specs() + [
        pl.BlockSpec((1, 6, D), lambda b, s: (b, 0, 0)),
        _full((1, D)), _full((D, QR)), _full((1, QR)), _full((H * DQK, QR)),
        pl.BlockSpec((1, DR // 2, TM), lambda b, s: (b, 0, s)),
        pl.BlockSpec((1, DR // 2, TM), lambda b, s: (b, 0, s)),
    ]
    args = [x, *moe_in, mod_l, n1g, wdq, qng, wuqt, cos_t, sin_t]
    out_specs = [
        pl.BlockSpec((1, TM, D), lambda b, s: (b, s, 0)),
        pl.BlockSpec((1, H, DQK, TM), lambda b, s: (b, 0, 0, s)),
    ]
    out_shape = [
        jax.ShapeDtypeStruct((B, S, D), f32),
        jax.ShapeDtypeStruct((B, H, DQK, S), bf16),
    ]
    if with_kv:
        in_specs += [_full((1, D)), _full((D, KVR)), _full((DR, D)), _full((1, KVR)),
                     _full((KVR, H * DN)), _full((H * DV, KVR))]
        args += list(kv)
        out_specs += [
            pl.BlockSpec((1, H, TM, DQK), lambda b, s: (b, 0, s, 0)),
            pl.BlockSpec((1, H, DV, TM), lambda b, s: (b, 0, 0, s)),
        ]
        out_shape += [
            jax.ShapeDtypeStruct((B, H, S, DQK), bf16),
            jax.ShapeDtypeStruct((B, H, DV, S), bf16),
        ]
    return pl.pallas_call(
        functools.partial(_preattn_kernel, with_kv),
        grid=(B, NT),
        in_specs=in_specs,
        out_specs=out_specs,
        out_shape=out_shape,
        compiler_params=_cp(2),
        name="preattn_kv" if with_kv else "preattn",
    )(*args)


def _attn_kernel(qt_ref, kc_ref, vt_ref, o_ref, m_ref, l_ref, acc_ref):
    qi = pl.program_id(2)
    q = qt_ref[0, 0]
    m_ref[...] = jnp.full((1, TQ), NEG, f32)
    l_ref[...] = jnp.zeros((1, TQ), f32)
    acc_ref[...] = jnp.zeros((DV, TQ), f32)

    def step(j, masked):
        k0 = pl.multiple_of(j * TK, TK)
        k = kc_ref[0, 0, pl.ds(k0, TK), :]
        s = jnp.dot(k, q, preferred_element_type=f32)
        if masked:
            kpos = k0 + lax.broadcasted_iota(i32, (TK, TQ), 0)
            qpos = qi * TQ + lax.broadcasted_iota(i32, (TK, TQ), 1)
            s = jnp.where(kpos <= qpos, s, NEG)
        m_old = m_ref[...]
        m_new = jnp.maximum(m_old, jnp.max(s, axis=0, keepdims=True))
        alpha = jnp.exp(m_old - m_new)
        p = jnp.exp(s - m_new)
        l_ref[...] = alpha * l_ref[...] + jnp.sum(p, axis=0, keepdims=True)
        v = vt_ref[0, 0, :, pl.ds(k0, TK)]
        acc_ref[...] = alpha * acc_ref[...] + jnp.dot(
            v, p.astype(bf16), preferred_element_type=f32)
        m_ref[...] = m_new

    def body(j, carry):
        step(j, False)
        return carry

    lax.fori_loop(0, qi, body, 0)
    step(qi, True)
    o = acc_ref[...] / l_ref[...]
    o_ref[0] = o.T.astype(bf16)


def _attention(qt, kc, vt):
    return pl.pallas_call(
        _attn_kernel,
        grid=(B, H, S // TQ),
        in_specs=[
            pl.BlockSpec((1, 1, DQK, TQ), lambda b, h, q: (b, h, 0, q)),
            pl.BlockSpec((1, 1, S, DQK), lambda b, h, q: (b, h, 0, 0)),
            pl.BlockSpec((1, 1, DV, S), lambda b, h, q: (b, h, 0, 0)),
        ],
        out_specs=pl.BlockSpec((1, TQ, DV), lambda b, h, q: (b, q, h)),
        out_shape=jax.ShapeDtypeStruct((B, S, H * DV), bf16),
        scratch_shapes=[pltpu.VMEM((1, TQ), f32), pltpu.VMEM((1, TQ), f32),
                        pltpu.VMEM((DV, TQ), f32)],
        compiler_params=_cp(3),
        name="attn",
    )(qt, kc, vt)


def _postattn_kernel(x_ref, o_ref, mod_ref, wo_ref, n2g_ref, rw_ref, rb_ref,
                     xo_ref, h2p_ref, eidx_ref, ewt_ref):
    g1 = mod_ref[0, 2:3, :]
    mix = jnp.dot(o_ref[0], wo_ref[...], preferred_element_type=f32)
    x_new = x_ref[0] + g1 * mix
    xo_ref[0] = x_new
    _route_and_pack(x_new, mod_ref, n2g_ref, rw_ref, rb_ref, h2p_ref, eidx_ref, ewt_ref)


def _postattn(x, o, mod_l, wo, n2g, rw, rb):
    out_specs, out_shape = _tok_specs()
    return pl.pallas_call(
        _postattn_kernel,
        grid=(B, NT),
        in_specs=[
            pl.BlockSpec((1, TM, D), lambda b, s: (b, s, 0)),
            pl.BlockSpec((1, TM, H * DV), lambda b, s: (b, s, 0)),
            pl.BlockSpec((1, 6, D), lambda b, s: (b, 0, 0)),
            _full((H * DV, D)), _full((1, D)), _full((2 * E, D)), _full((E, 1)),
        ],
        out_specs=out_specs,
        out_shape=out_shape,
        compiler_params=_cp(2),
        name="postattn",
    )(x, o, mod_l, wo, n2g, rw, rb)


def _row_copy(src_ref, dst_ref, sem, src_row, dst_row):
    return pltpu.make_async_copy(
        src_ref.at[pl.ds(src_row, 1)], dst_ref.at[pl.ds(dst_row, 1)], sem)


def _gather_kernel(idx_ref, src_ref, dst_ref, sem):
    def issue(r, carry):
        _row_copy(src_ref, dst_ref, sem, idx_ref[r], r).start()
        return carry

    lax.fori_loop(0, GR, issue, 0, unroll=8)

    def drain(r, carry):
        _row_copy(src_ref, dst_ref, sem, 0, r).wait()
        return carry

    lax.fori_loop(0, GR, drain, 0, unroll=8)


def _gather_rows(src, idx):
    n = idx.shape[0]
    return pl.pallas_call(
        _gather_kernel,
        grid=(n // GR,),
        in_specs=[
            pl.BlockSpec((GR,), lambda i: (i,), memory_space=pltpu.SMEM),
            pl.BlockSpec(memory_space=pl.ANY),
        ],
        out_specs=pl.BlockSpec((GR, src.shape[1]), lambda i: (i, 0)),
        out_shape=jax.ShapeDtypeStruct((n, src.shape[1]), src.dtype),
        scratch_shapes=[pltpu.SemaphoreType.DMA],
        compiler_params=_cp(1),
        name="gather_rows",
    )(idx, src)


def _expert_kernel(be_ref, first_ref, nv_ref, xb_ref, wg_ref, wu_ref, wd_ref, yb_ref,
                   wgu_bf, wd_bf):
    i = pl.program_id(0)
    nv = nv_ref[i]

    @pl.when(first_ref[i] == 1)
    def _():
        wgu_bf[:, :DE] = wg_ref[0].astype(bf16)
        wgu_bf[:, DE:] = wu_ref[0].astype(bf16)
        wd_bf[...] = wd_ref[0].astype(bf16)

    @pl.when(nv > 0)
    def _():
        x = _unpack_rows(xb_ref[...]).astype(bf16)
        gu = jnp.dot(x, wgu_bf[...], preferred_element_type=f32)
        g = gu[:, :DE]
        hmid = (g * jax.nn.sigmoid(g) * gu[:, DE:]).astype(bf16)
        y = jnp.dot(hmid, wd_bf[...], preferred_element_type=f32)
        yb_ref[...] = _pack_pair(y[:, :HP], y[:, HP:])

    @pl.when(nv == 0)
    def _():
        yb_ref[...] = jnp.zeros((RB, HP), u32)


def _experts(blk_e, blk_first, blk_nv, xb, wg, wu, wd):
    grid_spec = pltpu.PrefetchScalarGridSpec(
        num_scalar_prefetch=3,
        grid=(NB,),
        in_specs=[
            pl.BlockSpec((RB, HP), lambda i, be, bf, nv: (i, 0)),
            pl.BlockSpec((1, D, DE), lambda i, be, bf, nv: (be[i], 0, 0)),
            pl.BlockSpec((1, D, DE), lambda i, be, bf, nv: (be[i], 0, 0)),
            pl.BlockSpec((1, DE, D), lambda i, be, bf, nv: (be[i], 0, 0)),
        ],
        out_specs=pl.BlockSpec((RB, HP), lambda i, be, bf, nv: (i, 0)),
        scratch_shapes=[pltpu.VMEM((D, 2 * DE), bf16), pltpu.VMEM((DE, D), bf16)],
    )
    return pl.pallas_call(
        _expert_kernel,
        grid_spec=grid_spec,
        out_shape=jax.ShapeDtypeStruct((N_ROWS, HP), u32),
        compiler_params=_cp(1),
        name="experts",
    )(blk_e, blk_first, blk_nv, xb, wg, wu, wd)


PCH = 256
NBP = 256


def _route_pos_kernel(eidx_ref, dest_ref, blk_ref, pre_ref):
    upper = (lax.broadcasted_iota(i32, (PCH, PCH), 0)
             < lax.broadcasted_iota(i32, (PCH, PCH), 1)).astype(bf16)
    eio = lax.broadcasted_iota(i32, (E, PCH), 0)

    def onehot(c0, width, eiota):
        e0 = eidx_ref[0:1, pl.ds(c0, width)]
        e1 = eidx_ref[1:2, pl.ds(c0, width)]
        return jnp.concatenate([e0 == eiota, e1 == eiota], axis=0)

    ones_r = jnp.ones((8, PCH), bf16)

    def count(c, carry):
        col, row = carry
        c0 = pl.multiple_of(c * PCH, PCH)
        oh = onehot(c0, PCH, eio).astype(bf16)
        pre_ref[:, pl.ds(c0, PCH)] = col + jnp.dot(oh, upper, preferred_element_type=f32)
        col = col + jnp.sum(oh.astype(f32), axis=1, keepdims=True)
        row = row + _dot_nt(ones_r, oh[:E]) + _dot_nt(ones_r, oh[E:])
        return col, row

    tot_c, tot_r = lax.fori_loop(
        0, T // PCH, count, (jnp.zeros((2 * E, 1), f32), jnp.zeros((8, E), f32)))
    cnt0_c = tot_c[:E]
    cnt_c = tot_c[:E] + tot_c[E:]
    shift = RB.bit_length() - 1
    assert RB == 1 << shift
    padded_r = lax.shift_left(
        lax.shift_right_logical(tot_r[0:1].astype(i32) + (RB - 1), shift), shift).astype(f32)
    ee = lax.broadcasted_iota(i32, (E, E), 0)
    ep = lax.broadcasted_iota(i32, (E, E), 1)
    start_col = jnp.sum(jnp.where(ep < ee, padded_r, 0.0), axis=1, keepdims=True)
    end_col = jnp.sum(jnp.where(ep <= ee, padded_r, 0.0), axis=1, keepdims=True)
    vend_col = start_col + cnt_c
    basef = jnp.concatenate([start_col, start_col + cnt0_c], axis=0)

    wide = 2048
    eiow = lax.broadcasted_iota(i32, (E, wide), 0)

    def place(c, carry):
        c0 = pl.multiple_of(c * wide, wide)
        oh = onehot(c0, wide, eiow)
        val = jnp.where(oh, pre_ref[:, pl.ds(c0, wide)] + basef, 0.0)
        d0 = jnp.sum(val[:E], axis=0, keepdims=True)
        d1 = jnp.sum(val[E:], axis=0, keepdims=True)
        dest_ref[0:1, pl.ds(c0, wide)] = d0.astype(i32)
        dest_ref[1:2, pl.ds(c0, wide)] = d1.astype(i32)
        return carry

    lax.fori_loop(0, T // wide, place, 0)

    start_b = (lax.broadcasted_iota(i32, (1, NBP), 1) * RB).astype(f32)
    be = jnp.minimum(jnp.sum((end_col <= start_b).astype(f32), axis=0, keepdims=True), E - 1.0)
    bprev = jnp.minimum(
        jnp.sum((end_col <= start_b - RB).astype(f32), axis=0, keepdims=True), E - 1.0)
    first = jnp.logical_or(start_b == 0.0, be != bprev)
    eion = lax.broadcasted_iota(i32, (E, NBP), 0).astype(f32)
    vend_b = jnp.sum(jnp.where(eion == be, vend_col, 0.0), axis=0, keepdims=True)
    nv = jnp.clip(vend_b - start_b, 0.0, float(RB))
    blk_ref[...] = jnp.zeros((8, NBP), i32)
    blk_ref[0:1, :] = be.astype(i32)
    blk_ref[1:2, :] = first.astype(i32)
    blk_ref[2:3, :] = nv.astype(i32)


def _route_pos(eidx):
    return pl.pallas_call(
        _route_pos_kernel,
        in_specs=[pl.BlockSpec(memory_space=pltpu.VMEM)],
        out_specs=[pl.BlockSpec(memory_space=pltpu.VMEM)] * 2,
        out_shape=[jax.ShapeDtypeStruct((TOPK, T), i32), jax.ShapeDtypeStruct((8, NBP), i32)],
        scratch_shapes=[pltpu.VMEM((2 * E, T), f32)],
        compiler_params=pltpu.CompilerParams(vmem_limit_bytes=VMEM_LIMIT),
        name="route_pos",
    )(eidx)


TD = 1024
ZR = 1024


def _dispatch_kernel(d0_ref, d1_ref, h2p_ref, xb_ref, zbuf, sem):
    @pl.when(pl.program_id(0) == 0)
    def _():
        zbuf[...] = jnp.zeros((ZR, HP), u32)
        for j in range(N_ROWS // ZR):
            pltpu.make_async_copy(zbuf, xb_ref.at[pl.ds(j * ZR, ZR)], sem).start()
        for j in range(N_ROWS // ZR):
            pltpu.make_async_copy(zbuf, xb_ref.at[pl.ds(j * ZR, ZR)], sem).wait()

    def issue(r, carry):
        _row_copy(h2p_ref, xb_ref, sem, r, d0_ref[r]).start()
        _row_copy(h2p_ref, xb_ref, sem, r, d1_ref[r]).start()
        return carry

    lax.fori_loop(0, TD, issue, 0, unroll=8)

    def drain(r, carry):
        _row_copy(h2p_ref, xb_ref, sem, 0, 0).wait()
        _row_copy(h2p_ref, xb_ref, sem, 0, 0).wait()
        return carry

    lax.fori_loop(0, TD, drain, 0, unroll=8)


def _dispatch(h2p, d0, d1):
    return pl.pallas_call(
        _dispatch_kernel,
        grid=(T // TD,),
        in_specs=[
            pl.BlockSpec((TD,), lambda i: (i,), memory_space=pltpu.SMEM),
            pl.BlockSpec((TD,), lambda i: (i,), memory_space=pltpu.SMEM),
            pl.BlockSpec((TD, HP), lambda i: (i, 0)),
        ],
        out_specs=pl.BlockSpec(memory_space=pl.ANY),
        out_shape=jax.ShapeDtypeStruct((N_ROWS, HP), u32),
        scratch_shapes=[pltpu.VMEM((ZR, HP), u32), pltpu.SemaphoreType.DMA],
        compiler_params=_cp(1),
        name="dispatch",
    )(d0, d1, h2p)


def _final_kernel(x_ref, yg_ref, ewp_ref, modp_ref, g_ref, o_ref):
    o_ref[0] = _rms(_moe_combine(x_ref, yg_ref, ewp_ref, modp_ref), g_ref[...])


def _final(x, moe_in, g):
    return pl.pallas_call(
        _final_kernel,
        grid=(B, NT),
        in_specs=[pl.BlockSpec((1, TM, D), lambda b, s: (b, s, 0))] + _moe_in_specs()
        + [_full((1, D))],
        out_specs=pl.BlockSpec((1, TM, D), lambda b, s: (b, s, 0)),
        out_shape=jax.ShapeDtypeStruct((B, S, D), f32),
        compiler_params=_cp(2),
        name="final_norm",
    )(x, *moe_in, g)


def _moe(h2p, eidx, wg, wu, wd):
    dest, blk = _route_pos(eidx)
    xb = _dispatch(h2p, dest[0], dest[1])
    yb = _experts(blk[0, :NB], blk[1, :NB], blk[2, :NB], xb, wg, wu, wd)
    yg = _gather_rows(yb, dest.reshape(-1))
    return yg.reshape(TOPK, T, HP)


def kernel(x, c, positions, ada_w, ada_b, norm1_g, norm2_g, conv_w_in, conv_b_in, conv_w_dw,
           conv_b_dw, conv_ln_g, conv_ln_b, conv_w_out, kv_in_g, w_dkv, kv_norm_g, w_ukv, w_dq,
           q_norm_g, w_uq, w_o, router_w, router_b, exp_w_gate, exp_w_up, exp_w_down, final_g):
    c8 = jnp.pad(c, ((0, 8 - B), (0, 0)))
    mod = _ada(c8, ada_w, ada_b[:, None, :])[:, :B].reshape(DEPTH, B, 6, D)

    invf = (ROPE_THETA ** (-jnp.arange(0, DR, 2, dtype=f32) / DR))[:, None]
    cos_t, sin_t = _rope_tables(positions[:, None, :], invf)

    rwt = router_w.astype(f32).T
    rw_hi = rwt.astype(bf16)
    rw_lo = (rwt - rw_hi.astype(f32)).astype(bf16)
    rw = jnp.concatenate([rw_hi, rw_lo], axis=0)
    rb = router_b.astype(f32)[:, None]

    w_ukv3 = w_ukv.reshape(KVR, H, DN + DV)
    wuk = w_ukv3[:, :, :DN].reshape(KVR, H * DN).astype(bf16)
    wuvt = w_ukv3[:, :, DN:].reshape(KVR, H * DV).T.astype(bf16)
    kv_w = (kv_in_g[None, :], w_dkv[:, :KVR].astype(bf16), w_dkv[:, KVR:].T.astype(bf16),
            kv_norm_g[None, :], wuk, wuvt)

    moe_in = None
    kc = vt = None
    for l in range(DEPTH):
        if l < N_A:
            wdw = jnp.pad(conv_w_dw[l], ((0, HALO - CONV_W), (0, 0)))
            x, h2p, eidx, ew = _conv_layer(
                x, moe_in, mod[l], norm1_g[l][None, :], conv_w_in[l].astype(bf16),
                conv_b_in[l][None, :], wdw, conv_b_dw[l][None, :], conv_ln_g[l][None, :],
                conv_ln_b[l][None, :], conv_w_out[l].astype(bf16), norm2_g[l][None, :], rw, rb)
        else:
            j = l - N_A
            outs = _preattn(
                x, moe_in, mod[l], norm1_g[l][None, :], w_dq[j].astype(bf16),
                q_norm_g[j][None, :], w_uq[j].T.astype(bf16), cos_t, sin_t,
                kv=kv_w if j == 0 else None)
            if j == 0:
                x, qt, kc, vt = outs
            else:
                x, qt = outs
            o = _attention(qt, kc, vt)
            x, h2p, eidx, ew = _postattn(
                x, o, mod[l], w_o[j].astype(bf16), norm2_g[l][None, :], rw, rb)
        yg = _moe(h2p, eidx, exp_w_gate[l], exp_w_up[l], exp_w_down[l])
        moe_in = (yg, ew, mod[l])
    return _final(x, moe_in, final_g[None, :])
```

```python
import functools
import math

import jax
import jax.numpy as jnp
from jax import lax
from jax.experimental import pallas as pl
from jax.experimental.pallas import tpu as pltpu

f32 = jnp.float32
bf16 = jnp.bfloat16
u32 = jnp.uint32
i32 = jnp.int32

D = 1024
B = 4
S = 4096
T = B * S
DEPTH = 4
N_A = 2
CONV_W = 31
C = 1024
H = 8
DN = 128
DR = 64
DV = 128
DQK = DN + DR
QR = 384
KVR = 256
ROPE_THETA = 10000.0
ATTN_SCALE = 1.0 / math.sqrt(DN + DR)
QSCALE = ATTN_SCALE * math.log2(math.e)
E = 16
NG = 4
EG = 4
TOPK = 2
DE = 512
EPS = 1e-6
NEG = -1e30

TM = 512
NT = S // TM
HALO = 32
CG = C // 128
OB = 8
TQ = 512
TK = 512
RB = 256
N_ROWS = T * TOPK + E * RB
NB = N_ROWS // RB
GR = 2048
HP = D // 2

VMEM_LIMIT = 56 * 1024 * 1024


def _cp(n_axes):
    return pltpu.CompilerParams(
        dimension_semantics=("arbitrary",) * n_axes, vmem_limit_bytes=VMEM_LIMIT)


def _full(shape):
    n = len(shape)
    return pl.BlockSpec(shape, lambda *_: (0,) * n)


def _rms(x, g):
    return x * lax.rsqrt(jnp.mean(x * x, axis=-1, keepdims=True) + EPS) * g


def _pack_pair(a, b):
    ua = lax.bitcast_convert_type(a.astype(bf16).astype(f32), u32)
    ub = lax.bitcast_convert_type(b.astype(bf16).astype(f32), u32)
    return (ua >> 16) | ub


def _unpack_pair(w):
    a = lax.bitcast_convert_type(w << 16, f32)
    b = lax.bitcast_convert_type(w & jnp.uint32(0xFFFF0000), f32)
    return a, b


def _unpack_rows(w):
    a, b = _unpack_pair(w)
    return jnp.concatenate([a, b], axis=1)


def _dot_nt(a, b):
    return lax.dot_general(a, b, (((1,), (1,)), ((), ())), preferred_element_type=f32)


def _top2sum4(a, b, c, d):
    lo1, hi1 = jnp.minimum(a, b), jnp.maximum(a, b)
    lo2, hi2 = jnp.minimum(c, d), jnp.maximum(c, d)
    return jnp.maximum(hi1, hi2) + jnp.maximum(jnp.minimum(hi1, hi2), jnp.maximum(lo1, lo2))


def _route_and_pack(x_new, mod_ref, n2g_ref, rw_ref, rb_ref, h2p_ref, eidx_ref, ewt_ref):
    sh2 = mod_ref[0, 3:4, :]
    sc2 = mod_ref[0, 4:5, :]
    h2 = _rms(x_new, n2g_ref[...]) * (1.0 + sc2) + sh2
    h2p_ref[...] = _pack_pair(h2[:, :HP], h2[:, HP:])

    hi = h2.astype(bf16)
    lo = (h2 - hi.astype(f32)).astype(bf16)
    rw = rw_ref[...]
    p = _dot_nt(rw, hi) + _dot_nt(rw, lo)
    logits = p[:E] + p[E:]
    s = jax.nn.sigmoid(logits)
    sb = s + rb_ref[...]

    srow = [s[e:e + 1] for e in range(E)]
    brow = [sb[e:e + 1] for e in range(E)]
    gs = [_top2sum4(*brow[EG * g:EG * g + EG]) for g in range(NG)]
    gidx = jnp.zeros_like(gs[0], dtype=i32)
    best = gs[0]
    for g in range(1, NG):
        upd = gs[g] > best
        gidx = jnp.where(upd, g, gidx)
        best = jnp.where(upd, gs[g], best)

    def pick(rows, i):
        out = rows[i]
        for g in range(1, NG):
            out = jnp.where(gidx == g, rows[EG * g + i], out)
        return out

    wv = [pick(brow, i) for i in range(EG)]
    sv = [pick(srow, i) for i in range(EG)]

    l1 = jnp.zeros_like(gidx)
    b1 = wv[0]
    for i in range(1, EG):
        upd = wv[i] > b1
        l1 = jnp.where(upd, i, l1)
        b1 = jnp.where(upd, wv[i], b1)
    l2 = jnp.zeros_like(gidx)
    b2 = jnp.where(l1 == 0, -jnp.inf, wv[0])
    for i in range(1, EG):
        cand = jnp.where(l1 == i, -jnp.inf, wv[i])
        upd = cand > b2
        l2 = jnp.where(upd, i, l2)
        b2 = jnp.where(upd, cand, b2)

    def sel(loc):
        out = sv[0]
        for i in range(1, EG):
            out = jnp.where(loc == i, sv[i], out)
        return out

    w1 = sel(l1)
    w2 = sel(l2)
    tot = w1 + w2
    eidx_ref[...] = jnp.concatenate([gidx * EG + l1, gidx * EG + l2], axis=0)
    wt = jnp.concatenate([w1 / tot, w2 / tot, jnp.zeros((128 - TOPK, TM), f32)], axis=0)
    ewt_ref[...] = wt.T


def _moe_combine(x_ref, yg_ref, ewt_ref, modp_ref):
    y0 = _unpack_rows(yg_ref[0])
    y1 = _unpack_rows(yg_ref[1])
    w = ewt_ref[...]
    g2 = modp_ref[0, 5:6, :]
    return x_ref[0] + g2 * (w[:, 0:1] * y0 + w[:, 1:2] * y1)


def _ada_kernel(c_ref, w_ref, b_ref, o_ref):
    c = c_ref[...]
    ca = (c * jax.nn.sigmoid(c)).astype(bf16)
    o_ref[0] = jnp.dot(ca, w_ref[0].astype(bf16), preferred_element_type=f32) + b_ref[0]


def _ada(c8, ada_w, ada_b3):
    tn = 1536
    return pl.pallas_call(
        _ada_kernel,
        grid=(DEPTH, 6 * D // tn),
        in_specs=[
            pl.BlockSpec((8, D), lambda l, n: (0, 0)),
            pl.BlockSpec((1, D, tn), lambda l, n: (l, 0, n)),
            pl.BlockSpec((1, 1, tn), lambda l, n: (l, 0, n)),
        ],
        out_specs=pl.BlockSpec((1, 8, tn), lambda l, n: (l, 0, n)),
        out_shape=jax.ShapeDtypeStruct((DEPTH, 8, 6 * D), f32),
        compiler_params=_cp(2),
        name="ada",
    )(c8, ada_w, ada_b3)


def _rope_kernel(pos_ref, invf_ref, cos_ref, sin_ref):
    ang = pos_ref[0].astype(f32) * invf_ref[...]
    cos_ref[0] = jnp.cos(ang)
    sin_ref[0] = jnp.sin(ang)


def _rope_tables(pos3, invf):
    return pl.pallas_call(
        _rope_kernel,
        grid=(B,),
        in_specs=[pl.BlockSpec((1, 1, S), lambda b: (b, 0, 0)), _full((DR // 2, 1))],
        out_specs=[pl.BlockSpec((1, DR // 2, S), lambda b: (b, 0, 0))] * 2,
        out_shape=[jax.ShapeDtypeStruct((B, DR // 2, S), f32)] * 2,
        compiler_params=_cp(1),
        name="rope",
    )(pos3, invf)


def _conv_kernel(has_moe_in, *refs):
    if has_moe_in:
        x_ref, yg_ref, ewp_ref, modp_ref = refs[:4]
        refs = refs[4:]
    else:
        x_ref = refs[0]
        refs = refs[1:]
    (mod_ref, n1g_ref, win_ref, bin_ref, wdw_ref, bdw_ref, lng_ref, lnb_ref, wout_ref,
     n2g_ref, rw_ref, rb_ref,
     xo_ref, h2p_ref, eidx_ref, ewt_ref, ubuf, obuf) = refs
    si = pl.program_id(1)

    x = _moe_combine(x_ref, yg_ref, ewp_ref, modp_ref) if has_moe_in else x_ref[0]
    sh1 = mod_ref[0, 0:1, :]
    sc1 = mod_ref[0, 1:2, :]
    g1 = mod_ref[0, 2:3, :]
    h = (_rms(x, n1g_ref[...]) * (1.0 + sc1) + sh1).astype(bf16)
    u = jnp.dot(h, win_ref[...], preferred_element_type=f32) + bin_ref[...]
    glu = u[:, :C] * jax.nn.sigmoid(u[:, C:])

    @pl.when(si == 0)
    def _():
        ubuf[0:HALO * CG, :] = jnp.zeros((HALO * CG, 128), f32)

    @pl.when(si > 0)
    def _():
        ubuf[0:HALO * CG, :] = ubuf[TM * CG:(TM + HALO) * CG, :]

    for k in range(CG):
        ubuf[pl.ds(HALO * CG + k, TM, stride=CG), :] = glu[:, 128 * k:128 * (k + 1)]

    off = HALO - (CONV_W - 1)
    taps = [wdw_ref[j] for j in range(CONV_W)]
    bias = bdw_ref[...]

    def block(bi, carry):
        t0 = bi * OB
        acc = [bias] * OB
        for dd in range(OB + CONV_W - 1):
            row = pl.multiple_of((t0 + off + dd) * CG, CG)
            uv = ubuf[pl.ds(row, CG), :]
            for o in range(OB):
                j = dd - o
                if 0 <= j < CONV_W:
                    acc[o] = acc[o] + uv * taps[j]
        for o in range(OB):
            obuf[pl.ds(pl.multiple_of((t0 + o) * CG, CG), CG), :] = acc[o]
        return carry

    lax.fori_loop(0, TM // OB, block, 0)

    v = jnp.concatenate([obuf[pl.ds(k, TM, stride=CG), :] for k in range(CG)], axis=1)
    mu = jnp.mean(v, axis=-1, keepdims=True)
    vc = v - mu
    var = jnp.mean(vc * vc, axis=-1, keepdims=True)
    y = vc * lax.rsqrt(var + EPS) * lng_ref[...] + lnb_ref[...]
    y = (y * jax.nn.sigmoid(y)).astype(bf16)
    mix = jnp.dot(y, wout_ref[...], preferred_element_type=f32)
    x_new = x + g1 * mix
    xo_ref[0] = x_new
    _route_and_pack(x_new, mod_ref, n2g_ref, rw_ref, rb_ref, h2p_ref, eidx_ref, ewt_ref)


def _tok_specs():
    specs = [
        pl.BlockSpec((1, TM, D), lambda b, s: (b, s, 0)),
        pl.BlockSpec((TM, HP), lambda b, s: (b * NT + s, 0)),
        pl.BlockSpec((TOPK, TM), lambda b, s: (0, b * NT + s)),
        pl.BlockSpec((TM, 128), lambda b, s: (b * NT + s, 0)),
    ]
    shapes = [
        jax.ShapeDtypeStruct((B, S, D), f32),
        jax.ShapeDtypeStruct((T, HP), u32),
        jax.ShapeDtypeStruct((TOPK, T), i32),
        jax.ShapeDtypeStruct((T, 128), f32),
    ]
    return specs, shapes


def _moe_in_specs():
    return [
        pl.BlockSpec((TOPK, TM, HP), lambda b, s: (0, b * NT + s, 0)),
        pl.BlockSpec((TM, 128), lambda b, s: (b * NT + s, 0)),
        pl.BlockSpec((1, 6, D), lambda b, s: (b, 0, 0)),
    ]


def _conv_layer(x, moe_in, mod_l, n1g, win, b_in, wdw, bdw, lng, lnb, wout, n2g, rw, rb):
    has_moe_in = moe_in is not None
    in_specs = [pl.BlockSpec((1, TM, D), lambda b, s: (b, s, 0))]
    args = [x]
    if has_moe_in:
        in_specs += _moe_in_specs()
        args += list(moe_in)
    in_specs += [
        pl.BlockSpec((1, 6, D), lambda b, s: (b, 0, 0)),
        _full((1, D)), _full((D, 2 * C)), _full((1, 2 * C)), _full((HALO, CG, 128)),
        _full((CG, 128)),
        _full((1, C)), _full((1, C)), _full((C, D)), _full((1, D)), _full((2 * E, D)),
        _full((E, 1)),
    ]
    args += [mod_l, n1g, win, b_in, wdw, bdw, lng, lnb, wout, n2g, rw, rb]
    out_specs, out_shape = _tok_specs()
    return pl.pallas_call(
        functools.partial(_conv_kernel, has_moe_in),
        grid=(B, NT),
        in_specs=in_specs,
        out_specs=out_specs,
        out_shape=out_shape,
        scratch_shapes=[pltpu.VMEM(((TM + HALO) * CG, 128), f32),
                        pltpu.VMEM((TM * CG, 128), f32)],
        compiler_params=_cp(2),
        name="conv_layer",
    )(*args)


def _preattn_kernel(with_kv, *refs):
    (x_ref, yg_ref, ewp_ref, modp_ref, mod_ref, n1g_ref, wdq_ref, qng_ref, wuqt_ref, cos_ref,
     sin_ref) = refs[:11]
    refs = refs[11:]
    if with_kv:
        kvg_ref, wlat_ref, wropet_ref, kvng_ref, wuk_ref, wuvt_ref = refs[:6]
        refs = refs[6:]
        xo_ref, qt_ref, kc_ref, vt_ref = refs
    else:
        xo_ref, qt_ref = refs

    x = _moe_combine(x_ref, yg_ref, ewp_ref, modp_ref)
    xo_ref[0] = x
    cos = cos_ref[0]
    sin = sin_ref[0]
    sh1 = mod_ref[0, 0:1, :]
    sc1 = mod_ref[0, 1:2, :]
    h = (_rms(x, n1g_ref[...]) * (1.0 + sc1) + sh1).astype(bf16)
    cq = jnp.dot(h, wdq_ref[...], preferred_element_type=f32)
    cq = _rms(cq, qng_ref[...]).astype(bf16)
    qt = _dot_nt(wuqt_ref[...], cq)
    hr = DR // 2
    for hh in range(H):
        r0 = hh * DQK
        x1 = qt[r0 + DN:r0 + DN + hr]
        x2 = qt[r0 + DN + hr:r0 + DQK]
        qt_ref[0, hh, 0:DN, :] = (qt[r0:r0 + DN] * QSCALE).astype(bf16)
        qt_ref[0, hh, DN:DN + hr, :] = ((x1 * cos - x2 * sin) * QSCALE).astype(bf16)
        qt_ref[0, hh, DN + hr:DQK, :] = ((x1 * sin + x2 * cos) * QSCALE).astype(bf16)

    if with_kv:
        xn = _rms(x, kvg_ref[...]).astype(bf16)
        lat = jnp.dot(xn, wlat_ref[...], preferred_element_type=f32)
        latn = _rms(lat, kvng_ref[...]).astype(bf16)
        krt = _dot_nt(wropet_ref[...], xn)
        k1 = krt[:hr]
        k2 = krt[hr:]
        kr = jnp.concatenate([k1 * cos - k2 * sin, k1 * sin + k2 * cos], axis=0)
        kr = kr.T.astype(bf16)
        kn = jnp.dot(latn, wuk_ref[...], preferred_element_type=f32).astype(bf16)
        vt = _dot_nt(wuvt_ref[...], latn).astype(bf16)
        for hh in range(H):
            kc_ref[0, hh, :, 0:DN] = kn[:, hh * DN:(hh + 1) * DN]
            kc_ref[0, hh, :, DN:DQK] = kr
            vt_ref[0, hh] = vt[hh * DV:(hh + 1) * DV]


def _preattn(x, moe_in, mod_l, n1g, wdq, qng, wuqt, cos_t, sin_t, kv=None):
    with_kv = kv is not None
    in_specs = [pl.BlockSpec((1, TM, D), lambda b, s: (b, s, 0))] + _moe_in_specs() + [
        pl.BlockSpec((1, 6, D), lambda b, s: (b, 0, 0)),
        _full((1, D)), _full((D, QR)), _full((1, QR)), _full((H * DQK, QR)),
        pl.BlockSpec((1, DR // 2, TM), lambda b, s: (b, 0, s)),
        pl.BlockSpec((1, DR // 2, TM), lambda b, s: (b, 0, s)),
    ]
    args = [x, *moe_in, mod_l, n1g, wdq, qng, wuqt, cos_t, sin_t]
    out_specs = [
        pl.BlockSpec((1, TM, D), lambda b, s: (b, s, 0)),
        pl.BlockSpec((1, H, DQK, TM), lambda b, s: (b, 0, 0, s)),
    ]
    out_shape = [
        jax.ShapeDtypeStruct((B, S, D), f32),
        jax.ShapeDtypeStruct((B, H, DQK, S), bf16),
    ]
    if with_kv:
        in_specs += [_full((1, D)), _full((D, KVR)), _full((DR, D)), _full((1, KVR)),
                     _full((KVR, H * DN)), _full((H * DV, KVR))]
        args += list(kv)
        out_specs += [
            pl.BlockSpec((1, H, TM, DQK), lambda b, s: (b, 0, s, 0)),
            pl.BlockSpec((1, H, DV, TM), lambda b, s: (b, 0, 0, s)),
        ]
        out_shape += [
            jax.ShapeDtypeStruct((B, H, S, DQK), bf16),
            jax.ShapeDtypeStruct((B, H, DV, S), bf16),
        ]
    return pl.pallas_call(
        functools.partial(_preattn_kernel, with_kv),
        grid=(B, NT),
        in_specs=in_specs,
        out_specs=out_specs,
        out_shape=out_shape,
        compiler_params=_cp(2),
        name="preattn_kv" if with_kv else "preattn",
    )(*args)


def _attn_kernel(qt_ref, kc_ref, vt_ref, o_ref, m_ref, l_ref, acc_ref):
    qi = pl.program_id(2)
    q = qt_ref[0, 0]
    m_ref[...] = jnp.full((1, TQ), NEG, f32)
    l_ref[...] = jnp.zeros((1, TQ), f32)
    acc_ref[...] = jnp.zeros((DV, TQ), f32)

    def step(j, masked):
        k0 = pl.multiple_of(j * TK, TK)
        k = kc_ref[0, 0, pl.ds(k0, TK), :]
        s = jnp.dot(k, q, preferred_element_type=f32)
        if masked:
            kpos = k0 + lax.broadcasted_iota(i32, (TK, TQ), 0)
            qpos = qi * TQ + lax.broadcasted_iota(i32, (TK, TQ), 1)
            s = jnp.where(kpos <= qpos, s, NEG)
        m_old = m_ref[...]
        m_new = jnp.maximum(m_old, jnp.max(s, axis=0, keepdims=True))
        alpha = jnp.exp2(m_old - m_new)
        p = jnp.exp2(s - m_new)
        l_ref[...] = alpha * l_ref[...] + jnp.sum(p, axis=0, keepdims=True)
        v = vt_ref[0, 0, :, pl.ds(k0, TK)]
        acc_ref[...] = alpha * acc_ref[...] + jnp.dot(
            v, p.astype(bf16), preferred_element_type=f32)
        m_ref[...] = m_new

    def body(jj, carry):
        step(2 * jj, False)
        step(2 * jj + 1, False)
        return carry

    lax.fori_loop(0, lax.shift_right_logical(qi, 1), body, 0)

    @pl.when((qi & 1) == 1)
    def _():
        step(qi - 1, False)

    step(qi, True)
    o = acc_ref[...] / l_ref[...]
    o_ref[0] = o.T.astype(bf16)


def _attention(qt, kc, vt):
    return pl.pallas_call(
        _attn_kernel,
        grid=(B, H, S // TQ),
        in_specs=[
            pl.BlockSpec((1, 1, DQK, TQ), lambda b, h, q: (b, h, 0, q)),
            pl.BlockSpec((1, 1, S, DQK), lambda b, h, q: (b, h, 0, 0)),
            pl.BlockSpec((1, 1, DV, S), lambda b, h, q: (b, h, 0, 0)),
        ],
        out_specs=pl.BlockSpec((1, TQ, DV), lambda b, h, q: (b, q, h)),
        out_shape=jax.ShapeDtypeStruct((B, S, H * DV), bf16),
        scratch_shapes=[pltpu.VMEM((1, TQ), f32), pltpu.VMEM((1, TQ), f32),
                        pltpu.VMEM((DV, TQ), f32)],
        compiler_params=_cp(3),
        name="attn",
    )(qt, kc, vt)


def _postattn_kernel(x_ref, o_ref, mod_ref, wo_ref, n2g_ref, rw_ref, rb_ref,
                     xo_ref, h2p_ref, eidx_ref, ewt_ref):
    g1 = mod_ref[0, 2:3, :]
    mix = jnp.dot(o_ref[0], wo_ref[...], preferred_element_type=f32)
    x_new = x_ref[0] + g1 * mix
    xo_ref[0] = x_new
    _route_and_pack(x_new, mod_ref, n2g_ref, rw_ref, rb_ref, h2p_ref, eidx_ref, ewt_ref)


def _postattn(x, o, mod_l, wo, n2g, rw, rb):
    out_specs, out_shape = _tok_specs()
    return pl.pallas_call(
        _postattn_kernel,
        grid=(B, NT),
        in_specs=[
            pl.BlockSpec((1, TM, D), lambda b, s: (b, s, 0)),
            pl.BlockSpec((1, TM, H * DV), lambda b, s: (b, s, 0)),
            pl.BlockSpec((1, 6, D), lambda b, s: (b, 0, 0)),
            _full((H * DV, D)), _full((1, D)), _full((2 * E, D)), _full((E, 1)),
        ],
        out_specs=out_specs,
        out_shape=out_shape,
        compiler_params=_cp(2),
        name="postattn",
    )(x, o, mod_l, wo, n2g, rw, rb)


def _row_copy(src_ref, dst_ref, sem, src_row, dst_row):
    return pltpu.make_async_copy(
        src_ref.at[pl.ds(src_row, 1)], dst_ref.at[pl.ds(dst_row, 1)], sem)


def _gather_kernel(idx_ref, src_ref, dst_ref, sem):
    def issue(r, carry):
        _row_copy(src_ref, dst_ref, sem, idx_ref[r], r).start()
        return carry

    lax.fori_loop(0, GR, issue, 0, unroll=8)

    def drain(r, carry):
        _row_copy(src_ref, dst_ref, sem, 0, r).wait()
        return carry

    lax.fori_loop(0, GR, drain, 0, unroll=8)


def _gather_rows(src, idx):
    n = idx.shape[0]
    return pl.pallas_call(
        _gather_kernel,
        grid=(n // GR,),
        in_specs=[
            pl.BlockSpec((GR,), lambda i: (i,), memory_space=pltpu.SMEM),
            pl.BlockSpec(memory_space=pl.ANY),
        ],
        out_specs=pl.BlockSpec((GR, src.shape[1]), lambda i: (i, 0)),
        out_shape=jax.ShapeDtypeStruct((n, src.shape[1]), src.dtype),
        scratch_shapes=[pltpu.SemaphoreType.DMA],
        compiler_params=_cp(1),
        name="gather_rows",
    )(idx, src)


def _expert_kernel(be_ref, first_ref, nv_ref, xb_ref, wg_ref, wu_ref, wd_ref, yb_ref,
                   wgu_bf, wd_bf):
    i = pl.program_id(0)
    nv = nv_ref[i]

    @pl.when(first_ref[i] == 1)
    def _():
        wgu_bf[:, :DE] = wg_ref[0, 0].astype(bf16)
        wgu_bf[:, DE:] = wu_ref[0, 0].astype(bf16)
        wd_bf[...] = wd_ref[0, 0].astype(bf16)

    @pl.when(nv > 0)
    def _():
        x = _unpack_rows(xb_ref[...]).astype(bf16)
        gu = jnp.dot(x, wgu_bf[...], preferred_element_type=f32)
        g = gu[:, :DE]
        hmid = (g * jax.nn.sigmoid(g) * gu[:, DE:]).astype(bf16)
        y = jnp.dot(hmid, wd_bf[...], preferred_element_type=f32)
        yb_ref[...] = _pack_pair(y[:, :HP], y[:, HP:])

    @pl.when(nv == 0)
    def _():
        yb_ref[...] = jnp.zeros((RB, HP), u32)


def _experts(layer, blk_e, blk_first, blk_nv, xb, wg, wu, wd):
    grid_spec = pltpu.PrefetchScalarGridSpec(
        num_scalar_prefetch=3,
        grid=(NB,),
        in_specs=[
            pl.BlockSpec((RB, HP), lambda i, be, bf, nv: (i, 0)),
            pl.BlockSpec((1, 1, D, DE), lambda i, be, bf, nv: (layer, be[i], 0, 0)),
            pl.BlockSpec((1, 1, D, DE), lambda i, be, bf, nv: (layer, be[i], 0, 0)),
            pl.BlockSpec((1, 1, DE, D), lambda i, be, bf, nv: (layer, be[i], 0, 0)),
        ],
        out_specs=pl.BlockSpec((RB, HP), lambda i, be, bf, nv: (i, 0)),
        scratch_shapes=[pltpu.VMEM((D, 2 * DE), bf16), pltpu.VMEM((DE, D), bf16)],
    )
    return pl.pallas_call(
        _expert_kernel,
        grid_spec=grid_spec,
        out_shape=jax.ShapeDtypeStruct((N_ROWS, HP), u32),
        compiler_params=_cp(1),
        name="experts",
    )(blk_e, blk_first, blk_nv, xb, wg, wu, wd)


PCH = 256
NBP = 256


def _route_pos_kernel(eidx_ref, dest_ref, blk_ref, pre_ref):
    upper = (lax.broadcasted_iota(i32, (PCH, PCH), 0)
             < lax.broadcasted_iota(i32, (PCH, PCH), 1)).astype(bf16)
    eio = lax.broadcasted_iota(i32, (E, PCH), 0)

    def onehot(c0, width, eiota):
        e0 = eidx_ref[0:1, pl.ds(c0, width)]
        e1 = eidx_ref[1:2, pl.ds(c0, width)]
        return jnp.concatenate([e0 == eiota, e1 == eiota], axis=0)

    ones_r = jnp.ones((8, PCH), bf16)

    def count(c, carry):
        col, row = carry
        c0 = pl.multiple_of(c * PCH, PCH)
        oh = onehot(c0, PCH, eio).astype(bf16)
        pre_ref[:, pl.ds(c0, PCH)] = col + jnp.dot(oh, upper, preferred_element_type=f32)
        col = col + jnp.sum(oh.astype(f32), axis=1, keepdims=True)
        row = row + _dot_nt(ones_r, oh[:E]) + _dot_nt(ones_r, oh[E:])
        return col, row

    tot_c, tot_r = lax.fori_loop(
        0, T // PCH, count, (jnp.zeros((2 * E, 1), f32), jnp.zeros((8, E), f32)))
    cnt0_c = tot_c[:E]
    cnt_c = tot_c[:E] + tot_c[E:]
    shift = RB.bit_length() - 1
    assert RB == 1 << shift
    padded_r = lax.shift_left(
        lax.shift_right_logical(tot_r[0:1].astype(i32) + (RB - 1), shift), shift).astype(f32)
    ee = lax.broadcasted_iota(i32, (E, E), 0)
    ep = lax.broadcasted_iota(i32, (E, E), 1)
    start_col = jnp.sum(jnp.where(ep < ee, padded_r, 0.0), axis=1, keepdims=True)
    end_col = jnp.sum(jnp.where(ep <= ee, padded_r, 0.0), axis=1, keepdims=True)
    vend_col = start_col + cnt_c
    basef = jnp.concatenate([start_col, start_col + cnt0_c], axis=0)

    wide = 2048
    eiow = lax.broadcasted_iota(i32, (E, wide), 0)

    def place(c, carry):
        c0 = pl.multiple_of(c * wide, wide)
        oh = onehot(c0, wide, eiow)
        val = jnp.where(oh, pre_ref[:, pl.ds(c0, wide)] + basef, 0.0)
        d0 = jnp.sum(val[:E], axis=0, keepdims=True)
        d1 = jnp.sum(val[E:], axis=0, keepdims=True)
        dest_ref[0:1, pl.ds(c0, wide)] = d0.astype(i32)
        dest_ref[1:2, pl.ds(c0, wide)] = d1.astype(i32)
        return carry

    lax.fori_loop(0, T // wide, place, 0)

    start_b = (lax.broadcasted_iota(i32, (1, NBP), 1) * RB).astype(f32)
    be = jnp.minimum(jnp.sum((end_col <= start_b).astype(f32), axis=0, keepdims=True), E - 1.0)
    bprev = jnp.minimum(
        jnp.sum((end_col <= start_b - RB).astype(f32), axis=0, keepdims=True), E - 1.0)
    first = jnp.logical_or(start_b == 0.0, be != bprev)
    eion = lax.broadcasted_iota(i32, (E, NBP), 0).astype(f32)
    vend_b = jnp.sum(jnp.where(eion == be, vend_col, 0.0), axis=0, keepdims=True)
    nv = jnp.clip(vend_b - start_b, 0.0, float(RB))
    blk_ref[...] = jnp.zeros((8, NBP), i32)
    blk_ref[0:1, :] = be.astype(i32)
    blk_ref[1:2, :] = first.astype(i32)
    blk_ref[2:3, :] = nv.astype(i32)


def _route_pos(eidx):
    return pl.pallas_call(
        _route_pos_kernel,
        in_specs=[pl.BlockSpec(memory_space=pltpu.VMEM)],
        out_specs=[pl.BlockSpec(memory_space=pltpu.VMEM)] * 2,
        out_shape=[jax.ShapeDtypeStruct((TOPK, T), i32), jax.ShapeDtypeStruct((8, NBP), i32)],
        scratch_shapes=[pltpu.VMEM((2 * E, T), f32)],
        compiler_params=pltpu.CompilerParams(vmem_limit_bytes=VMEM_LIMIT),
        name="route_pos",
    )(eidx)


TD = 1024
ZR = 1024


def _dispatch_kernel(d0_ref, d1_ref, h2p_ref, xb_ref, zbuf, sem):
    @pl.when(pl.program_id(0) == 0)
    def _():
        zbuf[...] = jnp.zeros((ZR, HP), u32)
        for j in range(N_ROWS // ZR):
            pltpu.make_async_copy(zbuf, xb_ref.at[pl.ds(j * ZR, ZR)], sem).start()
        for j in range(N_ROWS // ZR):
            pltpu.make_async_copy(zbuf, xb_ref.at[pl.ds(j * ZR, ZR)], sem).wait()

    def issue(r, carry):
        _row_copy(h2p_ref, xb_ref, sem, r, d0_ref[r]).start()
        _row_copy(h2p_ref, xb_ref, sem, r, d1_ref[r]).start()
        return carry

    lax.fori_loop(0, TD, issue, 0, unroll=8)

    def drain(r, carry):
        _row_copy(h2p_ref, xb_ref, sem, 0, 0).wait()
        _row_copy(h2p_ref, xb_ref, sem, 0, 0).wait()
        return carry

    lax.fori_loop(0, TD, drain, 0, unroll=8)


def _dispatch(h2p, d0, d1):
    return pl.pallas_call(
        _dispatch_kernel,
        grid=(T // TD,),
        in_specs=[
            pl.BlockSpec((TD,), lambda i: (i,), memory_space=pltpu.SMEM),
            pl.BlockSpec((TD,), lambda i: (i,), memory_space=pltpu.SMEM),
            pl.BlockSpec((TD, HP), lambda i: (i, 0)),
        ],
        out_specs=pl.BlockSpec(memory_space=pl.ANY),
        out_shape=jax.ShapeDtypeStruct((N_ROWS, HP), u32),
        scratch_shapes=[pltpu.VMEM((ZR, HP), u32), pltpu.SemaphoreType.DMA],
        compiler_params=_cp(1),
        name="dispatch",
    )(d0, d1, h2p)


def _final_kernel(x_ref, yg_ref, ewp_ref, modp_ref, g_ref, o_ref):
    o_ref[0] = _rms(_moe_combine(x_ref, yg_ref, ewp_ref, modp_ref), g_ref[...])


def _final(x, moe_in, g):
    return pl.pallas_call(
        _final_kernel,
        grid=(B, NT),
        in_specs=[pl.BlockSpec((1, TM, D), lambda b, s: (b, s, 0))] + _moe_in_specs()
        + [_full((1, D))],
        out_specs=pl.BlockSpec((1, TM, D), lambda b, s: (b, s, 0)),
        out_shape=jax.ShapeDtypeStruct((B, S, D), f32),
        compiler_params=_cp(2),
        name="final_norm",
    )(x, *moe_in, g)


def _moe(layer, h2p, eidx, wg, wu, wd):
    dest, blk = _route_pos(eidx)
    xb = _dispatch(h2p, dest[0], dest[1])
    yb = _experts(layer, blk[0, :NB], blk[1, :NB], blk[2, :NB], xb, wg, wu, wd)
    yg = _gather_rows(yb, dest.reshape(-1))
    return yg.reshape(TOPK, T, HP)


def kernel(x, c, positions, ada_w, ada_b, norm1_g, norm2_g, conv_w_in, conv_b_in, conv_w_dw,
           conv_b_dw, conv_ln_g, conv_ln_b, conv_w_out, kv_in_g, w_dkv, kv_norm_g, w_ukv, w_dq,
           q_norm_g, w_uq, w_o, router_w, router_b, exp_w_gate, exp_w_up, exp_w_down, final_g):
    c8 = jnp.pad(c, ((0, 8 - B), (0, 0)))
    mod = _ada(c8, ada_w, ada_b[:, None, :])[:, :B].reshape(DEPTH, B, 6, D)

    invf = (ROPE_THETA ** (-jnp.arange(0, DR, 2, dtype=f32) / DR))[:, None]
    cos_t, sin_t = _rope_tables(positions[:, None, :], invf)

    rwt = router_w.astype(f32).T
    rw_hi = rwt.astype(bf16)
    rw_lo = (rwt - rw_hi.astype(f32)).astype(bf16)
    rw = jnp.concatenate([rw_hi, rw_lo], axis=0)
    rb = router_b.astype(f32)[:, None]

    w_ukv3 = w_ukv.reshape(KVR, H, DN + DV)
    wuk = w_ukv3[:, :, :DN].reshape(KVR, H * DN).astype(bf16)
    wuvt = w_ukv3[:, :, DN:].reshape(KVR, H * DV).T.astype(bf16)
    kv_w = (kv_in_g[None, :], w_dkv[:, :KVR].astype(bf16), w_dkv[:, KVR:].T.astype(bf16),
            kv_norm_g[None, :], wuk, wuvt)

    moe_in = None
    kc = vt = None
    for l in range(DEPTH):
        if l < N_A:
            wdw = jnp.pad(conv_w_dw[l], ((0, HALO - CONV_W), (0, 0))).reshape(HALO, CG, 128)
            x, h2p, eidx, ew = _conv_layer(
                x, moe_in, mod[l], norm1_g[l][None, :], conv_w_in[l].astype(bf16),
                conv_b_in[l][None, :], wdw, conv_b_dw[l].reshape(CG, 128), conv_ln_g[l][None, :],
                conv_ln_b[l][None, :], conv_w_out[l].astype(bf16), norm2_g[l][None, :], rw, rb)
        else:
            j = l - N_A
            outs = _preattn(
                x, moe_in, mod[l], norm1_g[l][None, :], w_dq[j].astype(bf16),
                q_norm_g[j][None, :], w_uq[j].T.astype(bf16), cos_t, sin_t,
                kv=kv_w if j == 0 else None)
            if j == 0:
                x, qt, kc, vt = outs
            else:
                x, qt = outs
            o = _attention(qt, kc, vt)
            x, h2p, eidx, ew = _postattn(
                x, o, mod[l], w_o[j].astype(bf16), norm2_g[l][None, :], rw, rb)
        yg = _moe(l, h2p, eidx, exp_w_gate, exp_w_up, exp_w_down)
        moe_in = (yg, ew, mod[l])
    return _final(x, moe_in, final_g[None, :])
```

```python
import functools
import math

import jax
import jax.numpy as jnp
from jax import lax
from jax.experimental import pallas as pl
from jax.experimental.pallas import tpu as pltpu

f32 = jnp.float32
bf16 = jnp.bfloat16
u32 = jnp.uint32
i32 = jnp.int32

D = 1024
B = 4
S = 4096
T = B * S
DEPTH = 4
N_A = 2
CONV_W = 31
C = 1024
H = 8
DN = 128
DR = 64
DV = 128
DQK = DN + DR
QR = 384
KVR = 256
ROPE_THETA = 10000.0
ATTN_SCALE = 1.0 / math.sqrt(DN + DR)
QSCALE = ATTN_SCALE * math.log2(math.e)
E = 16
NG = 4
EG = 4
TOPK = 2
DE = 512
EPS = 1e-6
NEG = -1e30

TM = 512
NT = S // TM
HALO = 32
CG = C // 128
OB = 8
TQ = 512
TK = 512
RB = 512
N_ROWS = T * TOPK + E * RB
NB = N_ROWS // RB
GR = 2048
HP = D // 2

VMEM_LIMIT = 56 * 1024 * 1024


def _cp(n_axes):
    return pltpu.CompilerParams(
        dimension_semantics=("arbitrary",) * n_axes, vmem_limit_bytes=VMEM_LIMIT)


def _full(shape):
    n = len(shape)
    return pl.BlockSpec(shape, lambda *_: (0,) * n)


def _rms(x, g):
    return x * lax.rsqrt(jnp.mean(x * x, axis=-1, keepdims=True) + EPS) * g


def _pack_pair(a, b):
    ua = lax.bitcast_convert_type(a.astype(bf16).astype(f32), u32)
    ub = lax.bitcast_convert_type(b.astype(bf16).astype(f32), u32)
    return (ua >> 16) | ub


def _unpack_pair(w):
    a = lax.bitcast_convert_type(w << 16, f32)
    b = lax.bitcast_convert_type(w & jnp.uint32(0xFFFF0000), f32)
    return a, b


def _unpack_rows(w):
    a, b = _unpack_pair(w)
    return jnp.concatenate([a, b], axis=1)


def _dot_nt(a, b):
    return lax.dot_general(a, b, (((1,), (1,)), ((), ())), preferred_element_type=f32)


def _top2sum4(a, b, c, d):
    lo1, hi1 = jnp.minimum(a, b), jnp.maximum(a, b)
    lo2, hi2 = jnp.minimum(c, d), jnp.maximum(c, d)
    return jnp.maximum(hi1, hi2) + jnp.maximum(jnp.minimum(hi1, hi2), jnp.maximum(lo1, lo2))


def _route_and_pack(x_new, mod_ref, n2g_ref, rw_ref, rb_ref, h2p_ref, eidx_ref, ewt_ref):
    sh2 = mod_ref[0, 3:4, :]
    sc2 = mod_ref[0, 4:5, :]
    h2 = _rms(x_new, n2g_ref[...]) * (1.0 + sc2) + sh2
    h2p_ref[...] = _pack_pair(h2[:, :HP], h2[:, HP:])

    hi = h2.astype(bf16)
    lo = (h2 - hi.astype(f32)).astype(bf16)
    rw = rw_ref[...]
    p = _dot_nt(rw, hi) + _dot_nt(rw, lo)
    logits = p[:E] + p[E:]
    s = jax.nn.sigmoid(logits)
    sb = s + rb_ref[...]

    srow = [s[e:e + 1] for e in range(E)]
    brow = [sb[e:e + 1] for e in range(E)]
    gs = [_top2sum4(*brow[EG * g:EG * g + EG]) for g in range(NG)]
    gidx = jnp.zeros_like(gs[0], dtype=i32)
    best = gs[0]
    for g in range(1, NG):
        upd = gs[g] > best
        gidx = jnp.where(upd, g, gidx)
        best = jnp.where(upd, gs[g], best)

    def pick(rows, i):
        out = rows[i]
        for g in range(1, NG):
            out = jnp.where(gidx == g, rows[EG * g + i], out)
        return out

    wv = [pick(brow, i) for i in range(EG)]
    sv = [pick(srow, i) for i in range(EG)]

    l1 = jnp.zeros_like(gidx)
    b1 = wv[0]
    for i in range(1, EG):
        upd = wv[i] > b1
        l1 = jnp.where(upd, i, l1)
        b1 = jnp.where(upd, wv[i], b1)
    l2 = jnp.zeros_like(gidx)
    b2 = jnp.where(l1 == 0, -jnp.inf, wv[0])
    for i in range(1, EG):
        cand = jnp.where(l1 == i, -jnp.inf, wv[i])
        upd = cand > b2
        l2 = jnp.where(upd, i, l2)
        b2 = jnp.where(upd, cand, b2)

    def sel(loc):
        out = sv[0]
        for i in range(1, EG):
            out = jnp.where(loc == i, sv[i], out)
        return out

    w1 = sel(l1)
    w2 = sel(l2)
    tot = w1 + w2
    eidx_ref[...] = jnp.concatenate([gidx * EG + l1, gidx * EG + l2], axis=0)
    wt = jnp.concatenate([w1 / tot, w2 / tot, jnp.zeros((128 - TOPK, TM), f32)], axis=0)
    ewt_ref[...] = wt.T


def _moe_combine(x_ref, yg_ref, ewt_ref, modp_ref):
    y0 = _unpack_rows(yg_ref[0])
    y1 = _unpack_rows(yg_ref[1])
    w = ewt_ref[...]
    g2 = modp_ref[0, 5:6, :]
    return x_ref[0] + g2 * (w[:, 0:1] * y0 + w[:, 1:2] * y1)


def _ada_kernel(c_ref, w_ref, b_ref, o_ref):
    c = c_ref[...]
    ca = (c * jax.nn.sigmoid(c)).astype(bf16)
    o_ref[0] = jnp.dot(ca, w_ref[0].astype(bf16), preferred_element_type=f32) + b_ref[0]


def _ada(c8, ada_w, ada_b3):
    tn = 1536
    return pl.pallas_call(
        _ada_kernel,
        grid=(DEPTH, 6 * D // tn),
        in_specs=[
            pl.BlockSpec((8, D), lambda l, n: (0, 0)),
            pl.BlockSpec((1, D, tn), lambda l, n: (l, 0, n)),
            pl.BlockSpec((1, 1, tn), lambda l, n: (l, 0, n)),
        ],
        out_specs=pl.BlockSpec((1, 8, tn), lambda l, n: (l, 0, n)),
        out_shape=jax.ShapeDtypeStruct((DEPTH, 8, 6 * D), f32),
        compiler_params=_cp(2),
        name="ada",
    )(c8, ada_w, ada_b3)


def _rope_kernel(pos_ref, invf_ref, cos_ref, sin_ref):
    ang = pos_ref[0].astype(f32) * invf_ref[...]
    cos_ref[0] = jnp.cos(ang)
    sin_ref[0] = jnp.sin(ang)


def _rope_tables(pos3, invf):
    return pl.pallas_call(
        _rope_kernel,
        grid=(B,),
        in_specs=[pl.BlockSpec((1, 1, S), lambda b: (b, 0, 0)), _full((DR // 2, 1))],
        out_specs=[pl.BlockSpec((1, DR // 2, S), lambda b: (b, 0, 0))] * 2,
        out_shape=[jax.ShapeDtypeStruct((B, DR // 2, S), f32)] * 2,
        compiler_params=_cp(1),
        name="rope",
    )(pos3, invf)


def _conv_kernel(has_moe_in, *refs):
    if has_moe_in:
        x_ref, yg_ref, ewp_ref, modp_ref = refs[:4]
        refs = refs[4:]
    else:
        x_ref = refs[0]
        refs = refs[1:]
    (mod_ref, n1g_ref, win_ref, bin_ref, wdw_ref, bdw_ref, lng_ref, lnb_ref, wout_ref,
     n2g_ref, rw_ref, rb_ref,
     xo_ref, h2p_ref, eidx_ref, ewt_ref, ubuf, obuf) = refs
    si = pl.program_id(1)

    x = _moe_combine(x_ref, yg_ref, ewp_ref, modp_ref) if has_moe_in else x_ref[0]
    sh1 = mod_ref[0, 0:1, :]
    sc1 = mod_ref[0, 1:2, :]
    g1 = mod_ref[0, 2:3, :]
    h = (_rms(x, n1g_ref[...]) * (1.0 + sc1) + sh1).astype(bf16)
    u = jnp.dot(h, win_ref[...], preferred_element_type=f32) + bin_ref[...]
    glu = u[:, :C] * jax.nn.sigmoid(u[:, C:])

    @pl.when(si == 0)
    def _():
        ubuf[0:HALO * CG, :] = jnp.zeros((HALO * CG, 128), f32)

    @pl.when(si > 0)
    def _():
        ubuf[0:HALO * CG, :] = ubuf[TM * CG:(TM + HALO) * CG, :]

    for k in range(CG):
        ubuf[pl.ds(HALO * CG + k, TM, stride=CG), :] = glu[:, 128 * k:128 * (k + 1)]

    off = HALO - (CONV_W - 1)
    taps = [wdw_ref[j] for j in range(CONV_W)]
    bias = bdw_ref[...]

    def block(bi, carry):
        t0 = bi * OB
        acc = [bias] * OB
        for dd in range(OB + CONV_W - 1):
            row = pl.multiple_of((t0 + off + dd) * CG, CG)
            uv = ubuf[pl.ds(row, CG), :]
            for o in range(OB):
                j = dd - o
                if 0 <= j < CONV_W:
                    acc[o] = acc[o] + uv * taps[j]
        for o in range(OB):
            obuf[pl.ds(pl.multiple_of((t0 + o) * CG, CG), CG), :] = acc[o]
        return carry

    lax.fori_loop(0, TM // OB, block, 0)

    v = jnp.concatenate([obuf[pl.ds(k, TM, stride=CG), :] for k in range(CG)], axis=1)
    mu = jnp.mean(v, axis=-1, keepdims=True)
    vc = v - mu
    var = jnp.mean(vc * vc, axis=-1, keepdims=True)
    y = vc * lax.rsqrt(var + EPS) * lng_ref[...] + lnb_ref[...]
    y = (y * jax.nn.sigmoid(y)).astype(bf16)
    mix = jnp.dot(y, wout_ref[...], preferred_element_type=f32)
    x_new = x + g1 * mix
    xo_ref[0] = x_new
    _route_and_pack(x_new, mod_ref, n2g_ref, rw_ref, rb_ref, h2p_ref, eidx_ref, ewt_ref)


def _tok_specs():
    specs = [
        pl.BlockSpec((1, TM, D), lambda b, s: (b, s, 0)),
        pl.BlockSpec((TM, HP), lambda b, s: (b * NT + s, 0)),
        pl.BlockSpec((TOPK, TM), lambda b, s: (0, b * NT + s)),
        pl.BlockSpec((TM, 128), lambda b, s: (b * NT + s, 0)),
    ]
    shapes = [
        jax.ShapeDtypeStruct((B, S, D), f32),
        jax.ShapeDtypeStruct((T, HP), u32),
        jax.ShapeDtypeStruct((TOPK, T), i32),
        jax.ShapeDtypeStruct((T, 128), f32),
    ]
    return specs, shapes


def _moe_in_specs():
    return [
        pl.BlockSpec((TOPK, TM, HP), lambda b, s: (0, b * NT + s, 0)),
        pl.BlockSpec((TM, 128), lambda b, s: (b * NT + s, 0)),
        pl.BlockSpec((1, 6, D), lambda b, s: (b, 0, 0)),
    ]


def _conv_layer(x, moe_in, mod_l, n1g, win, b_in, wdw, bdw, lng, lnb, wout, n2g, rw, rb):
    has_moe_in = moe_in is not None
    in_specs = [pl.BlockSpec((1, TM, D), lambda b, s: (b, s, 0))]
    args = [x]
    if has_moe_in:
        in_specs += _moe_in_specs()
        args += list(moe_in)
    in_specs += [
        pl.BlockSpec((1, 6, D), lambda b, s: (b, 0, 0)),
        _full((1, D)), _full((D, 2 * C)), _full((1, 2 * C)), _full((HALO, CG, 128)),
        _full((CG, 128)),
        _full((1, C)), _full((1, C)), _full((C, D)), _full((1, D)), _full((2 * E, D)),
        _full((E, 1)),
    ]
    args += [mod_l, n1g, win, b_in, wdw, bdw, lng, lnb, wout, n2g, rw, rb]
    out_specs, out_shape = _tok_specs()
    return pl.pallas_call(
        functools.partial(_conv_kernel, has_moe_in),
        grid=(B, NT),
        in_specs=in_specs,
        out_specs=out_specs,
        out_shape=out_shape,
        scratch_shapes=[pltpu.VMEM(((TM + HALO) * CG, 128), f32),
                        pltpu.VMEM((TM * CG, 128), f32)],
        compiler_params=_cp(2),
        name="conv_layer",
    )(*args)


def _preattn_kernel(with_kv, *refs):
    (x_ref, yg_ref, ewp_ref, modp_ref, mod_ref, n1g_ref, wdq_ref, qng_ref, wuqt_ref, cos_ref,
     sin_ref) = refs[:11]
    refs = refs[11:]
    if with_kv:
        kvg_ref, wlat_ref, wropet_ref, kvng_ref, wuk_ref, wuvt_ref = refs[:6]
        refs = refs[6:]
        xo_ref, qt_ref, kc_ref, vt_ref = refs
    else:
        xo_ref, qt_ref = refs

    x = _moe_combine(x_ref, yg_ref, ewp_ref, modp_ref)
    xo_ref[0] = x
    cos = cos_ref[0]
    sin = sin_ref[0]
    sh1 = mod_ref[0, 0:1, :]
    sc1 = mod_ref[0, 1:2, :]
    h = (_rms(x, n1g_ref[...]) * (1.0 + sc1) + sh1).astype(bf16)
    cq = jnp.dot(h, wdq_ref[...], preferred_element_type=f32)
    cq = _rms(cq, qng_ref[...]).astype(bf16)
    qt = _dot_nt(wuqt_ref[...], cq)
    hr = DR // 2
    for hh in range(H):
        r0 = hh * DQK
        x1 = qt[r0 + DN:r0 + DN + hr]
        x2 = qt[r0 + DN + hr:r0 + DQK]
        qt_ref[0, hh, 0:DN, :] = (qt[r0:r0 + DN] * QSCALE).astype(bf16)
        qt_ref[0, hh, DN:DN + hr, :] = ((x1 * cos - x2 * sin) * QSCALE).astype(bf16)
        qt_ref[0, hh, DN + hr:DQK, :] = ((x1 * sin + x2 * cos) * QSCALE).astype(bf16)

    if with_kv:
        xn = _rms(x, kvg_ref[...]).astype(bf16)
        lat = jnp.dot(xn, wlat_ref[...], preferred_element_type=f32)
        latn = _rms(lat, kvng_ref[...]).astype(bf16)
        krt = _dot_nt(wropet_ref[...], xn)
        k1 = krt[:hr]
        k2 = krt[hr:]
        kr = jnp.concatenate([k1 * cos - k2 * sin, k1 * sin + k2 * cos], axis=0)
        kr = kr.T.astype(bf16)
        kn = jnp.dot(latn, wuk_ref[...], preferred_element_type=f32).astype(bf16)
        vt = _dot_nt(wuvt_ref[...], latn).astype(bf16)
        for hh in range(H):
            kc_ref[0, hh, :, 0:DN] = kn[:, hh * DN:(hh + 1) * DN]
            kc_ref[0, hh, :, DN:DQK] = kr
            vt_ref[0, hh] = vt[hh * DV:(hh + 1) * DV]


def _preattn(x, moe_in, mod_l, n1g, wdq, qng, wuqt, cos_t, sin_t, kv=None):
    with_kv = kv is not None
    in_specs = [pl.BlockSpec((1, TM, D), lambda b, s: (b, s, 0))] + _moe_in_specs() + [
        pl.BlockSpec((1, 6, D), lambda b, s: (b, 0, 0)),
        _full((1, D)), _full((D, QR)), _full((1, QR)), _full((H * DQK, QR)),
        pl.BlockSpec((1, DR // 2, TM), lambda b, s: (b, 0, s)),
        pl.BlockSpec((1, DR // 2, TM), lambda b, s: (b, 0, s)),
    ]
    args = [x, *moe_in, mod_l, n1g, wdq, qng, wuqt, cos_t, sin_t]
    out_specs = [
        pl.BlockSpec((1, TM, D), lambda b, s: (b, s, 0)),
        pl.BlockSpec((1, H, DQK, TM), lambda b, s: (b, 0, 0, s)),
    ]
    out_shape = [
        jax.ShapeDtypeStruct((B, S, D), f32),
        jax.ShapeDtypeStruct((B, H, DQK, S), bf16),
    ]
    if with_kv:
        in_specs += [_full((1, D)), _full((D, KVR)), _full((DR, D)), _full((1, KVR)),
                     _full((KVR, H * DN)), _full((H * DV, KVR))]
        args += list(kv)
        out_specs += [
            pl.BlockSpec((1, H, TM, DQK), lambda b, s: (b, 0, s, 0)),
            pl.BlockSpec((1, H, DV, TM), lambda b, s: (b, 0, 0, s)),
        ]
        out_shape += [
            jax.ShapeDtypeStruct((B, H, S, DQK), bf16),
            jax.ShapeDtypeStruct((B, H, DV, S), bf16),
        ]
    return pl.pallas_call(
        functools.partial(_preattn_kernel, with_kv),
        grid=(B, NT),
        in_specs=in_specs,
        out_specs=out_specs,
        out_shape=out_shape,
        compiler_params=_cp(2),
        name="preattn_kv" if with_kv else "preattn",
    )(*args)


def _attn_kernel(qt_ref, kc_ref, vt_ref, o_ref, m_ref, l_ref, acc_ref, s_ref):
    qi = pl.program_id(2)
    q = qt_ref[0, 0]
    m_ref[...] = jnp.full((1, TQ), NEG, f32)
    l_ref[...] = jnp.zeros((1, TQ), f32)
    acc_ref[...] = jnp.zeros((DV, TQ), f32)

    def scores(j, slot):
        k0 = pl.multiple_of(j * TK, TK)
        k = kc_ref[0, 0, pl.ds(k0, TK), :]
        s_ref[slot] = jnp.dot(k, q, preferred_element_type=f32)

    def softmax_pv(j, slot, masked):
        k0 = pl.multiple_of(j * TK, TK)
        s = s_ref[slot]
        if masked:
            kpos = k0 + lax.broadcasted_iota(i32, (TK, TQ), 0)
            qpos = qi * TQ + lax.broadcasted_iota(i32, (TK, TQ), 1)
            s = jnp.where(kpos <= qpos, s, NEG)
        m_old = m_ref[...]
        m_new = jnp.maximum(m_old, jnp.max(s, axis=0, keepdims=True))
        alpha = jnp.exp2(m_old - m_new)
        p = jnp.exp2(s - m_new)
        l_ref[...] = alpha * l_ref[...] + jnp.sum(p, axis=0, keepdims=True)
        v = vt_ref[0, 0, :, pl.ds(k0, TK)]
        acc_ref[...] = alpha * acc_ref[...] + jnp.dot(
            v, p.astype(bf16), preferred_element_type=f32)
        m_ref[...] = m_new

    half = lax.shift_right_logical(qi, 1)
    scores(0, 0)

    def body(jj, carry):
        scores(2 * jj + 1, 1)
        softmax_pv(2 * jj, 0, False)
        scores(2 * jj + 2, 0)
        softmax_pv(2 * jj + 1, 1, False)
        return carry

    lax.fori_loop(0, half, body, 0)

    @pl.when((qi & 1) == 0)
    def _():
        softmax_pv(qi, 0, True)

    @pl.when((qi & 1) == 1)
    def _():
        scores(qi, 1)
        softmax_pv(qi - 1, 0, False)
        softmax_pv(qi, 1, True)

    o = acc_ref[...] / l_ref[...]
    o_ref[0] = o.T.astype(bf16)


def _attention(qt, kc, vt):
    return pl.pallas_call(
        _attn_kernel,
        grid=(B, H, S // TQ),
        in_specs=[
            pl.BlockSpec((1, 1, DQK, TQ), lambda b, h, q: (b, h, 0, q)),
            pl.BlockSpec((1, 1, S, DQK), lambda b, h, q: (b, h, 0, 0)),
            pl.BlockSpec((1, 1, DV, S), lambda b, h, q: (b, h, 0, 0)),
        ],
        out_specs=pl.BlockSpec((1, TQ, DV), lambda b, h, q: (b, q, h)),
        out_shape=jax.ShapeDtypeStruct((B, S, H * DV), bf16),
        scratch_shapes=[pltpu.VMEM((1, TQ), f32), pltpu.VMEM((1, TQ), f32),
                        pltpu.VMEM((DV, TQ), f32), pltpu.VMEM((2, TK, TQ), f32)],
        compiler_params=_cp(3),
        name="attn",
    )(qt, kc, vt)


def _postattn_kernel(x_ref, o_ref, mod_ref, wo_ref, n2g_ref, rw_ref, rb_ref,
                     xo_ref, h2p_ref, eidx_ref, ewt_ref):
    g1 = mod_ref[0, 2:3, :]
    mix = jnp.dot(o_ref[0], wo_ref[...], preferred_element_type=f32)
    x_new = x_ref[0] + g1 * mix
    xo_ref[0] = x_new
    _route_and_pack(x_new, mod_ref, n2g_ref, rw_ref, rb_ref, h2p_ref, eidx_ref, ewt_ref)


def _postattn(x, o, mod_l, wo, n2g, rw, rb):
    out_specs, out_shape = _tok_specs()
    return pl.pallas_call(
        _postattn_kernel,
        grid=(B, NT),
        in_specs=[
            pl.BlockSpec((1, TM, D), lambda b, s: (b, s, 0)),
            pl.BlockSpec((1, TM, H * DV), lambda b, s: (b, s, 0)),
            pl.BlockSpec((1, 6, D), lambda b, s: (b, 0, 0)),
            _full((H * DV, D)), _full((1, D)), _full((2 * E, D)), _full((E, 1)),
        ],
        out_specs=out_specs,
        out_shape=out_shape,
        compiler_params=_cp(2),
        name="postattn",
    )(x, o, mod_l, wo, n2g, rw, rb)


def _row_copy(src_ref, dst_ref, sem, src_row, dst_row):
    return pltpu.make_async_copy(
        src_ref.at[pl.ds(src_row, 1)], dst_ref.at[pl.ds(dst_row, 1)], sem)


def _gather_kernel(idx_ref, src_ref, dst_ref, sem):
    def issue(r, carry):
        _row_copy(src_ref, dst_ref, sem, idx_ref[r], r).start()
        return carry

    lax.fori_loop(0, GR, issue, 0, unroll=8)

    def drain(r, carry):
        _row_copy(src_ref, dst_ref, sem, 0, r).wait()
        return carry

    lax.fori_loop(0, GR, drain, 0, unroll=8)


def _gather_rows(src, idx):
    n = idx.shape[0]
    return pl.pallas_call(
        _gather_kernel,
        grid=(n // GR,),
        in_specs=[
            pl.BlockSpec((GR,), lambda i: (i,), memory_space=pltpu.SMEM),
            pl.BlockSpec(memory_space=pl.ANY),
        ],
        out_specs=pl.BlockSpec((GR, src.shape[1]), lambda i: (i, 0)),
        out_shape=jax.ShapeDtypeStruct((n, src.shape[1]), src.dtype),
        scratch_shapes=[pltpu.SemaphoreType.DMA],
        compiler_params=_cp(1),
        name="gather_rows",
    )(idx, src)


def _expert_kernel(be_ref, first_ref, nv_ref, xb_ref, wg_ref, wu_ref, wd_ref, yb_ref,
                   wgu_bf, wd_bf):
    i = pl.program_id(0)
    nv = nv_ref[i]

    @pl.when(first_ref[i] == 1)
    def _():
        wgu_bf[:, :DE] = wg_ref[0, 0].astype(bf16)
        wgu_bf[:, DE:] = wu_ref[0, 0].astype(bf16)
        wd_bf[...] = wd_ref[0, 0].astype(bf16)

    @pl.when(nv > 0)
    def _():
        x = _unpack_rows(xb_ref[...]).astype(bf16)
        gu = jnp.dot(x, wgu_bf[...], preferred_element_type=f32)
        g = gu[:, :DE]
        hmid = (g * jax.nn.sigmoid(g) * gu[:, DE:]).astype(bf16)
        y = jnp.dot(hmid, wd_bf[...], preferred_element_type=f32)
        yb_ref[...] = _pack_pair(y[:, :HP], y[:, HP:])

    @pl.when(nv == 0)
    def _():
        yb_ref[...] = jnp.zeros((RB, HP), u32)


def _experts(layer, blk_e, blk_first, blk_nv, xb, wg, wu, wd):
    grid_spec = pltpu.PrefetchScalarGridSpec(
        num_scalar_prefetch=3,
        grid=(NB,),
        in_specs=[
            pl.BlockSpec((RB, HP), lambda i, be, bf, nv: (i, 0)),
            pl.BlockSpec((1, 1, D, DE), lambda i, be, bf, nv: (layer, be[i], 0, 0)),
            pl.BlockSpec((1, 1, D, DE), lambda i, be, bf, nv: (layer, be[i], 0, 0)),
            pl.BlockSpec((1, 1, DE, D), lambda i, be, bf, nv: (layer, be[i], 0, 0)),
        ],
        out_specs=pl.BlockSpec((RB, HP), lambda i, be, bf, nv: (i, 0)),
        scratch_shapes=[pltpu.VMEM((D, 2 * DE), bf16), pltpu.VMEM((DE, D), bf16)],
    )
    return pl.pallas_call(
        _expert_kernel,
        grid_spec=grid_spec,
        out_shape=jax.ShapeDtypeStruct((N_ROWS, HP), u32),
        compiler_params=_cp(1),
        name="experts",
    )(blk_e, blk_first, blk_nv, xb, wg, wu, wd)


PCH = 256
NBP = 256


def _route_pos_kernel(eidx_ref, dest_ref, blk_ref, pre_ref):
    upper = (lax.broadcasted_iota(i32, (PCH, PCH), 0)
             < lax.broadcasted_iota(i32, (PCH, PCH), 1)).astype(bf16)
    eio = lax.broadcasted_iota(i32, (E, PCH), 0)

    def onehot(c0, width, eiota):
        e0 = eidx_ref[0:1, pl.ds(c0, width)]
        e1 = eidx_ref[1:2, pl.ds(c0, width)]
        return jnp.concatenate([e0 == eiota, e1 == eiota], axis=0)

    ones_r = jnp.ones((8, PCH), bf16)

    def count(c, carry):
        col, row = carry
        c0 = pl.multiple_of(c * PCH, PCH)
        oh = onehot(c0, PCH, eio).astype(bf16)
        pre_ref[:, pl.ds(c0, PCH)] = col + jnp.dot(oh, upper, preferred_element_type=f32)
        col = col + jnp.sum(oh.astype(f32), axis=1, keepdims=True)
        row = row + _dot_nt(ones_r, oh[:E]) + _dot_nt(ones_r, oh[E:])
        return col, row

    tot_c, tot_r = lax.fori_loop(
        0, T // PCH, count, (jnp.zeros((2 * E, 1), f32), jnp.zeros((8, E), f32)))
    cnt0_c = tot_c[:E]
    cnt_c = tot_c[:E] + tot_c[E:]
    shift = RB.bit_length() - 1
    assert RB == 1 << shift
    padded_r = lax.shift_left(
        lax.shift_right_logical(tot_r[0:1].astype(i32) + (RB - 1), shift), shift).astype(f32)
    ee = lax.broadcasted_iota(i32, (E, E), 0)
    ep = lax.broadcasted_iota(i32, (E, E), 1)
    start_col = jnp.sum(jnp.where(ep < ee, padded_r, 0.0), axis=1, keepdims=True)
    end_col = jnp.sum(jnp.where(ep <= ee, padded_r, 0.0), axis=1, keepdims=True)
    vend_col = start_col + cnt_c
    basef = jnp.concatenate([start_col, start_col + cnt0_c], axis=0)

    wide = 2048
    eiow = lax.broadcasted_iota(i32, (E, wide), 0)

    def place(c, carry):
        c0 = pl.multiple_of(c * wide, wide)
        oh = onehot(c0, wide, eiow)
        val = jnp.where(oh, pre_ref[:, pl.ds(c0, wide)] + basef, 0.0)
        d0 = jnp.sum(val[:E], axis=0, keepdims=True)
        d1 = jnp.sum(val[E:], axis=0, keepdims=True)
        dest_ref[0:1, pl.ds(c0, wide)] = d0.astype(i32)
        dest_ref[1:2, pl.ds(c0, wide)] = d1.astype(i32)
        return carry

    lax.fori_loop(0, T // wide, place, 0)

    start_b = (lax.broadcasted_iota(i32, (1, NBP), 1) * RB).astype(f32)
    be = jnp.minimum(jnp.sum((end_col <= start_b).astype(f32), axis=0, keepdims=True), E - 1.0)
    bprev = jnp.minimum(
        jnp.sum((end_col <= start_b - RB).astype(f32), axis=0, keepdims=True), E - 1.0)
    first = jnp.logical_or(start_b == 0.0, be != bprev)
    eion = lax.broadcasted_iota(i32, (E, NBP), 0).astype(f32)
    vend_b = jnp.sum(jnp.where(eion == be, vend_col, 0.0), axis=0, keepdims=True)
    nv = jnp.clip(vend_b - start_b, 0.0, float(RB))
    blk_ref[...] = jnp.zeros((8, NBP), i32)
    blk_ref[0:1, :] = be.astype(i32)
    blk_ref[1:2, :] = first.astype(i32)
    blk_ref[2:3, :] = nv.astype(i32)


def _route_pos(eidx):
    return pl.pallas_call(
        _route_pos_kernel,
        in_specs=[pl.BlockSpec(memory_space=pltpu.VMEM)],
        out_specs=[pl.BlockSpec(memory_space=pltpu.VMEM)] * 2,
        out_shape=[jax.ShapeDtypeStruct((TOPK, T), i32), jax.ShapeDtypeStruct((8, NBP), i32)],
        scratch_shapes=[pltpu.VMEM((2 * E, T), f32)],
        compiler_params=pltpu.CompilerParams(vmem_limit_bytes=VMEM_LIMIT),
        name="route_pos",
    )(eidx)


TD = 1024
ZR = 1024


def _dispatch_kernel(d0_ref, d1_ref, h2p_ref, xb_ref, zbuf, sem):
    @pl.when(pl.program_id(0) == 0)
    def _():
        zbuf[...] = jnp.zeros((ZR, HP), u32)
        for j in range(N_ROWS // ZR):
            pltpu.make_async_copy(zbuf, xb_ref.at[pl.ds(j * ZR, ZR)], sem).start()
        for j in range(N_ROWS // ZR):
            pltpu.make_async_copy(zbuf, xb_ref.at[pl.ds(j * ZR, ZR)], sem).wait()

    def issue(r, carry):
        _row_copy(h2p_ref, xb_ref, sem, r, d0_ref[r]).start()
        _row_copy(h2p_ref, xb_ref, sem, r, d1_ref[r]).start()
        return carry

    lax.fori_loop(0, TD, issue, 0, unroll=8)

    def drain(r, carry):
        _row_copy(h2p_ref, xb_ref, sem, 0, 0).wait()
        _row_copy(h2p_ref, xb_ref, sem, 0, 0).wait()
        return carry

    lax.fori_loop(0, TD, drain, 0, unroll=8)


def _dispatch(h2p, d0, d1):
    return pl.pallas_call(
        _dispatch_kernel,
        grid=(T // TD,),
        in_specs=[
            pl.BlockSpec((TD,), lambda i: (i,), memory_space=pltpu.SMEM),
            pl.BlockSpec((TD,), lambda i: (i,), memory_space=pltpu.SMEM),
            pl.BlockSpec((TD, HP), lambda i: (i, 0)),
        ],
        out_specs=pl.BlockSpec(memory_space=pl.ANY),
        out_shape=jax.ShapeDtypeStruct((N_ROWS, HP), u32),
        scratch_shapes=[pltpu.VMEM((ZR, HP), u32), pltpu.SemaphoreType.DMA],
        compiler_params=_cp(1),
        name="dispatch",
    )(d0, d1, h2p)


def _final_kernel(x_ref, yg_ref, ewp_ref, modp_ref, g_ref, o_ref):
    o_ref[0] = _rms(_moe_combine(x_ref, yg_ref, ewp_ref, modp_ref), g_ref[...])


def _final(x, moe_in, g):
    return pl.pallas_call(
        _final_kernel,
        grid=(B, NT),
        in_specs=[pl.BlockSpec((1, TM, D), lambda b, s: (b, s, 0))] + _moe_in_specs()
        + [_full((1, D))],
        out_specs=pl.BlockSpec((1, TM, D), lambda b, s: (b, s, 0)),
        out_shape=jax.ShapeDtypeStruct((B, S, D), f32),
        compiler_params=_cp(2),
        name="final_norm",
    )(x, *moe_in, g)


def _moe(layer, h2p, eidx, wg, wu, wd):
    dest, blk = _route_pos(eidx)
    xb = _dispatch(h2p, dest[0], dest[1])
    yb = _experts(layer, blk[0, :NB], blk[1, :NB], blk[2, :NB], xb, wg, wu, wd)
    yg = _gather_rows(yb, dest.reshape(-1))
    return yg.reshape(TOPK, T, HP)


def kernel(x, c, positions, ada_w, ada_b, norm1_g, norm2_g, conv_w_in, conv_b_in, conv_w_dw,
           conv_b_dw, conv_ln_g, conv_ln_b, conv_w_out, kv_in_g, w_dkv, kv_norm_g, w_ukv, w_dq,
           q_norm_g, w_uq, w_o, router_w, router_b, exp_w_gate, exp_w_up, exp_w_down, final_g):
    c8 = jnp.pad(c, ((0, 8 - B), (0, 0)))
    mod = _ada(c8, ada_w, ada_b[:, None, :])[:, :B].reshape(DEPTH, B, 6, D)

    invf = (ROPE_THETA ** (-jnp.arange(0, DR, 2, dtype=f32) / DR))[:, None]
    cos_t, sin_t = _rope_tables(positions[:, None, :], invf)

    rwt = router_w.astype(f32).T
    rw_hi = rwt.astype(bf16)
    rw_lo = (rwt - rw_hi.astype(f32)).astype(bf16)
    rw = jnp.concatenate([rw_hi, rw_lo], axis=0)
    rb = router_b.astype(f32)[:, None]

    w_ukv3 = w_ukv.reshape(KVR, H, DN + DV)
    wuk = w_ukv3[:, :, :DN].reshape(KVR, H * DN).astype(bf16)
    wuvt = w_ukv3[:, :, DN:].reshape(KVR, H * DV).T.astype(bf16)
    kv_w = (kv_in_g[None, :], w_dkv[:, :KVR].astype(bf16), w_dkv[:, KVR:].T.astype(bf16),
            kv_norm_g[None, :], wuk, wuvt)

    moe_in = None
    kc = vt = None
    for l in range(DEPTH):
        if l < N_A:
            wdw = jnp.pad(conv_w_dw[l], ((0, HALO - CONV_W), (0, 0))).reshape(HALO, CG, 128)
            x, h2p, eidx, ew = _conv_layer(
                x, moe_in, mod[l], norm1_g[l][None, :], conv_w_in[l].astype(bf16),
                conv_b_in[l][None, :], wdw, conv_b_dw[l].reshape(CG, 128), conv_ln_g[l][None, :],
                conv_ln_b[l][None, :], conv_w_out[l].astype(bf16), norm2_g[l][None, :], rw, rb)
        else:
            j = l - N_A
            outs = _preattn(
                x, moe_in, mod[l], norm1_g[l][None, :], w_dq[j].astype(bf16),
                q_norm_g[j][None, :], w_uq[j].T.astype(bf16), cos_t, sin_t,
                kv=kv_w if j == 0 else None)
            if j == 0:
                x, qt, kc, vt = outs
            else:
                x, qt = outs
            o = _attention(qt, kc, vt)
            x, h2p, eidx, ew = _postattn(
                x, o, mod[l], w_o[j].astype(bf16), norm2_g[l][None, :], rw, rb)
        yg = _moe(l, h2p, eidx, exp_w_gate, exp_w_up, exp_w_down)
        moe_in = (yg, ew, mod[l])
    return _final(x, moe_in, final_g[None, :])
```

```python
import functools
import math

import jax
import jax.numpy as jnp
from jax import lax
from jax.experimental import pallas as pl
from jax.experimental.pallas import tpu as pltpu

f32 = jnp.float32
bf16 = jnp.bfloat16
u32 = jnp.uint32
i32 = jnp.int32

D = 1024
B = 4
S = 4096
T = B * S
DEPTH = 4
N_A = 2
CONV_W = 31
C = 1024
H = 8
DN = 128
DR = 64
DV = 128
DQK = DN + DR
QR = 384
KVR = 256
ROPE_THETA = 10000.0
ATTN_SCALE = 1.0 / math.sqrt(DN + DR)
QSCALE = ATTN_SCALE * math.log2(math.e)
E = 16
NG = 4
EG = 4
TOPK = 2
DE = 512
EPS = 1e-6
NEG = -1e30

TM = 512
NT = S // TM
HALO = 32
CG = C // 128
OB = 8
TQ = 512
TK = 512
RB = 512
N_ROWS = T * TOPK + E * RB
NB = N_ROWS // RB
GR = 2048
HP = D // 2
PG = HP // 128

VMEM_LIMIT = 56 * 1024 * 1024


def _cp(n_axes):
    return pltpu.CompilerParams(
        dimension_semantics=("arbitrary",) * n_axes, vmem_limit_bytes=VMEM_LIMIT)


def _full(shape):
    n = len(shape)
    return pl.BlockSpec(shape, lambda *_: (0,) * n)


def _rms(x, g):
    return x * lax.rsqrt(jnp.mean(x * x, axis=-1, keepdims=True) + EPS) * g


def _pack_pair(a, b):
    ua = lax.bitcast_convert_type(a.astype(bf16).astype(f32), u32)
    ub = lax.bitcast_convert_type(b.astype(bf16).astype(f32), u32)
    return (ua >> 16) | ub


def _unpack_pair(w):
    a = lax.bitcast_convert_type(w << 16, f32)
    b = lax.bitcast_convert_type(w & jnp.uint32(0xFFFF0000), f32)
    return a, b


def _unpack_rows(w):
    a, b = _unpack_pair(w)
    return jnp.concatenate([a, b], axis=1)


def _load_rows(ref, rows):
    return jnp.concatenate([ref[pl.ds(k, rows, stride=PG), :] for k in range(PG)], axis=1)


def _store_rows(ref, w, rows):
    for k in range(PG):
        ref[pl.ds(k, rows, stride=PG), :] = w[:, 128 * k:128 * (k + 1)]


def _dot_nt(a, b):
    return lax.dot_general(a, b, (((1,), (1,)), ((), ())), preferred_element_type=f32)


def _top2sum4(a, b, c, d):
    lo1, hi1 = jnp.minimum(a, b), jnp.maximum(a, b)
    lo2, hi2 = jnp.minimum(c, d), jnp.maximum(c, d)
    return jnp.maximum(hi1, hi2) + jnp.maximum(jnp.minimum(hi1, hi2), jnp.maximum(lo1, lo2))


def _route_and_pack(x_new, mod_ref, n2g_ref, rw_ref, rb_ref, h2p_ref, eidx_ref, ewt_ref):
    sh2 = mod_ref[0, 3:4, :]
    sc2 = mod_ref[0, 4:5, :]
    h2 = _rms(x_new, n2g_ref[...]) * (1.0 + sc2) + sh2
    _store_rows(h2p_ref, _pack_pair(h2[:, :HP], h2[:, HP:]), TM)

    hi = h2.astype(bf16)
    lo = (h2 - hi.astype(f32)).astype(bf16)
    rw = rw_ref[...]
    p = _dot_nt(rw, hi) + _dot_nt(rw, lo)
    logits = p[:E] + p[E:]
    s = jax.nn.sigmoid(logits)
    sb = s + rb_ref[...]

    srow = [s[e:e + 1] for e in range(E)]
    brow = [sb[e:e + 1] for e in range(E)]
    gs = [_top2sum4(*brow[EG * g:EG * g + EG]) for g in range(NG)]
    gidx = jnp.zeros_like(gs[0], dtype=i32)
    best = gs[0]
    for g in range(1, NG):
        upd = gs[g] > best
        gidx = jnp.where(upd, g, gidx)
        best = jnp.where(upd, gs[g], best)

    def pick(rows, i):
        out = rows[i]
        for g in range(1, NG):
            out = jnp.where(gidx == g, rows[EG * g + i], out)
        return out

    wv = [pick(brow, i) for i in range(EG)]
    sv = [pick(srow, i) for i in range(EG)]

    l1 = jnp.zeros_like(gidx)
    b1 = wv[0]
    for i in range(1, EG):
        upd = wv[i] > b1
        l1 = jnp.where(upd, i, l1)
        b1 = jnp.where(upd, wv[i], b1)
    l2 = jnp.zeros_like(gidx)
    b2 = jnp.where(l1 == 0, -jnp.inf, wv[0])
    for i in range(1, EG):
        cand = jnp.where(l1 == i, -jnp.inf, wv[i])
        upd = cand > b2
        l2 = jnp.where(upd, i, l2)
        b2 = jnp.where(upd, cand, b2)

    def sel(loc):
        out = sv[0]
        for i in range(1, EG):
            out = jnp.where(loc == i, sv[i], out)
        return out

    w1 = sel(l1)
    w2 = sel(l2)
    tot = w1 + w2
    eidx_ref[...] = jnp.concatenate([gidx * EG + l1, gidx * EG + l2], axis=0)
    wt = jnp.concatenate([w1 / tot, w2 / tot, jnp.zeros((128 - TOPK, TM), f32)], axis=0)
    ewt_ref[...] = wt.T


def _moe_combine(x_ref, yg0_ref, yg1_ref, ewt_ref, modp_ref):
    y0 = _unpack_rows(_load_rows(yg0_ref, TM))
    y1 = _unpack_rows(_load_rows(yg1_ref, TM))
    w = ewt_ref[...]
    g2 = modp_ref[0, 5:6, :]
    return x_ref[0] + g2 * (w[:, 0:1] * y0 + w[:, 1:2] * y1)


def _ada_kernel(c_ref, w_ref, b_ref, o_ref):
    c = c_ref[...]
    ca = (c * jax.nn.sigmoid(c)).astype(bf16)
    o_ref[0] = jnp.dot(ca, w_ref[0].astype(bf16), preferred_element_type=f32) + b_ref[0]


def _ada(c8, ada_w, ada_b3):
    tn = 1536
    return pl.pallas_call(
        _ada_kernel,
        grid=(DEPTH, 6 * D // tn),
        in_specs=[
            pl.BlockSpec((8, D), lambda l, n: (0, 0)),
            pl.BlockSpec((1, D, tn), lambda l, n: (l, 0, n)),
            pl.BlockSpec((1, 1, tn), lambda l, n: (l, 0, n)),
        ],
        out_specs=pl.BlockSpec((1, 8, tn), lambda l, n: (l, 0, n)),
        out_shape=jax.ShapeDtypeStruct((DEPTH, 8, 6 * D), f32),
        compiler_params=_cp(2),
        name="ada",
    )(c8, ada_w, ada_b3)


def _rope_kernel(pos_ref, invf_ref, cos_ref, sin_ref):
    ang = pos_ref[0].astype(f32) * invf_ref[...]
    cos_ref[0] = jnp.cos(ang)
    sin_ref[0] = jnp.sin(ang)


def _rope_tables(pos3, invf):
    return pl.pallas_call(
        _rope_kernel,
        grid=(B,),
        in_specs=[pl.BlockSpec((1, 1, S), lambda b: (b, 0, 0)), _full((DR // 2, 1))],
        out_specs=[pl.BlockSpec((1, DR // 2, S), lambda b: (b, 0, 0))] * 2,
        out_shape=[jax.ShapeDtypeStruct((B, DR // 2, S), f32)] * 2,
        compiler_params=_cp(1),
        name="rope",
    )(pos3, invf)


def _conv_kernel(has_moe_in, *refs):
    if has_moe_in:
        x_ref, yg0_ref, yg1_ref, ewp_ref, modp_ref = refs[:5]
        refs = refs[5:]
    else:
        x_ref = refs[0]
        refs = refs[1:]
    (mod_ref, n1g_ref, win_ref, bin_ref, wdw_ref, bdw_ref, lng_ref, lnb_ref, wout_ref,
     n2g_ref, rw_ref, rb_ref,
     xo_ref, h2p_ref, eidx_ref, ewt_ref, ubuf, obuf) = refs
    si = pl.program_id(1)

    x = _moe_combine(x_ref, yg0_ref, yg1_ref, ewp_ref, modp_ref) if has_moe_in else x_ref[0]
    sh1 = mod_ref[0, 0:1, :]
    sc1 = mod_ref[0, 1:2, :]
    g1 = mod_ref[0, 2:3, :]
    h = (_rms(x, n1g_ref[...]) * (1.0 + sc1) + sh1).astype(bf16)
    u = jnp.dot(h, win_ref[...], preferred_element_type=f32) + bin_ref[...]
    glu = u[:, :C] * jax.nn.sigmoid(u[:, C:])

    @pl.when(si == 0)
    def _():
        ubuf[0:HALO * CG, :] = jnp.zeros((HALO * CG, 128), f32)

    @pl.when(si > 0)
    def _():
        ubuf[0:HALO * CG, :] = ubuf[TM * CG:(TM + HALO) * CG, :]

    for k in range(CG):
        ubuf[pl.ds(HALO * CG + k, TM, stride=CG), :] = glu[:, 128 * k:128 * (k + 1)]

    off = HALO - (CONV_W - 1)
    taps = [wdw_ref[j] for j in range(CONV_W)]
    bias = bdw_ref[...]

    def block(bi, carry):
        t0 = bi * OB
        acc = [bias] * OB
        for dd in range(OB + CONV_W - 1):
            row = pl.multiple_of((t0 + off + dd) * CG, CG)
            uv = ubuf[pl.ds(row, CG), :]
            for o in range(OB):
                j = dd - o
                if 0 <= j < CONV_W:
                    acc[o] = acc[o] + uv * taps[j]
        for o in range(OB):
            obuf[pl.ds(pl.multiple_of((t0 + o) * CG, CG), CG), :] = acc[o]
        return carry

    lax.fori_loop(0, TM // OB, block, 0)

    v = jnp.concatenate([obuf[pl.ds(k, TM, stride=CG), :] for k in range(CG)], axis=1)
    mu = jnp.mean(v, axis=-1, keepdims=True)
    vc = v - mu
    var = jnp.mean(vc * vc, axis=-1, keepdims=True)
    y = vc * lax.rsqrt(var + EPS) * lng_ref[...] + lnb_ref[...]
    y = (y * jax.nn.sigmoid(y)).astype(bf16)
    mix = jnp.dot(y, wout_ref[...], preferred_element_type=f32)
    x_new = x + g1 * mix
    xo_ref[0] = x_new
    _route_and_pack(x_new, mod_ref, n2g_ref, rw_ref, rb_ref, h2p_ref, eidx_ref, ewt_ref)


def _tok_specs():
    specs = [
        pl.BlockSpec((1, TM, D), lambda b, s: (b, s, 0)),
        pl.BlockSpec((TM * PG, 128), lambda b, s: (b * NT + s, 0)),
        pl.BlockSpec((TOPK, TM), lambda b, s: (0, b * NT + s)),
        pl.BlockSpec((TM, 128), lambda b, s: (b * NT + s, 0)),
    ]
    shapes = [
        jax.ShapeDtypeStruct((B, S, D), f32),
        jax.ShapeDtypeStruct((T * PG, 128), u32),
        jax.ShapeDtypeStruct((TOPK, T), i32),
        jax.ShapeDtypeStruct((T, 128), f32),
    ]
    return specs, shapes


def _moe_in_specs():
    return [
        pl.BlockSpec((TM * PG, 128), lambda b, s: (b * NT + s, 0)),
        pl.BlockSpec((TM * PG, 128), lambda b, s: (T // TM + b * NT + s, 0)),
        pl.BlockSpec((TM, 128), lambda b, s: (b * NT + s, 0)),
        pl.BlockSpec((1, 6, D), lambda b, s: (b, 0, 0)),
    ]


def _conv_layer(x, moe_in, mod_l, n1g, win, b_in, wdw, bdw, lng, lnb, wout, n2g, rw, rb):
    has_moe_in = moe_in is not None
    in_specs = [pl.BlockSpec((1, TM, D), lambda b, s: (b, s, 0))]
    args = [x]
    if has_moe_in:
        in_specs += _moe_in_specs()
        args += list(moe_in)
    in_specs += [
        pl.BlockSpec((1, 6, D), lambda b, s: (b, 0, 0)),
        _full((1, D)), _full((D, 2 * C)), _full((1, 2 * C)), _full((HALO, CG, 128)),
        _full((CG, 128)),
        _full((1, C)), _full((1, C)), _full((C, D)), _full((1, D)), _full((2 * E, D)),
        _full((E, 1)),
    ]
    args += [mod_l, n1g, win, b_in, wdw, bdw, lng, lnb, wout, n2g, rw, rb]
    out_specs, out_shape = _tok_specs()
    return pl.pallas_call(
        functools.partial(_conv_kernel, has_moe_in),
        grid=(B, NT),
        in_specs=in_specs,
        out_specs=out_specs,
        out_shape=out_shape,
        scratch_shapes=[pltpu.VMEM(((TM + HALO) * CG, 128), f32),
                        pltpu.VMEM((TM * CG, 128), f32)],
        compiler_params=_cp(2),
        name="conv_layer",
    )(*args)


def _preattn_kernel(with_kv, *refs):
    (x_ref, yg0_ref, yg1_ref, ewp_ref, modp_ref, mod_ref, n1g_ref, wdq_ref, qng_ref, wuqt_ref,
     cos_ref, sin_ref) = refs[:12]
    refs = refs[12:]
    if with_kv:
        kvg_ref, wlat_ref, wropet_ref, kvng_ref, wuk_ref, wuvt_ref = refs[:6]
        refs = refs[6:]
        xo_ref, qt_ref, kc_ref, vt_ref = refs
    else:
        xo_ref, qt_ref = refs

    x = _moe_combine(x_ref, yg0_ref, yg1_ref, ewp_ref, modp_ref)
    xo_ref[0] = x
    cos = cos_ref[0]
    sin = sin_ref[0]
    sh1 = mod_ref[0, 0:1, :]
    sc1 = mod_ref[0, 1:2, :]
    h = (_rms(x, n1g_ref[...]) * (1.0 + sc1) + sh1).astype(bf16)
    cq = jnp.dot(h, wdq_ref[...], preferred_element_type=f32)
    cq = _rms(cq, qng_ref[...]).astype(bf16)
    qt = _dot_nt(wuqt_ref[...], cq)
    hr = DR // 2
    for hh in range(H):
        r0 = hh * DQK
        x1 = qt[r0 + DN:r0 + DN + hr]
        x2 = qt[r0 + DN + hr:r0 + DQK]
        qt_ref[0, hh, 0:DN, :] = (qt[r0:r0 + DN] * QSCALE).astype(bf16)
        qt_ref[0, hh, DN:DN + hr, :] = ((x1 * cos - x2 * sin) * QSCALE).astype(bf16)
        qt_ref[0, hh, DN + hr:DQK, :] = ((x1 * sin + x2 * cos) * QSCALE).astype(bf16)

    if with_kv:
        xn = _rms(x, kvg_ref[...]).astype(bf16)
        lat = jnp.dot(xn, wlat_ref[...], preferred_element_type=f32)
        latn = _rms(lat, kvng_ref[...]).astype(bf16)
        krt = _dot_nt(wropet_ref[...], xn)
        k1 = krt[:hr]
        k2 = krt[hr:]
        kr = jnp.concatenate([k1 * cos - k2 * sin, k1 * sin + k2 * cos], axis=0)
        kr = kr.T.astype(bf16)
        kn = jnp.dot(latn, wuk_ref[...], preferred_element_type=f32).astype(bf16)
        vt = _dot_nt(wuvt_ref[...], latn).astype(bf16)
        for hh in range(H):
            kc_ref[0, hh, :, 0:DN] = kn[:, hh * DN:(hh + 1) * DN]
            kc_ref[0, hh, :, DN:DQK] = kr
            vt_ref[0, hh] = vt[hh * DV:(hh + 1) * DV]


def _preattn(x, moe_in, mod_l, n1g, wdq, qng, wuqt, cos_t, sin_t, kv=None):
    with_kv = kv is not None
    in_specs = [pl.BlockSpec((1, TM, D), lambda b, s: (b, s, 0))] + _moe_in_specs() + [
        pl.BlockSpec((1, 6, D), lambda b, s: (b, 0, 0)),
        _full((1, D)), _full((D, QR)), _full((1, QR)), _full((H * DQK, QR)),
        pl.BlockSpec((1, DR // 2, TM), lambda b, s: (b, 0, s)),
        pl.BlockSpec((1, DR // 2, TM), lambda b, s: (b, 0, s)),
    ]
    args = [x, *moe_in, mod_l, n1g, wdq, qng, wuqt, cos_t, sin_t]
    out_specs = [
        pl.BlockSpec((1, TM, D), lambda b, s: (b, s, 0)),
        pl.BlockSpec((1, H, DQK, TM), lambda b, s: (b, 0, 0, s)),
    ]
    out_shape = [
        jax.ShapeDtypeStruct((B, S, D), f32),
        jax.ShapeDtypeStruct((B, H, DQK, S), bf16),
    ]
    if with_kv:
        in_specs += [_full((1, D)), _full((D, KVR)), _full((DR, D)), _full((1, KVR)),
                     _full((KVR, H * DN)), _full((H * DV, KVR))]
        args += list(kv)
        out_specs += [
            pl.BlockSpec((1, H, TM, DQK), lambda b, s: (b, 0, s, 0)),
            pl.BlockSpec((1, H, DV, TM), lambda b, s: (b, 0, 0, s)),
        ]
        out_shape += [
            jax.ShapeDtypeStruct((B, H, S, DQK), bf16),
            jax.ShapeDtypeStruct((B, H, DV, S), bf16),
        ]
    return pl.pallas_call(
        functools.partial(_preattn_kernel, with_kv),
        grid=(B, NT),
        in_specs=in_specs,
        out_specs=out_specs,
        out_shape=out_shape,
        compiler_params=_cp(2),
        name="preattn_kv" if with_kv else "preattn",
    )(*args)


def _attn_kernel(qt_ref, kc_ref, vt_ref, o_ref, m_ref, l_ref, acc_ref, s_ref):
    qi = pl.program_id(2)
    q = qt_ref[0, 0]
    m_ref[...] = jnp.full((1, TQ), NEG, f32)
    l_ref[...] = jnp.zeros((1, TQ), f32)
    acc_ref[...] = jnp.zeros((DV, TQ), f32)

    def scores(j, slot):
        k0 = pl.multiple_of(j * TK, TK)
        k = kc_ref[0, 0, pl.ds(k0, TK), :]
        s_ref[slot] = jnp.dot(k, q, preferred_element_type=f32)

    def softmax_pv(j, slot, masked):
        k0 = pl.multiple_of(j * TK, TK)
        s = s_ref[slot]
        if masked:
            kpos = k0 + lax.broadcasted_iota(i32, (TK, TQ), 0)
            qpos = qi * TQ + lax.broadcasted_iota(i32, (TK, TQ), 1)
            s = jnp.where(kpos <= qpos, s, NEG)
        m_old = m_ref[...]
        m_new = jnp.maximum(m_old, jnp.max(s, axis=0, keepdims=True))
        alpha = jnp.exp2(m_old - m_new)
        p = jnp.exp2(s - m_new)
        l_ref[...] = alpha * l_ref[...] + jnp.sum(p, axis=0, keepdims=True)
        v = vt_ref[0, 0, :, pl.ds(k0, TK)]
        acc_ref[...] = alpha * acc_ref[...] + jnp.dot(
            v, p.astype(bf16), preferred_element_type=f32)
        m_ref[...] = m_new

    half = lax.shift_right_logical(qi, 1)
    scores(0, 0)

    def body(jj, carry):
        scores(2 * jj + 1, 1)
        softmax_pv(2 * jj, 0, False)
        scores(2 * jj + 2, 0)
        softmax_pv(2 * jj + 1, 1, False)
        return carry

    lax.fori_loop(0, half, body, 0)

    @pl.when((qi & 1) == 0)
    def _():
        softmax_pv(qi, 0, True)

    @pl.when((qi & 1) == 1)
    def _():
        scores(qi, 1)
        softmax_pv(qi - 1, 0, False)
        softmax_pv(qi, 1, True)

    o = acc_ref[...] / l_ref[...]
    o_ref[0] = o.T.astype(bf16)


def _attention(qt, kc, vt):
    return pl.pallas_call(
        _attn_kernel,
        grid=(B, H, S // TQ),
        in_specs=[
            pl.BlockSpec((1, 1, DQK, TQ), lambda b, h, q: (b, h, 0, q)),
            pl.BlockSpec((1, 1, S, DQK), lambda b, h, q: (b, h, 0, 0)),
            pl.BlockSpec((1, 1, DV, S), lambda b, h, q: (b, h, 0, 0)),
        ],
        out_specs=pl.BlockSpec((1, TQ, DV), lambda b, h, q: (b, q, h)),
        out_shape=jax.ShapeDtypeStruct((B, S, H * DV), bf16),
        scratch_shapes=[pltpu.VMEM((1, TQ), f32), pltpu.VMEM((1, TQ), f32),
                        pltpu.VMEM((DV, TQ), f32), pltpu.VMEM((2, TK, TQ), f32)],
        compiler_params=_cp(3),
        name="attn",
    )(qt, kc, vt)


def _postattn_kernel(x_ref, o_ref, mod_ref, wo_ref, n2g_ref, rw_ref, rb_ref,
                     xo_ref, h2p_ref, eidx_ref, ewt_ref):
    g1 = mod_ref[0, 2:3, :]
    mix = jnp.dot(o_ref[0], wo_ref[...], preferred_element_type=f32)
    x_new = x_ref[0] + g1 * mix
    xo_ref[0] = x_new
    _route_and_pack(x_new, mod_ref, n2g_ref, rw_ref, rb_ref, h2p_ref, eidx_ref, ewt_ref)


def _postattn(x, o, mod_l, wo, n2g, rw, rb):
    out_specs, out_shape = _tok_specs()
    return pl.pallas_call(
        _postattn_kernel,
        grid=(B, NT),
        in_specs=[
            pl.BlockSpec((1, TM, D), lambda b, s: (b, s, 0)),
            pl.BlockSpec((1, TM, H * DV), lambda b, s: (b, s, 0)),
            pl.BlockSpec((1, 6, D), lambda b, s: (b, 0, 0)),
            _full((H * DV, D)), _full((1, D)), _full((2 * E, D)), _full((E, 1)),
        ],
        out_specs=out_specs,
        out_shape=out_shape,
        compiler_params=_cp(2),
        name="postattn",
    )(x, o, mod_l, wo, n2g, rw, rb)


def _row_copy(src_ref, dst_ref, sem, src_row, dst_row):
    return pltpu.make_async_copy(src_ref.at[src_row], dst_ref.at[dst_row], sem)


def _gather_kernel(idx_ref, src_ref, dst_ref, sem):
    def issue(r, carry):
        _row_copy(src_ref, dst_ref, sem, idx_ref[r], r).start()
        return carry

    lax.fori_loop(0, GR, issue, 0, unroll=8)

    def drain(r, carry):
        _row_copy(src_ref, dst_ref, sem, 0, r).wait()
        return carry

    lax.fori_loop(0, GR, drain, 0, unroll=8)


def _gather_rows(src, idx):
    n = idx.shape[0]
    return pl.pallas_call(
        _gather_kernel,
        grid=(n // GR,),
        in_specs=[
            pl.BlockSpec((GR,), lambda i: (i,), memory_space=pltpu.SMEM),
            pl.BlockSpec(memory_space=pl.ANY),
        ],
        out_specs=pl.BlockSpec((GR, PG, 128), lambda i: (i, 0, 0)),
        out_shape=jax.ShapeDtypeStruct((n, PG, 128), src.dtype),
        scratch_shapes=[pltpu.SemaphoreType.DMA],
        compiler_params=_cp(1),
        name="gather_rows",
    )(idx, src)


def _expert_kernel(be_ref, first_ref, nv_ref, xb_ref, wg_ref, wu_ref, wd_ref, yb_ref,
                   wgu_bf, wd_bf):
    i = pl.program_id(0)
    nv = nv_ref[i]

    @pl.when(first_ref[i] == 1)
    def _():
        wgu_bf[:, :DE] = wg_ref[0, 0].astype(bf16)
        wgu_bf[:, DE:] = wu_ref[0, 0].astype(bf16)
        wd_bf[...] = wd_ref[0, 0].astype(bf16)

    @pl.when(nv > 0)
    def _():
        x = _unpack_rows(_load_rows(xb_ref, RB)).astype(bf16)
        gu = jnp.dot(x, wgu_bf[...], preferred_element_type=f32)
        g = gu[:, :DE]
        hmid = (g * jax.nn.sigmoid(g) * gu[:, DE:]).astype(bf16)
        y = jnp.dot(hmid, wd_bf[...], preferred_element_type=f32)
        _store_rows(yb_ref, _pack_pair(y[:, :HP], y[:, HP:]), RB)

    @pl.when(nv == 0)
    def _():
        yb_ref[...] = jnp.zeros((RB * PG, 128), u32)


def _experts(layer, blk_e, blk_first, blk_nv, xb, wg, wu, wd):
    grid_spec = pltpu.PrefetchScalarGridSpec(
        num_scalar_prefetch=3,
        grid=(NB,),
        in_specs=[
            pl.BlockSpec((RB * PG, 128), lambda i, be, bf, nv: (i, 0)),
            pl.BlockSpec((1, 1, D, DE), lambda i, be, bf, nv: (layer, be[i], 0, 0)),
            pl.BlockSpec((1, 1, D, DE), lambda i, be, bf, nv: (layer, be[i], 0, 0)),
            pl.BlockSpec((1, 1, DE, D), lambda i, be, bf, nv: (layer, be[i], 0, 0)),
        ],
        out_specs=pl.BlockSpec((RB * PG, 128), lambda i, be, bf, nv: (i, 0)),
        scratch_shapes=[pltpu.VMEM((D, 2 * DE), bf16), pltpu.VMEM((DE, D), bf16)],
    )
    return pl.pallas_call(
        _expert_kernel,
        grid_spec=grid_spec,
        out_shape=jax.ShapeDtypeStruct((N_ROWS * PG, 128), u32),
        compiler_params=_cp(1),
        name="experts",
    )(blk_e, blk_first, blk_nv, xb, wg, wu, wd)


PCH = 256
NBP = 256


def _route_pos_kernel(eidx_ref, dest_ref, blk_ref, pre_ref):
    upper = (lax.broadcasted_iota(i32, (PCH, PCH), 0)
             < lax.broadcasted_iota(i32, (PCH, PCH), 1)).astype(bf16)
    eio = lax.broadcasted_iota(i32, (E, PCH), 0)

    def onehot(c0, width, eiota):
        e0 = eidx_ref[0:1, pl.ds(c0, width)]
        e1 = eidx_ref[1:2, pl.ds(c0, width)]
        return jnp.concatenate([e0 == eiota, e1 == eiota], axis=0)

    ones_r = jnp.ones((8, PCH), bf16)

    def count(c, carry):
        col, row = carry
        c0 = pl.multiple_of(c * PCH, PCH)
        oh = onehot(c0, PCH, eio).astype(bf16)
        pre_ref[:, pl.ds(c0, PCH)] = col + jnp.dot(oh, upper, preferred_element_type=f32)
        col = col + jnp.sum(oh.astype(f32), axis=1, keepdims=True)
        row = row + _dot_nt(ones_r, oh[:E]) + _dot_nt(ones_r, oh[E:])
        return col, row

    tot_c, tot_r = lax.fori_loop(
        0, T // PCH, count, (jnp.zeros((2 * E, 1), f32), jnp.zeros((8, E), f32)))
    cnt0_c = tot_c[:E]
    cnt_c = tot_c[:E] + tot_c[E:]
    shift = RB.bit_length() - 1
    assert RB == 1 << shift
    padded_r = lax.shift_left(
        lax.shift_right_logical(tot_r[0:1].astype(i32) + (RB - 1), shift), shift).astype(f32)
    ee = lax.broadcasted_iota(i32, (E, E), 0)
    ep = lax.broadcasted_iota(i32, (E, E), 1)
    start_col = jnp.sum(jnp.where(ep < ee, padded_r, 0.0), axis=1, keepdims=True)
    end_col = jnp.sum(jnp.where(ep <= ee, padded_r, 0.0), axis=1, keepdims=True)
    vend_col = start_col + cnt_c
    basef = jnp.concatenate([start_col, start_col + cnt0_c], axis=0)

    wide = 2048
    eiow = lax.broadcasted_iota(i32, (E, wide), 0)

    def place(c, carry):
        c0 = pl.multiple_of(c * wide, wide)
        oh = onehot(c0, wide, eiow)
        val = jnp.where(oh, pre_ref[:, pl.ds(c0, wide)] + basef, 0.0)
        d0 = jnp.sum(val[:E], axis=0, keepdims=True)
        d1 = jnp.sum(val[E:], axis=0, keepdims=True)
        dest_ref[0:1, pl.ds(c0, wide)] = d0.astype(i32)
        dest_ref[1:2, pl.ds(c0, wide)] = d1.astype(i32)
        return carry

    lax.fori_loop(0, T // wide, place, 0)

    start_b = (lax.broadcasted_iota(i32, (1, NBP), 1) * RB).astype(f32)
    be = jnp.minimum(jnp.sum((end_col <= start_b).astype(f32), axis=0, keepdims=True), E - 1.0)
    bprev = jnp.minimum(
        jnp.sum((end_col <= start_b - RB).astype(f32), axis=0, keepdims=True), E - 1.0)
    first = jnp.logical_or(start_b == 0.0, be != bprev)
    eion = lax.broadcasted_iota(i32, (E, NBP), 0).astype(f32)
    vend_b = jnp.sum(jnp.where(eion == be, vend_col, 0.0), axis=0, keepdims=True)
    nv = jnp.clip(vend_b - start_b, 0.0, float(RB))
    blk_ref[...] = jnp.zeros((8, NBP), i32)
    blk_ref[0:1, :] = be.astype(i32)
    blk_ref[1:2, :] = first.astype(i32)
    blk_ref[2:3, :] = nv.astype(i32)


def _route_pos(eidx):
    return pl.pallas_call(
        _route_pos_kernel,
        in_specs=[pl.BlockSpec(memory_space=pltpu.VMEM)],
        out_specs=[pl.BlockSpec(memory_space=pltpu.VMEM)] * 2,
        out_shape=[jax.ShapeDtypeStruct((TOPK, T), i32), jax.ShapeDtypeStruct((8, NBP), i32)],
        scratch_shapes=[pltpu.VMEM((2 * E, T), f32)],
        compiler_params=pltpu.CompilerParams(vmem_limit_bytes=VMEM_LIMIT),
        name="route_pos",
    )(eidx)


TD = 1024
ZR = 1024


def _dispatch_kernel(d0_ref, d1_ref, h2p_ref, xb_ref, zbuf, sem):
    @pl.when(pl.program_id(0) == 0)
    def _():
        zbuf[...] = jnp.zeros((ZR, PG, 128), u32)
        for j in range(N_ROWS // ZR):
            pltpu.make_async_copy(zbuf, xb_ref.at[pl.ds(j * ZR, ZR)], sem).start()
        for j in range(N_ROWS // ZR):
            pltpu.make_async_copy(zbuf, xb_ref.at[pl.ds(j * ZR, ZR)], sem).wait()

    def issue(r, carry):
        _row_copy(h2p_ref, xb_ref, sem, r, d0_ref[r]).start()
        _row_copy(h2p_ref, xb_ref, sem, r, d1_ref[r]).start()
        return carry

    lax.fori_loop(0, TD, issue, 0, unroll=8)

    def drain(r, carry):
        _row_copy(h2p_ref, xb_ref, sem, 0, 0).wait()
        _row_copy(h2p_ref, xb_ref, sem, 0, 0).wait()
        return carry

    lax.fori_loop(0, TD, drain, 0, unroll=8)


def _dispatch(h2p, d0, d1):
    return pl.pallas_call(
        _dispatch_kernel,
        grid=(T // TD,),
        in_specs=[
            pl.BlockSpec((TD,), lambda i: (i,), memory_space=pltpu.SMEM),
            pl.BlockSpec((TD,), lambda i: (i,), memory_space=pltpu.SMEM),
            pl.BlockSpec((TD, PG, 128), lambda i: (i, 0, 0)),
        ],
        out_specs=pl.BlockSpec(memory_space=pl.ANY),
        out_shape=jax.ShapeDtypeStruct((N_ROWS, PG, 128), u32),
        scratch_shapes=[pltpu.VMEM((ZR, PG, 128), u32), pltpu.SemaphoreType.DMA],
        compiler_params=_cp(1),
        name="dispatch",
    )(d0, d1, h2p)


def _final_kernel(x_ref, yg0_ref, yg1_ref, ewp_ref, modp_ref, g_ref, o_ref):
    o_ref[0] = _rms(_moe_combine(x_ref, yg0_ref, yg1_ref, ewp_ref, modp_ref), g_ref[...])


def _final(x, moe_in, g):
    return pl.pallas_call(
        _final_kernel,
        grid=(B, NT),
        in_specs=[pl.BlockSpec((1, TM, D), lambda b, s: (b, s, 0))] + _moe_in_specs()
        + [_full((1, D))],
        out_specs=pl.BlockSpec((1, TM, D), lambda b, s: (b, s, 0)),
        out_shape=jax.ShapeDtypeStruct((B, S, D), f32),
        compiler_params=_cp(2),
        name="final_norm",
    )(x, *moe_in, g)


def _moe(layer, h2p, eidx, wg, wu, wd):
    dest, blk = _route_pos(eidx)
    xb = _dispatch(h2p.reshape(T, PG, 128), dest[0], dest[1])
    yb = _experts(layer, blk[0, :NB], blk[1, :NB], blk[2, :NB], xb.reshape(N_ROWS * PG, 128),
                  wg, wu, wd)
    yg = _gather_rows(yb.reshape(N_ROWS, PG, 128), dest.reshape(-1))
    return yg.reshape(TOPK * T * PG, 128)


def kernel(x, c, positions, ada_w, ada_b, norm1_g, norm2_g, conv_w_in, conv_b_in, conv_w_dw,
           conv_b_dw, conv_ln_g, conv_ln_b, conv_w_out, kv_in_g, w_dkv, kv_norm_g, w_ukv, w_dq,
           q_norm_g, w_uq, w_o, router_w, router_b, exp_w_gate, exp_w_up, exp_w_down, final_g):
    c8 = jnp.pad(c, ((0, 8 - B), (0, 0)))
    mod = _ada(c8, ada_w, ada_b[:, None, :])[:, :B].reshape(DEPTH, B, 6, D)

    invf = (ROPE_THETA ** (-jnp.arange(0, DR, 2, dtype=f32) / DR))[:, None]
    cos_t, sin_t = _rope_tables(positions[:, None, :], invf)

    rwt = router_w.astype(f32).T
    rw_hi = rwt.astype(bf16)
    rw_lo = (rwt - rw_hi.astype(f32)).astype(bf16)
    rw = jnp.concatenate([rw_hi, rw_lo], axis=0)
    rb = router_b.astype(f32)[:, None]

    w_ukv3 = w_ukv.reshape(KVR, H, DN + DV)
    wuk = w_ukv3[:, :, :DN].reshape(KVR, H * DN).astype(bf16)
    wuvt = w_ukv3[:, :, DN:].reshape(KVR, H * DV).T.astype(bf16)
    kv_w = (kv_in_g[None, :], w_dkv[:, :KVR].astype(bf16), w_dkv[:, KVR:].T.astype(bf16),
            kv_norm_g[None, :], wuk, wuvt)

    moe_in = None
    kc = vt = None
    for l in range(DEPTH):
        if l < N_A:
            wdw = jnp.pad(conv_w_dw[l], ((0, HALO - CONV_W), (0, 0))).reshape(HALO, CG, 128)
            x, h2p, eidx, ew = _conv_layer(
                x, moe_in, mod[l], norm1_g[l][None, :], conv_w_in[l].astype(bf16),
                conv_b_in[l][None, :], wdw, conv_b_dw[l].reshape(CG, 128), conv_ln_g[l][None, :],
                conv_ln_b[l][None, :], conv_w_out[l].astype(bf16), norm2_g[l][None, :], rw, rb)
        else:
            j = l - N_A
            outs = _preattn(
                x, moe_in, mod[l], norm1_g[l][None, :], w_dq[j].astype(bf16),
                q_norm_g[j][None, :], w_uq[j].T.astype(bf16), cos_t, sin_t,
                kv=kv_w if j == 0 else None)
            if j == 0:
                x, qt, kc, vt = outs
            else:
                x, qt = outs
            o = _attention(qt, kc, vt)
            x, h2p, eidx, ew = _postattn(
                x, o, mod[l], w_o[j].astype(bf16), norm2_g[l][None, :], rw, rb)
        yg = _moe(l, h2p, eidx, exp_w_gate, exp_w_up, exp_w_down)
        moe_in = (yg, yg, ew, mod[l])
    return _final(x, moe_in, final_g[None, :])
```

```python
import functools
import math

import jax
import jax.numpy as jnp
from jax import lax
from jax.experimental import pallas as pl
from jax.experimental.pallas import tpu as pltpu

f32 = jnp.float32
bf16 = jnp.bfloat16
u32 = jnp.uint32
i32 = jnp.int32

D = 1024
B = 4
S = 4096
T = B * S
DEPTH = 4
N_A = 2
CONV_W = 31
C = 1024
H = 8
DN = 128
DR = 64
DV = 128
DQK = DN + DR
QR = 384
KVR = 256
ROPE_THETA = 10000.0
ATTN_SCALE = 1.0 / math.sqrt(DN + DR)
QSCALE = ATTN_SCALE * math.log2(math.e)
E = 16
NG = 4
EG = 4
TOPK = 2
DE = 512
EPS = 1e-6
NEG = -1e30

TM = 512
NT = S // TM
HALO = 32
CG = C // 128
OB = 8
TQ = 512
TK = 512
RB = 512
N_ROWS = T * TOPK + E * RB
NB = N_ROWS // RB
GR = 2048
DU = 8
HP = D // 2
PG = HP // 128

VMEM_LIMIT = 56 * 1024 * 1024


def _cp(n_axes):
    return pltpu.CompilerParams(
        dimension_semantics=("arbitrary",) * n_axes, vmem_limit_bytes=VMEM_LIMIT)


def _full(shape):
    n = len(shape)
    return pl.BlockSpec(shape, lambda *_: (0,) * n)


def _rms(x, g):
    return x * lax.rsqrt(jnp.mean(x * x, axis=-1, keepdims=True) + EPS) * g


def _sigmoid(x):
    return 0.5 * jnp.tanh(0.5 * x) + 0.5


def _pack_pair(a, b):
    ua = lax.bitcast_convert_type(a.astype(bf16).astype(f32), u32)
    ub = lax.bitcast_convert_type(b.astype(bf16).astype(f32), u32)
    return (ua >> 16) | ub


def _unpack_pair(w):
    a = lax.bitcast_convert_type(w << 16, f32)
    b = lax.bitcast_convert_type(w & jnp.uint32(0xFFFF0000), f32)
    return a, b


def _unpack_rows(w):
    a, b = _unpack_pair(w)
    return jnp.concatenate([a, b], axis=1)


def _load_rows(ref, rows):
    return jnp.concatenate([ref[pl.ds(k, rows, stride=PG), :] for k in range(PG)], axis=1)


def _store_rows(ref, w, rows):
    for k in range(PG):
        ref[pl.ds(k, rows, stride=PG), :] = w[:, 128 * k:128 * (k + 1)]


def _dot_nt(a, b):
    return lax.dot_general(a, b, (((1,), (1,)), ((), ())), preferred_element_type=f32)


def _top2sum4(a, b, c, d):
    lo1, hi1 = jnp.minimum(a, b), jnp.maximum(a, b)
    lo2, hi2 = jnp.minimum(c, d), jnp.maximum(c, d)
    return jnp.maximum(hi1, hi2) + jnp.maximum(jnp.minimum(hi1, hi2), jnp.maximum(lo1, lo2))


def _route_and_pack(x_new, mod_ref, n2g_ref, rw_ref, rb_ref, h2p_ref, eidx_ref, ewt_ref):
    sh2 = mod_ref[0, 3:4, :]
    sc2 = mod_ref[0, 4:5, :]
    h2 = _rms(x_new, n2g_ref[...]) * (1.0 + sc2) + sh2
    _store_rows(h2p_ref, _pack_pair(h2[:, :HP], h2[:, HP:]), TM)

    hi = h2.astype(bf16)
    lo = (h2 - hi.astype(f32)).astype(bf16)
    rw = rw_ref[...]
    p = _dot_nt(rw, hi) + _dot_nt(rw, lo)
    logits = p[:E] + p[E:]
    s = jax.nn.sigmoid(logits)
    sb = s + rb_ref[...]

    srow = [s[e:e + 1] for e in range(E)]
    brow = [sb[e:e + 1] for e in range(E)]
    gs = [_top2sum4(*brow[EG * g:EG * g + EG]) for g in range(NG)]
    gidx = jnp.zeros_like(gs[0], dtype=i32)
    best = gs[0]
    for g in range(1, NG):
        upd = gs[g] > best
        gidx = jnp.where(upd, g, gidx)
        best = jnp.where(upd, gs[g], best)

    def pick(rows, i):
        out = rows[i]
        for g in range(1, NG):
            out = jnp.where(gidx == g, rows[EG * g + i], out)
        return out

    wv = [pick(brow, i) for i in range(EG)]
    sv = [pick(srow, i) for i in range(EG)]

    l1 = jnp.zeros_like(gidx)
    b1 = wv[0]
    for i in range(1, EG):
        upd = wv[i] > b1
        l1 = jnp.where(upd, i, l1)
        b1 = jnp.where(upd, wv[i], b1)
    l2 = jnp.zeros_like(gidx)
    b2 = jnp.where(l1 == 0, -jnp.inf, wv[0])
    for i in range(1, EG):
        cand = jnp.where(l1 == i, -jnp.inf, wv[i])
        upd = cand > b2
        l2 = jnp.where(upd, i, l2)
        b2 = jnp.where(upd, cand, b2)

    def sel(loc):
        out = sv[0]
        for i in range(1, EG):
            out = jnp.where(loc == i, sv[i], out)
        return out

    w1 = sel(l1)
    w2 = sel(l2)
    tot = w1 + w2
    eidx_ref[...] = jnp.concatenate([gidx * EG + l1, gidx * EG + l2], axis=0)
    wt = jnp.concatenate([w1 / tot, w2 / tot, jnp.zeros((128 - TOPK, TM), f32)], axis=0)
    ewt_ref[...] = wt.T


def _moe_combine(x_ref, yg0_ref, yg1_ref, ewt_ref, modp_ref):
    y0 = _unpack_rows(_load_rows(yg0_ref, TM))
    y1 = _unpack_rows(_load_rows(yg1_ref, TM))
    w = ewt_ref[...]
    g2 = modp_ref[0, 5:6, :]
    return x_ref[0] + g2 * (w[:, 0:1] * y0 + w[:, 1:2] * y1)


def _ada_kernel(c_ref, w_ref, b_ref, o_ref):
    c = c_ref[...]
    ca = (c * jax.nn.sigmoid(c)).astype(bf16)
    o_ref[0] = jnp.dot(ca, w_ref[0].astype(bf16), preferred_element_type=f32) + b_ref[0]


def _ada(c8, ada_w, ada_b3):
    tn = 1536
    return pl.pallas_call(
        _ada_kernel,
        grid=(DEPTH, 6 * D // tn),
        in_specs=[
            pl.BlockSpec((8, D), lambda l, n: (0, 0)),
            pl.BlockSpec((1, D, tn), lambda l, n: (l, 0, n)),
            pl.BlockSpec((1, 1, tn), lambda l, n: (l, 0, n)),
        ],
        out_specs=pl.BlockSpec((1, 8, tn), lambda l, n: (l, 0, n)),
        out_shape=jax.ShapeDtypeStruct((DEPTH, 8, 6 * D), f32),
        compiler_params=_cp(2),
        name="ada",
    )(c8, ada_w, ada_b3)


def _rope_kernel(pos_ref, invf_ref, cos_ref, sin_ref):
    ang = pos_ref[0].astype(f32) * invf_ref[...]
    cos_ref[0] = jnp.cos(ang)
    sin_ref[0] = jnp.sin(ang)


def _rope_tables(pos3, invf):
    return pl.pallas_call(
        _rope_kernel,
        grid=(B,),
        in_specs=[pl.BlockSpec((1, 1, S), lambda b: (b, 0, 0)), _full((DR // 2, 1))],
        out_specs=[pl.BlockSpec((1, DR // 2, S), lambda b: (b, 0, 0))] * 2,
        out_shape=[jax.ShapeDtypeStruct((B, DR // 2, S), f32)] * 2,
        compiler_params=_cp(1),
        name="rope",
    )(pos3, invf)


def _conv_kernel(has_moe_in, *refs):
    if has_moe_in:
        x_ref, yg0_ref, yg1_ref, ewp_ref, modp_ref = refs[:5]
        refs = refs[5:]
    else:
        x_ref = refs[0]
        refs = refs[1:]
    (mod_ref, n1g_ref, win_ref, bin_ref, wdw_ref, bdw_ref, lng_ref, lnb_ref, wout_ref,
     n2g_ref, rw_ref, rb_ref,
     xo_ref, h2p_ref, eidx_ref, ewt_ref, ubuf, obuf) = refs
    si = pl.program_id(1)

    x = _moe_combine(x_ref, yg0_ref, yg1_ref, ewp_ref, modp_ref) if has_moe_in else x_ref[0]
    sh1 = mod_ref[0, 0:1, :]
    sc1 = mod_ref[0, 1:2, :]
    g1 = mod_ref[0, 2:3, :]
    h = (_rms(x, n1g_ref[...]) * (1.0 + sc1) + sh1).astype(bf16)
    u = jnp.dot(h, win_ref[...], preferred_element_type=f32) + bin_ref[...]
    glu = u[:, :C] * _sigmoid(u[:, C:])

    @pl.when(si == 0)
    def _():
        ubuf[0:HALO * CG, :] = jnp.zeros((HALO * CG, 128), f32)

    @pl.when(si > 0)
    def _():
        ubuf[0:HALO * CG, :] = ubuf[TM * CG:(TM + HALO) * CG, :]

    for k in range(CG):
        ubuf[pl.ds(HALO * CG + k, TM, stride=CG), :] = glu[:, 128 * k:128 * (k + 1)]

    off = HALO - (CONV_W - 1)
    taps = [wdw_ref[j] for j in range(CONV_W)]
    bias = bdw_ref[...]

    def block(bi, carry):
        t0 = bi * OB
        acc = [bias] * OB
        for dd in range(OB + CONV_W - 1):
            row = pl.multiple_of((t0 + off + dd) * CG, CG)
            uv = ubuf[pl.ds(row, CG), :]
            for o in range(OB):
                j = dd - o
                if 0 <= j < CONV_W:
                    acc[o] = acc[o] + uv * taps[j]
        for o in range(OB):
            obuf[pl.ds(pl.multiple_of((t0 + o) * CG, CG), CG), :] = acc[o]
        return carry

    lax.fori_loop(0, TM // OB, block, 0)

    v = jnp.concatenate([obuf[pl.ds(k, TM, stride=CG), :] for k in range(CG)], axis=1)
    mu = jnp.mean(v, axis=-1, keepdims=True)
    vc = v - mu
    var = jnp.mean(vc * vc, axis=-1, keepdims=True)
    y = vc * lax.rsqrt(var + EPS) * lng_ref[...] + lnb_ref[...]
    y = (y * _sigmoid(y)).astype(bf16)
    mix = jnp.dot(y, wout_ref[...], preferred_element_type=f32)
    x_new = x + g1 * mix
    xo_ref[0] = x_new
    _route_and_pack(x_new, mod_ref, n2g_ref, rw_ref, rb_ref, h2p_ref, eidx_ref, ewt_ref)


def _tok_specs():
    specs = [
        pl.BlockSpec((1, TM, D), lambda b, s: (b, s, 0)),
        pl.BlockSpec((TM * PG, 128), lambda b, s: (b * NT + s, 0)),
        pl.BlockSpec((TOPK, TM), lambda b, s: (0, b * NT + s)),
        pl.BlockSpec((TM, 128), lambda b, s: (b * NT + s, 0)),
    ]
    shapes = [
        jax.ShapeDtypeStruct((B, S, D), f32),
        jax.ShapeDtypeStruct((T * PG, 128), u32),
        jax.ShapeDtypeStruct((TOPK, T), i32),
        jax.ShapeDtypeStruct((T, 128), f32),
    ]
    return specs, shapes


def _moe_in_specs():
    return [
        pl.BlockSpec((TM * PG, 128), lambda b, s: (b * NT + s, 0)),
        pl.BlockSpec((TM * PG, 128), lambda b, s: (T // TM + b * NT + s, 0)),
        pl.BlockSpec((TM, 128), lambda b, s: (b * NT + s, 0)),
        pl.BlockSpec((1, 6, D), lambda b, s: (b, 0, 0)),
    ]


def _conv_layer(x, moe_in, mod_l, n1g, win, b_in, wdw, bdw, lng, lnb, wout, n2g, rw, rb):
    has_moe_in = moe_in is not None
    in_specs = [pl.BlockSpec((1, TM, D), lambda b, s: (b, s, 0))]
    args = [x]
    if has_moe_in:
        in_specs += _moe_in_specs()
        args += list(moe_in)
    in_specs += [
        pl.BlockSpec((1, 6, D), lambda b, s: (b, 0, 0)),
        _full((1, D)), _full((D, 2 * C)), _full((1, 2 * C)), _full((HALO, CG, 128)),
        _full((CG, 128)),
        _full((1, C)), _full((1, C)), _full((C, D)), _full((1, D)), _full((2 * E, D)),
        _full((E, 1)),
    ]
    args += [mod_l, n1g, win, b_in, wdw, bdw, lng, lnb, wout, n2g, rw, rb]
    out_specs, out_shape = _tok_specs()
    return pl.pallas_call(
        functools.partial(_conv_kernel, has_moe_in),
        grid=(B, NT),
        in_specs=in_specs,
        out_specs=out_specs,
        out_shape=out_shape,
        scratch_shapes=[pltpu.VMEM(((TM + HALO) * CG, 128), f32),
                        pltpu.VMEM((TM * CG, 128), f32)],
        compiler_params=_cp(2),
        name="conv_layer",
    )(*args)


def _preattn_kernel(with_kv, *refs):
    (x_ref, yg0_ref, yg1_ref, ewp_ref, modp_ref, mod_ref, n1g_ref, wdq_ref, qng_ref, wuqt_ref,
     cos_ref, sin_ref) = refs[:12]
    refs = refs[12:]
    if with_kv:
        kvg_ref, wlat_ref, wropet_ref, kvng_ref, wuk_ref, wuvt_ref = refs[:6]
        refs = refs[6:]
        xo_ref, qt_ref, kc_ref, vt_ref = refs
    else:
        xo_ref, qt_ref = refs

    x = _moe_combine(x_ref, yg0_ref, yg1_ref, ewp_ref, modp_ref)
    xo_ref[0] = x
    cos = cos_ref[0]
    sin = sin_ref[0]
    sh1 = mod_ref[0, 0:1, :]
    sc1 = mod_ref[0, 1:2, :]
    h = (_rms(x, n1g_ref[...]) * (1.0 + sc1) + sh1).astype(bf16)
    cq = jnp.dot(h, wdq_ref[...], preferred_element_type=f32)
    cq = _rms(cq, qng_ref[...]).astype(bf16)
    qt = _dot_nt(wuqt_ref[...], cq)
    hr = DR // 2
    for hh in range(H):
        r0 = hh * DQK
        x1 = qt[r0 + DN:r0 + DN + hr]
        x2 = qt[r0 + DN + hr:r0 + DQK]
        qt_ref[0, hh, 0:DN, :] = (qt[r0:r0 + DN] * QSCALE).astype(bf16)
        qt_ref[0, hh, DN:DN + hr, :] = ((x1 * cos - x2 * sin) * QSCALE).astype(bf16)
        qt_ref[0, hh, DN + hr:DQK, :] = ((x1 * sin + x2 * cos) * QSCALE).astype(bf16)

    if with_kv:
        xn = _rms(x, kvg_ref[...]).astype(bf16)
        lat = jnp.dot(xn, wlat_ref[...], preferred_element_type=f32)
        latn = _rms(lat, kvng_ref[...]).astype(bf16)
        krt = _dot_nt(wropet_ref[...], xn)
        k1 = krt[:hr]
        k2 = krt[hr:]
        kr = jnp.concatenate([k1 * cos - k2 * sin, k1 * sin + k2 * cos], axis=0)
        kr = kr.T.astype(bf16)
        kn = jnp.dot(latn, wuk_ref[...], preferred_element_type=f32).astype(bf16)
        vt = _dot_nt(wuvt_ref[...], latn).astype(bf16)
        for hh in range(H):
            kc_ref[0, hh, :, 0:DN] = kn[:, hh * DN:(hh + 1) * DN]
            kc_ref[0, hh, :, DN:DQK] = kr
            vt_ref[0, hh] = vt[hh * DV:(hh + 1) * DV]


def _preattn(x, moe_in, mod_l, n1g, wdq, qng, wuqt, cos_t, sin_t, kv=None):
    with_kv = kv is not None
    in_specs = [pl.BlockSpec((1, TM, D), lambda b, s: (b, s, 0))] + _moe_in_specs() + [
        pl.BlockSpec((1, 6, D), lambda b, s: (b, 0, 0)),
        _full((1, D)), _full((D, QR)), _full((1, QR)), _full((H * DQK, QR)),
        pl.BlockSpec((1, DR // 2, TM), lambda b, s: (b, 0, s)),
        pl.BlockSpec((1, DR // 2, TM), lambda b, s: (b, 0, s)),
    ]
    args = [x, *moe_in, mod_l, n1g, wdq, qng, wuqt, cos_t, sin_t]
    out_specs = [
        pl.BlockSpec((1, TM, D), lambda b, s: (b, s, 0)),
        pl.BlockSpec((1, H, DQK, TM), lambda b, s: (b, 0, 0, s)),
    ]
    out_shape = [
        jax.ShapeDtypeStruct((B, S, D), f32),
        jax.ShapeDtypeStruct((B, H, DQK, S), bf16),
    ]
    if with_kv:
        in_specs += [_full((1, D)), _full((D, KVR)), _full((DR, D)), _full((1, KVR)),
                     _full((KVR, H * DN)), _full((H * DV, KVR))]
        args += list(kv)
        out_specs += [
            pl.BlockSpec((1, H, TM, DQK), lambda b, s: (b, 0, s, 0)),
            pl.BlockSpec((1, H, DV, TM), lambda b, s: (b, 0, 0, s)),
        ]
        out_shape += [
            jax.ShapeDtypeStruct((B, H, S, DQK), bf16),
            jax.ShapeDtypeStruct((B, H, DV, S), bf16),
        ]
    return pl.pallas_call(
        functools.partial(_preattn_kernel, with_kv),
        grid=(B, NT),
        in_specs=in_specs,
        out_specs=out_specs,
        out_shape=out_shape,
        compiler_params=_cp(2),
        name="preattn_kv" if with_kv else "preattn",
    )(*args)


def _attn_kernel(qt_ref, kc_ref, vt_ref, o_ref, m_ref, l_ref, acc_ref, s_ref):
    qi = pl.program_id(2)
    q = qt_ref[0, 0]
    m_ref[...] = jnp.full((1, TQ), NEG, f32)
    l_ref[...] = jnp.zeros((1, TQ), f32)
    acc_ref[...] = jnp.zeros((DV, TQ), f32)

    def scores(j, slot):
        k0 = pl.multiple_of(j * TK, TK)
        k = kc_ref[0, 0, pl.ds(k0, TK), :]
        s_ref[slot] = jnp.dot(k, q, preferred_element_type=f32)

    def softmax_pv(j, slot, masked):
        k0 = pl.multiple_of(j * TK, TK)
        s = s_ref[slot]
        if masked:
            kpos = k0 + lax.broadcasted_iota(i32, (TK, TQ), 0)
            qpos = qi * TQ + lax.broadcasted_iota(i32, (TK, TQ), 1)
            s = jnp.where(kpos <= qpos, s, NEG)
        m_old = m_ref[...]
        m_new = jnp.maximum(m_old, jnp.max(s, axis=0, keepdims=True))
        alpha = jnp.exp2(m_old - m_new)
        p = jnp.exp2(s - m_new)
        l_ref[...] = alpha * l_ref[...] + jnp.sum(p, axis=0, keepdims=True)
        v = vt_ref[0, 0, :, pl.ds(k0, TK)]
        acc_ref[...] = alpha * acc_ref[...] + jnp.dot(
            v, p.astype(bf16), preferred_element_type=f32)
        m_ref[...] = m_new

    half = lax.shift_right_logical(qi, 1)
    scores(0, 0)

    def body(jj, carry):
        scores(2 * jj + 1, 1)
        softmax_pv(2 * jj, 0, False)
        scores(2 * jj + 2, 0)
        softmax_pv(2 * jj + 1, 1, False)
        return carry

    lax.fori_loop(0, half, body, 0)

    @pl.when((qi & 1) == 0)
    def _():
        softmax_pv(qi, 0, True)

    @pl.when((qi & 1) == 1)
    def _():
        scores(qi, 1)
        softmax_pv(qi - 1, 0, False)
        softmax_pv(qi, 1, True)

    o = acc_ref[...] / l_ref[...]
    o_ref[0] = o.T.astype(bf16)


def _attention(qt, kc, vt):
    return pl.pallas_call(
        _attn_kernel,
        grid=(B, H, S // TQ),
        in_specs=[
            pl.BlockSpec((1, 1, DQK, TQ), lambda b, h, q: (b, h, 0, q)),
            pl.BlockSpec((1, 1, S, DQK), lambda b, h, q: (b, h, 0, 0)),
            pl.BlockSpec((1, 1, DV, S), lambda b, h, q: (b, h, 0, 0)),
        ],
        out_specs=pl.BlockSpec((1, TQ, DV), lambda b, h, q: (b, q, h)),
        out_shape=jax.ShapeDtypeStruct((B, S, H * DV), bf16),
        scratch_shapes=[pltpu.VMEM((1, TQ), f32), pltpu.VMEM((1, TQ), f32),
                        pltpu.VMEM((DV, TQ), f32), pltpu.VMEM((2, TK, TQ), f32)],
        compiler_params=_cp(3),
        name="attn",
    )(qt, kc, vt)


def _postattn_kernel(x_ref, o_ref, mod_ref, wo_ref, n2g_ref, rw_ref, rb_ref,
                     xo_ref, h2p_ref, eidx_ref, ewt_ref):
    g1 = mod_ref[0, 2:3, :]
    mix = jnp.dot(o_ref[0], wo_ref[...], preferred_element_type=f32)
    x_new = x_ref[0] + g1 * mix
    xo_ref[0] = x_new
    _route_and_pack(x_new, mod_ref, n2g_ref, rw_ref, rb_ref, h2p_ref, eidx_ref, ewt_ref)


def _postattn(x, o, mod_l, wo, n2g, rw, rb):
    out_specs, out_shape = _tok_specs()
    return pl.pallas_call(
        _postattn_kernel,
        grid=(B, NT),
        in_specs=[
            pl.BlockSpec((1, TM, D), lambda b, s: (b, s, 0)),
            pl.BlockSpec((1, TM, H * DV), lambda b, s: (b, s, 0)),
            pl.BlockSpec((1, 6, D), lambda b, s: (b, 0, 0)),
            _full((H * DV, D)), _full((1, D)), _full((2 * E, D)), _full((E, 1)),
        ],
        out_specs=out_specs,
        out_shape=out_shape,
        compiler_params=_cp(2),
        name="postattn",
    )(x, o, mod_l, wo, n2g, rw, rb)


def _row_copy(src_ref, dst_ref, sem, src_row, dst_row):
    return pltpu.make_async_copy(src_ref.at[src_row], dst_ref.at[dst_row], sem)


def _gather_kernel(idx_ref, src_ref, dst_ref, sem):
    def issue(g, carry):
        r0 = g * DU
        for u in range(DU):
            _row_copy(src_ref, dst_ref, sem, idx_ref[r0 + u], r0 + u).start(priority=u % 2)
        return carry

    lax.fori_loop(0, GR // DU, issue, 0)

    def drain(r, carry):
        _row_copy(src_ref, dst_ref, sem, 0, r).wait()
        return carry

    lax.fori_loop(0, GR, drain, 0, unroll=8)


def _gather_rows(src, idx):
    n = idx.shape[0]
    return pl.pallas_call(
        _gather_kernel,
        grid=(n // GR,),
        in_specs=[
            pl.BlockSpec((GR,), lambda i: (i,), memory_space=pltpu.SMEM),
            pl.BlockSpec(memory_space=pl.ANY),
        ],
        out_specs=pl.BlockSpec((GR, PG, 128), lambda i: (i, 0, 0)),
        out_shape=jax.ShapeDtypeStruct((n, PG, 128), src.dtype),
        scratch_shapes=[pltpu.SemaphoreType.DMA],
        compiler_params=_cp(1),
        name="gather_rows",
    )(idx, src)


def _expert_kernel(be_ref, first_ref, nv_ref, xb_ref, wg_ref, wu_ref, wd_ref, yb_ref,
                   wgu_bf, wd_bf):
    i = pl.program_id(0)
    nv = nv_ref[i]

    @pl.when(first_ref[i] == 1)
    def _():
        wgu_bf[:, :DE] = wg_ref[0, 0].astype(bf16)
        wgu_bf[:, DE:] = wu_ref[0, 0].astype(bf16)
        wd_bf[...] = wd_ref[0, 0].astype(bf16)

    @pl.when(nv > 0)
    def _():
        x = _unpack_rows(_load_rows(xb_ref, RB)).astype(bf16)
        gu = jnp.dot(x, wgu_bf[...], preferred_element_type=f32)
        g = gu[:, :DE]
        hmid = (g * _sigmoid(g) * gu[:, DE:]).astype(bf16)
        y = jnp.dot(hmid, wd_bf[...], preferred_element_type=f32)
        _store_rows(yb_ref, _pack_pair(y[:, :HP], y[:, HP:]), RB)

    @pl.when(nv == 0)
    def _():
        yb_ref[...] = jnp.zeros((RB * PG, 128), u32)


def _experts(layer, blk_e, blk_first, blk_nv, xb, wg, wu, wd):
    grid_spec = pltpu.PrefetchScalarGridSpec(
        num_scalar_prefetch=3,
        grid=(NB,),
        in_specs=[
            pl.BlockSpec((RB * PG, 128), lambda i, be, bf, nv: (i, 0)),
            pl.BlockSpec((1, 1, D, DE), lambda i, be, bf, nv: (layer, be[i], 0, 0)),
            pl.BlockSpec((1, 1, D, DE), lambda i, be, bf, nv: (layer, be[i], 0, 0)),
            pl.BlockSpec((1, 1, DE, D), lambda i, be, bf, nv: (layer, be[i], 0, 0)),
        ],
        out_specs=pl.BlockSpec((RB * PG, 128), lambda i, be, bf, nv: (i, 0)),
        scratch_shapes=[pltpu.VMEM((D, 2 * DE), bf16), pltpu.VMEM((DE, D), bf16)],
    )
    return pl.pallas_call(
        _expert_kernel,
        grid_spec=grid_spec,
        out_shape=jax.ShapeDtypeStruct((N_ROWS * PG, 128), u32),
        compiler_params=_cp(1),
        name="experts",
    )(blk_e, blk_first, blk_nv, xb, wg, wu, wd)


PCH = 256
NBP = 256


def _route_pos_kernel(eidx_ref, dest_ref, blk_ref, pre_ref):
    upper = (lax.broadcasted_iota(i32, (PCH, PCH), 0)
             < lax.broadcasted_iota(i32, (PCH, PCH), 1)).astype(bf16)
    eio = lax.broadcasted_iota(i32, (E, PCH), 0)

    def onehot(c0, width, eiota):
        e0 = eidx_ref[0:1, pl.ds(c0, width)]
        e1 = eidx_ref[1:2, pl.ds(c0, width)]
        return jnp.concatenate([e0 == eiota, e1 == eiota], axis=0)

    ones_r = jnp.ones((8, PCH), bf16)

    def count(c, carry):
        col, row = carry
        c0 = pl.multiple_of(c * PCH, PCH)
        oh = onehot(c0, PCH, eio).astype(bf16)
        pre_ref[:, pl.ds(c0, PCH)] = col + jnp.dot(oh, upper, preferred_element_type=f32)
        col = col + jnp.sum(oh.astype(f32), axis=1, keepdims=True)
        row = row + _dot_nt(ones_r, oh[:E]) + _dot_nt(ones_r, oh[E:])
        return col, row

    tot_c, tot_r = lax.fori_loop(
        0, T // PCH, count, (jnp.zeros((2 * E, 1), f32), jnp.zeros((8, E), f32)))
    cnt0_c = tot_c[:E]
    cnt_c = tot_c[:E] + tot_c[E:]
    shift = RB.bit_length() - 1
    assert RB == 1 << shift
    padded_r = lax.shift_left(
        lax.shift_right_logical(tot_r[0:1].astype(i32) + (RB - 1), shift), shift).astype(f32)
    ee = lax.broadcasted_iota(i32, (E, E), 0)
    ep = lax.broadcasted_iota(i32, (E, E), 1)
    start_col = jnp.sum(jnp.where(ep < ee, padded_r, 0.0), axis=1, keepdims=True)
    end_col = jnp.sum(jnp.where(ep <= ee, padded_r, 0.0), axis=1, keepdims=True)
    vend_col = start_col + cnt_c
    basef = jnp.concatenate([start_col, start_col + cnt0_c], axis=0)

    wide = 2048
    eiow = lax.broadcasted_iota(i32, (E, wide), 0)

    def place(c, carry):
        c0 = pl.multiple_of(c * wide, wide)
        oh = onehot(c0, wide, eiow)
        val = jnp.where(oh, pre_ref[:, pl.ds(c0, wide)] + basef, 0.0)
        d0 = jnp.sum(val[:E], axis=0, keepdims=True)
        d1 = jnp.sum(val[E:], axis=0, keepdims=True)
        dest_ref[0:1, pl.ds(c0, wide)] = d0.astype(i32)
        dest_ref[1:2, pl.ds(c0, wide)] = d1.astype(i32)
        return carry

    lax.fori_loop(0, T // wide, place, 0)

    start_b = (lax.broadcasted_iota(i32, (1, NBP), 1) * RB).astype(f32)
    be = jnp.minimum(jnp.sum((end_col <= start_b).astype(f32), axis=0, keepdims=True), E - 1.0)
    bprev = jnp.minimum(
        jnp.sum((end_col <= start_b - RB).astype(f32), axis=0, keepdims=True), E - 1.0)
    first = jnp.logical_or(start_b == 0.0, be != bprev)
    eion = lax.broadcasted_iota(i32, (E, NBP), 0).astype(f32)
    vend_b = jnp.sum(jnp.where(eion == be, vend_col, 0.0), axis=0, keepdims=True)
    nv = jnp.clip(vend_b - start_b, 0.0, float(RB))
    blk_ref[...] = jnp.zeros((8, NBP), i32)
    blk_ref[0:1, :] = be.astype(i32)
    blk_ref[1:2, :] = first.astype(i32)
    blk_ref[2:3, :] = nv.astype(i32)


def _route_pos(eidx):
    return pl.pallas_call(
        _route_pos_kernel,
        in_specs=[pl.BlockSpec(memory_space=pltpu.VMEM)],
        out_specs=[pl.BlockSpec(memory_space=pltpu.VMEM)] * 2,
        out_shape=[jax.ShapeDtypeStruct((TOPK, T), i32), jax.ShapeDtypeStruct((8, NBP), i32)],
        scratch_shapes=[pltpu.VMEM((2 * E, T), f32)],
        compiler_params=pltpu.CompilerParams(vmem_limit_bytes=VMEM_LIMIT),
        name="route_pos",
    )(eidx)


TD = 1024
ZR = 1024


def _dispatch_kernel(d0_ref, d1_ref, h2p_ref, xb_ref, zbuf, sem):
    @pl.when(pl.program_id(0) == 0)
    def _():
        zbuf[...] = jnp.zeros((ZR, PG, 128), u32)
        for j in range(N_ROWS // ZR):
            pltpu.make_async_copy(zbuf, xb_ref.at[pl.ds(j * ZR, ZR)], sem).start()
        for j in range(N_ROWS // ZR):
            pltpu.make_async_copy(zbuf, xb_ref.at[pl.ds(j * ZR, ZR)], sem).wait()

    def issue(g, carry):
        r0 = g * DU
        for u in range(DU):
            _row_copy(h2p_ref, xb_ref, sem, r0 + u, d0_ref[r0 + u]).start(priority=0)
            _row_copy(h2p_ref, xb_ref, sem, r0 + u, d1_ref[r0 + u]).start(priority=1)
        return carry

    lax.fori_loop(0, TD // DU, issue, 0)

    def drain(r, carry):
        _row_copy(h2p_ref, xb_ref, sem, 0, 0).wait()
        _row_copy(h2p_ref, xb_ref, sem, 0, 0).wait()
        return carry

    lax.fori_loop(0, TD, drain, 0, unroll=8)


def _dispatch(h2p, d0, d1):
    return pl.pallas_call(
        _dispatch_kernel,
        grid=(T // TD,),
        in_specs=[
            pl.BlockSpec((TD,), lambda i: (i,), memory_space=pltpu.SMEM),
            pl.BlockSpec((TD,), lambda i: (i,), memory_space=pltpu.SMEM),
            pl.BlockSpec((TD, PG, 128), lambda i: (i, 0, 0)),
        ],
        out_specs=pl.BlockSpec(memory_space=pl.ANY),
        out_shape=jax.ShapeDtypeStruct((N_ROWS, PG, 128), u32),
        scratch_shapes=[pltpu.VMEM((ZR, PG, 128), u32), pltpu.SemaphoreType.DMA],
        compiler_params=_cp(1),
        name="dispatch",
    )(d0, d1, h2p)


def _final_kernel(x_ref, yg0_ref, yg1_ref, ewp_ref, modp_ref, g_ref, o_ref):
    o_ref[0] = _rms(_moe_combine(x_ref, yg0_ref, yg1_ref, ewp_ref, modp_ref), g_ref[...])


def _final(x, moe_in, g):
    return pl.pallas_call(
        _final_kernel,
        grid=(B, NT),
        in_specs=[pl.BlockSpec((1, TM, D), lambda b, s: (b, s, 0))] + _moe_in_specs()
        + [_full((1, D))],
        out_specs=pl.BlockSpec((1, TM, D), lambda b, s: (b, s, 0)),
        out_shape=jax.ShapeDtypeStruct((B, S, D), f32),
        compiler_params=_cp(2),
        name="final_norm",
    )(x, *moe_in, g)


def _moe(layer, h2p, eidx, wg, wu, wd):
    dest, blk = _route_pos(eidx)
    xb = _dispatch(h2p.reshape(T, PG, 128), dest[0], dest[1])
    yb = _experts(layer, blk[0, :NB], blk[1, :NB], blk[2, :NB], xb.reshape(N_ROWS * PG, 128),
                  wg, wu, wd)
    yg = _gather_rows(yb.reshape(N_ROWS, PG, 128), dest.reshape(-1))
    return yg.reshape(TOPK * T * PG, 128)


def kernel(x, c, positions, ada_w, ada_b, norm1_g, norm2_g, conv_w_in, conv_b_in, conv_w_dw,
           conv_b_dw, conv_ln_g, conv_ln_b, conv_w_out, kv_in_g, w_dkv, kv_norm_g, w_ukv, w_dq,
           q_norm_g, w_uq, w_o, router_w, router_b, exp_w_gate, exp_w_up, exp_w_down, final_g):
    c8 = jnp.pad(c, ((0, 8 - B), (0, 0)))
    mod = _ada(c8, ada_w, ada_b[:, None, :])[:, :B].reshape(DEPTH, B, 6, D)

    invf = (ROPE_THETA ** (-jnp.arange(0, DR, 2, dtype=f32) / DR))[:, None]
    cos_t, sin_t = _rope_tables(positions[:, None, :], invf)

    rwt = router_w.astype(f32).T
    rw_hi = rwt.astype(bf16)
    rw_lo = (rwt - rw_hi.astype(f32)).astype(bf16)
    rw = jnp.concatenate([rw_hi, rw_lo], axis=0)
    rb = router_b.astype(f32)[:, None]

    w_ukv3 = w_ukv.reshape(KVR, H, DN + DV)
    wuk = w_ukv3[:, :, :DN].reshape(KVR, H * DN).astype(bf16)
    wuvt = w_ukv3[:, :, DN:].reshape(KVR, H * DV).T.astype(bf16)
    kv_w = (kv_in_g[None, :], w_dkv[:, :KVR].astype(bf16), w_dkv[:, KVR:].T.astype(bf16),
            kv_norm_g[None, :], wuk, wuvt)

    moe_in = None
    kc = vt = None
    for l in range(DEPTH):
        if l < N_A:
            wdw = jnp.pad(conv_w_dw[l], ((0, HALO - CONV_W), (0, 0))).reshape(HALO, CG, 128)
            x, h2p, eidx, ew = _conv_layer(
                x, moe_in, mod[l], norm1_g[l][None, :], conv_w_in[l].astype(bf16),
                conv_b_in[l][None, :], wdw, conv_b_dw[l].reshape(CG, 128), conv_ln_g[l][None, :],
                conv_ln_b[l][None, :], conv_w_out[l].astype(bf16), norm2_g[l][None, :], rw, rb)
        else:
            j = l - N_A
            outs = _preattn(
                x, moe_in, mod[l], norm1_g[l][None, :], w_dq[j].astype(bf16),
                q_norm_g[j][None, :], w_uq[j].T.astype(bf16), cos_t, sin_t,
                kv=kv_w if j == 0 else None)
            if j == 0:
                x, qt, kc, vt = outs
            else:
                x, qt = outs
            o = _attention(qt, kc, vt)
            x, h2p, eidx, ew = _postattn(
                x, o, mod[l], w_o[j].astype(bf16), norm2_g[l][None, :], rw, rb)
        yg = _moe(l, h2p, eidx, exp_w_gate, exp_w_up, exp_w_down)
        moe_in = (yg, yg, ew, mod[l])
    return _final(x, moe_in, final_g[None, :])
```

```python
import functools
import math

import jax
import jax.numpy as jnp
from jax import lax
from jax.experimental import pallas as pl
from jax.experimental.pallas import tpu as pltpu

f32 = jnp.float32
bf16 = jnp.bfloat16
u32 = jnp.uint32
i32 = jnp.int32

D = 1024
B = 4
S = 4096
T = B * S
DEPTH = 4
N_A = 2
CONV_W = 31
C = 1024
H = 8
DN = 128
DR = 64
DV = 128
DQK = DN + DR
QR = 384
KVR = 256
ROPE_THETA = 10000.0
ATTN_SCALE = 1.0 / math.sqrt(DN + DR)
QSCALE = ATTN_SCALE * math.log2(math.e)
E = 16
NG = 4
EG = 4
TOPK = 2
DE = 512
EPS = 1e-6
NEG = -1e30

TM = 512
NT = S // TM
HALO = 32
CG = C // 128
OB = 8
TQ = 512
TK = 512
RB = 512
N_ROWS = T * TOPK + E * RB
NB = N_ROWS // RB
GR = 2048
DU = 8
HP = D // 2
PG = HP // 128

VMEM_LIMIT = 56 * 1024 * 1024


def _cp(n_axes):
    return pltpu.CompilerParams(
        dimension_semantics=("arbitrary",) * n_axes, vmem_limit_bytes=VMEM_LIMIT)


def _full(shape):
    n = len(shape)
    return pl.BlockSpec(shape, lambda *_: (0,) * n)


def _rms(x, g):
    return x * lax.rsqrt(jnp.mean(x * x, axis=-1, keepdims=True) + EPS) * g


def _sigmoid(x):
    return 0.5 * jnp.tanh(0.5 * x) + 0.5


def _pack_pair(a, b):
    ua = lax.bitcast_convert_type(a.astype(bf16).astype(f32), u32)
    ub = lax.bitcast_convert_type(b.astype(bf16).astype(f32), u32)
    return (ua >> 16) | ub


def _unpack_pair(w):
    a = lax.bitcast_convert_type(w << 16, f32)
    b = lax.bitcast_convert_type(w & jnp.uint32(0xFFFF0000), f32)
    return a, b


def _unpack_rows(w):
    a, b = _unpack_pair(w)
    return jnp.concatenate([a, b], axis=1)


def _load_rows(ref, rows):
    return jnp.concatenate([ref[pl.ds(k, rows, stride=PG), :] for k in range(PG)], axis=1)


def _store_rows(ref, w, rows):
    for k in range(PG):
        ref[pl.ds(k, rows, stride=PG), :] = w[:, 128 * k:128 * (k + 1)]


def _dot_nt(a, b):
    return lax.dot_general(a, b, (((1,), (1,)), ((), ())), preferred_element_type=f32)


def _top2sum4(a, b, c, d):
    lo1, hi1 = jnp.minimum(a, b), jnp.maximum(a, b)
    lo2, hi2 = jnp.minimum(c, d), jnp.maximum(c, d)
    return jnp.maximum(hi1, hi2) + jnp.maximum(jnp.minimum(hi1, hi2), jnp.maximum(lo1, lo2))


def _route_and_pack(x_new, mod_ref, n2g_ref, rw_ref, rb_ref, h2p_ref, eidx_ref, ewt_ref):
    sh2 = mod_ref[0, 3:4, :]
    sc2 = mod_ref[0, 4:5, :]
    h2 = _rms(x_new, n2g_ref[...]) * (1.0 + sc2) + sh2
    _store_rows(h2p_ref, _pack_pair(h2[:, :HP], h2[:, HP:]), TM)

    hi = h2.astype(bf16)
    lo = (h2 - hi.astype(f32)).astype(bf16)
    rw = rw_ref[...]
    p = _dot_nt(rw, hi) + _dot_nt(rw, lo)
    logits = p[:E] + p[E:]
    s = jax.nn.sigmoid(logits)
    sb = s + rb_ref[...]

    srow = [s[e:e + 1] for e in range(E)]
    brow = [sb[e:e + 1] for e in range(E)]
    gs = [_top2sum4(*brow[EG * g:EG * g + EG]) for g in range(NG)]
    gidx = jnp.zeros_like(gs[0], dtype=i32)
    best = gs[0]
    for g in range(1, NG):
        upd = gs[g] > best
        gidx = jnp.where(upd, g, gidx)
        best = jnp.where(upd, gs[g], best)

    def pick(rows, i):
        out = rows[i]
        for g in range(1, NG):
            out = jnp.where(gidx == g, rows[EG * g + i], out)
        return out

    wv = [pick(brow, i) for i in range(EG)]
    sv = [pick(srow, i) for i in range(EG)]

    l1 = jnp.zeros_like(gidx)
    b1 = wv[0]
    for i in range(1, EG):
        upd = wv[i] > b1
        l1 = jnp.where(upd, i, l1)
        b1 = jnp.where(upd, wv[i], b1)
    l2 = jnp.zeros_like(gidx)
    b2 = jnp.where(l1 == 0, -jnp.inf, wv[0])
    for i in range(1, EG):
        cand = jnp.where(l1 == i, -jnp.inf, wv[i])
        upd = cand > b2
        l2 = jnp.where(upd, i, l2)
        b2 = jnp.where(upd, cand, b2)

    def sel(loc):
        out = sv[0]
        for i in range(1, EG):
            out = jnp.where(loc == i, sv[i], out)
        return out

    w1 = sel(l1)
    w2 = sel(l2)
    tot = w1 + w2
    eidx_ref[...] = jnp.concatenate([gidx * EG + l1, gidx * EG + l2], axis=0)
    wt = jnp.concatenate([w1 / tot, w2 / tot, jnp.zeros((128 - TOPK, TM), f32)], axis=0)
    ewt_ref[...] = wt.T


def _moe_combine(x_ref, yg0_ref, yg1_ref, ewt_ref, modp_ref):
    y0 = _unpack_rows(_load_rows(yg0_ref, TM))
    y1 = _unpack_rows(_load_rows(yg1_ref, TM))
    w = ewt_ref[...]
    g2 = modp_ref[0, 5:6, :]
    return x_ref[0] + g2 * (w[:, 0:1] * y0 + w[:, 1:2] * y1)


def _ada_kernel(c_ref, w_ref, b_ref, o_ref):
    c = c_ref[...]
    ca = (c * jax.nn.sigmoid(c)).astype(bf16)
    o_ref[0] = jnp.dot(ca, w_ref[0].astype(bf16), preferred_element_type=f32) + b_ref[0]


def _ada(c8, ada_w, ada_b3):
    tn = 1536
    return pl.pallas_call(
        _ada_kernel,
        grid=(DEPTH, 6 * D // tn),
        in_specs=[
            pl.BlockSpec((8, D), lambda l, n: (0, 0)),
            pl.BlockSpec((1, D, tn), lambda l, n: (l, 0, n)),
            pl.BlockSpec((1, 1, tn), lambda l, n: (l, 0, n)),
        ],
        out_specs=pl.BlockSpec((1, 8, tn), lambda l, n: (l, 0, n)),
        out_shape=jax.ShapeDtypeStruct((DEPTH, 8, 6 * D), f32),
        compiler_params=_cp(2),
        name="ada",
    )(c8, ada_w, ada_b3)


def _rope_kernel(pos_ref, invf_ref, cos_ref, sin_ref):
    ang = pos_ref[0].astype(f32) * invf_ref[...]
    cos_ref[0] = jnp.cos(ang)
    sin_ref[0] = jnp.sin(ang)


def _rope_tables(pos3, invf):
    return pl.pallas_call(
        _rope_kernel,
        grid=(B,),
        in_specs=[pl.BlockSpec((1, 1, S), lambda b: (b, 0, 0)), _full((DR // 2, 1))],
        out_specs=[pl.BlockSpec((1, DR // 2, S), lambda b: (b, 0, 0))] * 2,
        out_shape=[jax.ShapeDtypeStruct((B, DR // 2, S), f32)] * 2,
        compiler_params=_cp(1),
        name="rope",
    )(pos3, invf)


def _conv_kernel(has_moe_in, *refs):
    if has_moe_in:
        x_ref, yg0_ref, yg1_ref, ewp_ref, modp_ref = refs[:5]
        refs = refs[5:]
    else:
        x_ref = refs[0]
        refs = refs[1:]
    (mod_ref, n1g_ref, win_ref, bin_ref, wdw_ref, bdw_ref, lng_ref, lnb_ref, wout_ref,
     n2g_ref, rw_ref, rb_ref,
     xo_ref, h2p_ref, eidx_ref, ewt_ref, ubuf, obuf) = refs
    si = pl.program_id(1)

    x = _moe_combine(x_ref, yg0_ref, yg1_ref, ewp_ref, modp_ref) if has_moe_in else x_ref[0]
    sh1 = mod_ref[0, 0:1, :]
    sc1 = mod_ref[0, 1:2, :]
    g1 = mod_ref[0, 2:3, :]
    h = (_rms(x, n1g_ref[...]) * (1.0 + sc1) + sh1).astype(bf16)
    u = jnp.dot(h, win_ref[...], preferred_element_type=f32) + bin_ref[...]
    glu = u[:, :C] * _sigmoid(u[:, C:])

    @pl.when(si == 0)
    def _():
        ubuf[0:HALO * CG, :] = jnp.zeros((HALO * CG, 128), f32)

    @pl.when(si > 0)
    def _():
        ubuf[0:HALO * CG, :] = ubuf[TM * CG:(TM + HALO) * CG, :]

    for k in range(CG):
        ubuf[pl.ds(HALO * CG + k, TM, stride=CG), :] = glu[:, 128 * k:128 * (k + 1)]

    off = HALO - (CONV_W - 1)
    taps = [wdw_ref[j] for j in range(CONV_W)]
    bias = bdw_ref[...]

    def block(bi, carry):
        t0 = bi * OB
        acc = [bias] * OB
        for dd in range(OB + CONV_W - 1):
            row = pl.multiple_of((t0 + off + dd) * CG, CG)
            uv = ubuf[pl.ds(row, CG), :]
            for o in range(OB):
                j = dd - o
                if 0 <= j < CONV_W:
                    acc[o] = acc[o] + uv * taps[j]
        for o in range(OB):
            obuf[pl.ds(pl.multiple_of((t0 + o) * CG, CG), CG), :] = acc[o]
        return carry

    lax.fori_loop(0, TM // OB, block, 0)

    v = jnp.concatenate([obuf[pl.ds(k, TM, stride=CG), :] for k in range(CG)], axis=1)
    mu = jnp.mean(v, axis=-1, keepdims=True)
    vc = v - mu
    var = jnp.mean(vc * vc, axis=-1, keepdims=True)
    y = vc * lax.rsqrt(var + EPS) * lng_ref[...] + lnb_ref[...]
    y = (y * _sigmoid(y)).astype(bf16)
    mix = jnp.dot(y, wout_ref[...], preferred_element_type=f32)
    x_new = x + g1 * mix
    xo_ref[0] = x_new
    _route_and_pack(x_new, mod_ref, n2g_ref, rw_ref, rb_ref, h2p_ref, eidx_ref, ewt_ref)


def _tok_specs():
    specs = [
        pl.BlockSpec((1, TM, D), lambda b, s: (b, s, 0)),
        pl.BlockSpec((TM * PG, 128), lambda b, s: (b * NT + s, 0)),
        pl.BlockSpec((TOPK, TM), lambda b, s: (0, b * NT + s)),
        pl.BlockSpec((TM, 128), lambda b, s: (b * NT + s, 0)),
    ]
    shapes = [
        jax.ShapeDtypeStruct((B, S, D), f32),
        jax.ShapeDtypeStruct((T * PG, 128), u32),
        jax.ShapeDtypeStruct((TOPK, T), i32),
        jax.ShapeDtypeStruct((T, 128), f32),
    ]
    return specs, shapes


def _moe_in_specs():
    return [
        pl.BlockSpec((TM * PG, 128), lambda b, s: (b * NT + s, 0)),
        pl.BlockSpec((TM * PG, 128), lambda b, s: (T // TM + b * NT + s, 0)),
        pl.BlockSpec((TM, 128), lambda b, s: (b * NT + s, 0)),
        pl.BlockSpec((1, 6, D), lambda b, s: (b, 0, 0)),
    ]


def _conv_layer(x, moe_in, mod_l, n1g, win, b_in, wdw, bdw, lng, lnb, wout, n2g, rw, rb):
    has_moe_in = moe_in is not None
    in_specs = [pl.BlockSpec((1, TM, D), lambda b, s: (b, s, 0))]
    args = [x]
    if has_moe_in:
        in_specs += _moe_in_specs()
        args += list(moe_in)
    in_specs += [
        pl.BlockSpec((1, 6, D), lambda b, s: (b, 0, 0)),
        _full((1, D)), _full((D, 2 * C)), _full((1, 2 * C)), _full((HALO, CG, 128)),
        _full((CG, 128)),
        _full((1, C)), _full((1, C)), _full((C, D)), _full((1, D)), _full((2 * E, D)),
        _full((E, 1)),
    ]
    args += [mod_l, n1g, win, b_in, wdw, bdw, lng, lnb, wout, n2g, rw, rb]
    out_specs, out_shape = _tok_specs()
    return pl.pallas_call(
        functools.partial(_conv_kernel, has_moe_in),
        grid=(B, NT),
        in_specs=in_specs,
        out_specs=out_specs,
        out_shape=out_shape,
        scratch_shapes=[pltpu.VMEM(((TM + HALO) * CG, 128), f32),
                        pltpu.VMEM((TM * CG, 128), f32)],
        compiler_params=_cp(2),
        name="conv_layer",
    )(*args)


def _preattn_kernel(with_kv, *refs):
    (x_ref, yg0_ref, yg1_ref, ewp_ref, modp_ref, mod_ref, n1g_ref, wdq_ref, qng_ref, wuqt_ref,
     cos_ref, sin_ref) = refs[:12]
    refs = refs[12:]
    if with_kv:
        kvg_ref, wlat_ref, wropet_ref, kvng_ref, wuk_ref, wuvt_ref = refs[:6]
        refs = refs[6:]
        xo_ref, qt_ref, kc_ref, vt_ref = refs
    else:
        xo_ref, qt_ref = refs

    x = _moe_combine(x_ref, yg0_ref, yg1_ref, ewp_ref, modp_ref)
    xo_ref[0] = x
    cos = cos_ref[0]
    sin = sin_ref[0]
    sh1 = mod_ref[0, 0:1, :]
    sc1 = mod_ref[0, 1:2, :]
    h = (_rms(x, n1g_ref[...]) * (1.0 + sc1) + sh1).astype(bf16)
    cq = jnp.dot(h, wdq_ref[...], preferred_element_type=f32)
    cq = _rms(cq, qng_ref[...]).astype(bf16)
    qt = _dot_nt(wuqt_ref[...], cq)
    hr = DR // 2
    for hh in range(H):
        r0 = hh * DQK
        x1 = qt[r0 + DN:r0 + DN + hr]
        x2 = qt[r0 + DN + hr:r0 + DQK]
        qt_ref[0, hh, 0:DN, :] = (qt[r0:r0 + DN] * QSCALE).astype(bf16)
        qt_ref[0, hh, DN:DN + hr, :] = ((x1 * cos - x2 * sin) * QSCALE).astype(bf16)
        qt_ref[0, hh, DN + hr:DQK, :] = ((x1 * sin + x2 * cos) * QSCALE).astype(bf16)

    if with_kv:
        xn = _rms(x, kvg_ref[...]).astype(bf16)
        lat = jnp.dot(xn, wlat_ref[...], preferred_element_type=f32)
        latn = _rms(lat, kvng_ref[...]).astype(bf16)
        krt = _dot_nt(wropet_ref[...], xn)
        k1 = krt[:hr]
        k2 = krt[hr:]
        kr = jnp.concatenate([k1 * cos - k2 * sin, k1 * sin + k2 * cos], axis=0)
        kr = kr.T.astype(bf16)
        kn = jnp.dot(latn, wuk_ref[...], preferred_element_type=f32).astype(bf16)
        vt = _dot_nt(wuvt_ref[...], latn).astype(bf16)
        for hh in range(H):
            kc_ref[0, hh, :, 0:DN] = kn[:, hh * DN:(hh + 1) * DN]
            kc_ref[0, hh, :, DN:DQK] = kr
            vt_ref[0, hh] = vt[hh * DV:(hh + 1) * DV]


def _preattn(x, moe_in, mod_l, n1g, wdq, qng, wuqt, cos_t, sin_t, kv=None):
    with_kv = kv is not None
    in_specs = [pl.BlockSpec((1, TM, D), lambda b, s: (b, s, 0))] + _moe_in_specs() + [
        pl.BlockSpec((1, 6, D), lambda b, s: (b, 0, 0)),
        _full((1, D)), _full((D, QR)), _full((1, QR)), _full((H * DQK, QR)),
        pl.BlockSpec((1, DR // 2, TM), lambda b, s: (b, 0, s)),
        pl.BlockSpec((1, DR // 2, TM), lambda b, s: (b, 0, s)),
    ]
    args = [x, *moe_in, mod_l, n1g, wdq, qng, wuqt, cos_t, sin_t]
    out_specs = [
        pl.BlockSpec((1, TM, D), lambda b, s: (b, s, 0)),
        pl.BlockSpec((1, H, DQK, TM), lambda b, s: (b, 0, 0, s)),
    ]
    out_shape = [
        jax.ShapeDtypeStruct((B, S, D), f32),
        jax.ShapeDtypeStruct((B, H, DQK, S), bf16),
    ]
    if with_kv:
        in_specs += [_full((1, D)), _full((D, KVR)), _full((DR, D)), _full((1, KVR)),
                     _full((KVR, H * DN)), _full((H * DV, KVR))]
        args += list(kv)
        out_specs += [
            pl.BlockSpec((1, H, TM, DQK), lambda b, s: (b, 0, s, 0)),
            pl.BlockSpec((1, H, DV, TM), lambda b, s: (b, 0, 0, s)),
        ]
        out_shape += [
            jax.ShapeDtypeStruct((B, H, S, DQK), bf16),
            jax.ShapeDtypeStruct((B, H, DV, S), bf16),
        ]
    return pl.pallas_call(
        functools.partial(_preattn_kernel, with_kv),
        grid=(B, NT),
        in_specs=in_specs,
        out_specs=out_specs,
        out_shape=out_shape,
        compiler_params=_cp(2),
        name="preattn_kv" if with_kv else "preattn",
    )(*args)


def _attn_kernel(qt_ref, kc_ref, vt_ref, o_ref, m_ref, l_ref, acc_ref, s_ref):
    assert TQ == TK
    pairs = [(qi, j) for qi in range(S // TQ) for j in range(qi + 1)]

    def scores(qi, j, slot):
        q = qt_ref[0, 0, :, qi * TQ:(qi + 1) * TQ]
        k = kc_ref[0, 0, j * TK:(j + 1) * TK, :]
        s_ref[slot] = jnp.dot(k, q, preferred_element_type=f32)

    def softmax_pv(qi, j, slot):
        s = s_ref[slot]
        if j == qi:
            keep = (lax.broadcasted_iota(i32, (TK, TQ), 0)
                    <= lax.broadcasted_iota(i32, (TK, TQ), 1))
            s = jnp.where(keep, s, NEG)
        v = vt_ref[0, 0, :, j * TK:(j + 1) * TK]
        if j == 0:
            m_new = jnp.max(s, axis=0, keepdims=True)
            p = jnp.exp2(s - m_new)
            l_ref[...] = jnp.sum(p, axis=0, keepdims=True)
            acc_ref[...] = jnp.dot(v, p.astype(bf16), preferred_element_type=f32)
        else:
            m_old = m_ref[...]
            m_new = jnp.maximum(m_old, jnp.max(s, axis=0, keepdims=True))
            alpha = jnp.exp2(m_old - m_new)
            p = jnp.exp2(s - m_new)
            l_ref[...] = alpha * l_ref[...] + jnp.sum(p, axis=0, keepdims=True)
            acc_ref[...] = alpha * acc_ref[...] + jnp.dot(
                v, p.astype(bf16), preferred_element_type=f32)
        m_ref[...] = m_new
        if j == qi:
            o = acc_ref[...] / l_ref[...]
            o_ref[0, qi * TQ:(qi + 1) * TQ, :] = o.T.astype(bf16)

    scores(*pairs[0], 0)
    for n, (qi, j) in enumerate(pairs):
        if n + 1 < len(pairs):
            scores(*pairs[n + 1], (n + 1) % 2)
        softmax_pv(qi, j, n % 2)


def _attention(qt, kc, vt):
    return pl.pallas_call(
        _attn_kernel,
        grid=(B, H),
        in_specs=[
            pl.BlockSpec((1, 1, DQK, S), lambda b, h: (b, h, 0, 0)),
            pl.BlockSpec((1, 1, S, DQK), lambda b, h: (b, h, 0, 0)),
            pl.BlockSpec((1, 1, DV, S), lambda b, h: (b, h, 0, 0)),
        ],
        out_specs=pl.BlockSpec((1, S, DV), lambda b, h: (b, 0, h)),
        out_shape=jax.ShapeDtypeStruct((B, S, H * DV), bf16),
        scratch_shapes=[pltpu.VMEM((1, TQ), f32), pltpu.VMEM((1, TQ), f32),
                        pltpu.VMEM((DV, TQ), f32), pltpu.VMEM((2, TK, TQ), f32)],
        compiler_params=_cp(2),
        name="attn",
    )(qt, kc, vt)


def _postattn_kernel(x_ref, o_ref, mod_ref, wo_ref, n2g_ref, rw_ref, rb_ref,
                     xo_ref, h2p_ref, eidx_ref, ewt_ref):
    g1 = mod_ref[0, 2:3, :]
    mix = jnp.dot(o_ref[0], wo_ref[...], preferred_element_type=f32)
    x_new = x_ref[0] + g1 * mix
    xo_ref[0] = x_new
    _route_and_pack(x_new, mod_ref, n2g_ref, rw_ref, rb_ref, h2p_ref, eidx_ref, ewt_ref)


def _postattn(x, o, mod_l, wo, n2g, rw, rb):
    out_specs, out_shape = _tok_specs()
    return pl.pallas_call(
        _postattn_kernel,
        grid=(B, NT),
        in_specs=[
            pl.BlockSpec((1, TM, D), lambda b, s: (b, s, 0)),
            pl.BlockSpec((1, TM, H * DV), lambda b, s: (b, s, 0)),
            pl.BlockSpec((1, 6, D), lambda b, s: (b, 0, 0)),
            _full((H * DV, D)), _full((1, D)), _full((2 * E, D)), _full((E, 1)),
        ],
        out_specs=out_specs,
        out_shape=out_shape,
        compiler_params=_cp(2),
        name="postattn",
    )(x, o, mod_l, wo, n2g, rw, rb)


def _row_copy(src_ref, dst_ref, sem, src_row, dst_row):
    return pltpu.make_async_copy(src_ref.at[src_row], dst_ref.at[dst_row], sem)


def _gather_kernel(idx_ref, src_ref, dst_ref, sem):
    def issue(g, carry):
        r0 = g * DU
        for u in range(DU):
            _row_copy(src_ref, dst_ref, sem, idx_ref[r0 + u], r0 + u).start(priority=u % 2)
        return carry

    lax.fori_loop(0, GR // DU, issue, 0)

    def drain(r, carry):
        _row_copy(src_ref, dst_ref, sem, 0, r).wait()
        return carry

    lax.fori_loop(0, GR, drain, 0, unroll=8)


def _gather_rows(src, idx):
    n = idx.shape[0]
    return pl.pallas_call(
        _gather_kernel,
        grid=(n // GR,),
        in_specs=[
            pl.BlockSpec((GR,), lambda i: (i,), memory_space=pltpu.SMEM),
            pl.BlockSpec(memory_space=pl.ANY),
        ],
        out_specs=pl.BlockSpec((GR, PG, 128), lambda i: (i, 0, 0)),
        out_shape=jax.ShapeDtypeStruct((n, PG, 128), src.dtype),
        scratch_shapes=[pltpu.SemaphoreType.DMA],
        compiler_params=_cp(1),
        name="gather_rows",
    )(idx, src)


def _expert_kernel(be_ref, first_ref, nv_ref, xb_ref, wg_ref, wu_ref, wd_ref, yb_ref,
                   wgu_bf, wd_bf):
    i = pl.program_id(0)
    nv = nv_ref[i]

    @pl.when(first_ref[i] == 1)
    def _():
        wgu_bf[:, :DE] = wg_ref[0, 0].astype(bf16)
        wgu_bf[:, DE:] = wu_ref[0, 0].astype(bf16)
        wd_bf[...] = wd_ref[0, 0].astype(bf16)

    @pl.when(nv > 0)
    def _():
        x = _unpack_rows(_load_rows(xb_ref, RB)).astype(bf16)
        gu = jnp.dot(x, wgu_bf[...], preferred_element_type=f32)
        g = gu[:, :DE]
        hmid = (g * _sigmoid(g) * gu[:, DE:]).astype(bf16)
        y = jnp.dot(hmid, wd_bf[...], preferred_element_type=f32)
        _store_rows(yb_ref, _pack_pair(y[:, :HP], y[:, HP:]), RB)

    @pl.when(nv == 0)
    def _():
        yb_ref[...] = jnp.zeros((RB * PG, 128), u32)


def _experts(layer, blk_e, blk_first, blk_nv, xb, wg, wu, wd):
    grid_spec = pltpu.PrefetchScalarGridSpec(
        num_scalar_prefetch=3,
        grid=(NB,),
        in_specs=[
            pl.BlockSpec((RB * PG, 128), lambda i, be, bf, nv: (i, 0)),
            pl.BlockSpec((1, 1, D, DE), lambda i, be, bf, nv: (layer, be[i], 0, 0)),
            pl.BlockSpec((1, 1, D, DE), lambda i, be, bf, nv: (layer, be[i], 0, 0)),
            pl.BlockSpec((1, 1, DE, D), lambda i, be, bf, nv: (layer, be[i], 0, 0)),
        ],
        out_specs=pl.BlockSpec((RB * PG, 128), lambda i, be, bf, nv: (i, 0)),
        scratch_shapes=[pltpu.VMEM((D, 2 * DE), bf16), pltpu.VMEM((DE, D), bf16)],
    )
    return pl.pallas_call(
        _expert_kernel,
        grid_spec=grid_spec,
        out_shape=jax.ShapeDtypeStruct((N_ROWS * PG, 128), u32),
        compiler_params=_cp(1),
        name="experts",
    )(blk_e, blk_first, blk_nv, xb, wg, wu, wd)


PCH = 256
NBP = 256


def _route_pos_kernel(eidx_ref, dest_ref, blk_ref, pre_ref):
    upper = (lax.broadcasted_iota(i32, (PCH, PCH), 0)
             < lax.broadcasted_iota(i32, (PCH, PCH), 1)).astype(bf16)
    eio = lax.broadcasted_iota(i32, (E, PCH), 0)

    def onehot(c0, width, eiota):
        e0 = eidx_ref[0:1, pl.ds(c0, width)]
        e1 = eidx_ref[1:2, pl.ds(c0, width)]
        return jnp.concatenate([e0 == eiota, e1 == eiota], axis=0)

    ones_r = jnp.ones((8, PCH), bf16)

    def count(c, carry):
        col, row = carry
        c0 = pl.multiple_of(c * PCH, PCH)
        oh = onehot(c0, PCH, eio).astype(bf16)
        pre_ref[:, pl.ds(c0, PCH)] = col + jnp.dot(oh, upper, preferred_element_type=f32)
        col = col + jnp.sum(oh.astype(f32), axis=1, keepdims=True)
        row = row + _dot_nt(ones_r, oh[:E]) + _dot_nt(ones_r, oh[E:])
        return col, row

    tot_c, tot_r = lax.fori_loop(
        0, T // PCH, count, (jnp.zeros((2 * E, 1), f32), jnp.zeros((8, E), f32)))
    cnt0_c = tot_c[:E]
    cnt_c = tot_c[:E] + tot_c[E:]
    shift = RB.bit_length() - 1
    assert RB == 1 << shift
    padded_r = lax.shift_left(
        lax.shift_right_logical(tot_r[0:1].astype(i32) + (RB - 1), shift), shift).astype(f32)
    ee = lax.broadcasted_iota(i32, (E, E), 0)
    ep = lax.broadcasted_iota(i32, (E, E), 1)
    start_col = jnp.sum(jnp.where(ep < ee, padded_r, 0.0), axis=1, keepdims=True)
    end_col = jnp.sum(jnp.where(ep <= ee, padded_r, 0.0), axis=1, keepdims=True)
    vend_col = start_col + cnt_c
    basef = jnp.concatenate([start_col, start_col + cnt0_c], axis=0)

    wide = 2048
    eiow = lax.broadcasted_iota(i32, (E, wide), 0)

    def place(c, carry):
        c0 = pl.multiple_of(c * wide, wide)
        oh = onehot(c0, wide, eiow)
        val = jnp.where(oh, pre_ref[:, pl.ds(c0, wide)] + basef, 0.0)
        d0 = jnp.sum(val[:E], axis=0, keepdims=True)
        d1 = jnp.sum(val[E:], axis=0, keepdims=True)
        dest_ref[0:1, pl.ds(c0, wide)] = d0.astype(i32)
        dest_ref[1:2, pl.ds(c0, wide)] = d1.astype(i32)
        return carry

    lax.fori_loop(0, T // wide, place, 0)

    start_b = (lax.broadcasted_iota(i32, (1, NBP), 1) * RB).astype(f32)
    be = jnp.minimum(jnp.sum((end_col <= start_b).astype(f32), axis=0, keepdims=True), E - 1.0)
    bprev = jnp.minimum(
        jnp.sum((end_col <= start_b - RB).astype(f32), axis=0, keepdims=True), E - 1.0)
    first = jnp.logical_or(start_b == 0.0, be != bprev)
    eion = lax.broadcasted_iota(i32, (E, NBP), 0).astype(f32)
    vend_b = jnp.sum(jnp.where(eion == be, vend_col, 0.0), axis=0, keepdims=True)
    nv = jnp.clip(vend_b - start_b, 0.0, float(RB))
    blk_ref[...] = jnp.zeros((8, NBP), i32)
    blk_ref[0:1, :] = be.astype(i32)
    blk_ref[1:2, :] = first.astype(i32)
    blk_ref[2:3, :] = nv.astype(i32)


def _route_pos(eidx):
    return pl.pallas_call(
        _route_pos_kernel,
        in_specs=[pl.BlockSpec(memory_space=pltpu.VMEM)],
        out_specs=[pl.BlockSpec(memory_space=pltpu.VMEM)] * 2,
        out_shape=[jax.ShapeDtypeStruct((TOPK, T), i32), jax.ShapeDtypeStruct((8, NBP), i32)],
        scratch_shapes=[pltpu.VMEM((2 * E, T), f32)],
        compiler_params=pltpu.CompilerParams(vmem_limit_bytes=VMEM_LIMIT),
        name="route_pos",
    )(eidx)


TD = 1024
ZR = 1024


def _dispatch_kernel(d0_ref, d1_ref, h2p_ref, xb_ref, zbuf, sem):
    @pl.when(pl.program_id(0) == 0)
    def _():
        zbuf[...] = jnp.zeros((ZR, PG, 128), u32)
        for j in range(N_ROWS // ZR):
            pltpu.make_async_copy(zbuf, xb_ref.at[pl.ds(j * ZR, ZR)], sem).start()
        for j in range(N_ROWS // ZR):
            pltpu.make_async_copy(zbuf, xb_ref.at[pl.ds(j * ZR, ZR)], sem).wait()

    def issue(g, carry):
        r0 = g * DU
        for u in range(DU):
            _row_copy(h2p_ref, xb_ref, sem, r0 + u, d0_ref[r0 + u]).start(priority=0)
            _row_copy(h2p_ref, xb_ref, sem, r0 + u, d1_ref[r0 + u]).start(priority=1)
        return carry

    lax.fori_loop(0, TD // DU, issue, 0)

    def drain(r, carry):
        _row_copy(h2p_ref, xb_ref, sem, 0, 0).wait()
        _row_copy(h2p_ref, xb_ref, sem, 0, 0).wait()
        return carry

    lax.fori_loop(0, TD, drain, 0, unroll=8)


def _dispatch(h2p, d0, d1):
    return pl.pallas_call(
        _dispatch_kernel,
        grid=(T // TD,),
        in_specs=[
            pl.BlockSpec((TD,), lambda i: (i,), memory_space=pltpu.SMEM),
            pl.BlockSpec((TD,), lambda i: (i,), memory_space=pltpu.SMEM),
            pl.BlockSpec((TD, PG, 128), lambda i: (i, 0, 0)),
        ],
        out_specs=pl.BlockSpec(memory_space=pl.ANY),
        out_shape=jax.ShapeDtypeStruct((N_ROWS, PG, 128), u32),
        scratch_shapes=[pltpu.VMEM((ZR, PG, 128), u32), pltpu.SemaphoreType.DMA],
        compiler_params=_cp(1),
        name="dispatch",
    )(d0, d1, h2p)


def _final_kernel(x_ref, yg0_ref, yg1_ref, ewp_ref, modp_ref, g_ref, o_ref):
    o_ref[0] = _rms(_moe_combine(x_ref, yg0_ref, yg1_ref, ewp_ref, modp_ref), g_ref[...])


def _final(x, moe_in, g):
    return pl.pallas_call(
        _final_kernel,
        grid=(B, NT),
        in_specs=[pl.BlockSpec((1, TM, D), lambda b, s: (b, s, 0))] + _moe_in_specs()
        + [_full((1, D))],
        out_specs=pl.BlockSpec((1, TM, D), lambda b, s: (b, s, 0)),
        out_shape=jax.ShapeDtypeStruct((B, S, D), f32),
        compiler_params=_cp(2),
        name="final_norm",
    )(x, *moe_in, g)


def _moe(layer, h2p, eidx, wg, wu, wd):
    dest, blk = _route_pos(eidx)
    xb = _dispatch(h2p.reshape(T, PG, 128), dest[0], dest[1])
    yb = _experts(layer, blk[0, :NB], blk[1, :NB], blk[2, :NB], xb.reshape(N_ROWS * PG, 128),
                  wg, wu, wd)
    yg = _gather_rows(yb.reshape(N_ROWS, PG, 128), dest.reshape(-1))
    return yg.reshape(TOPK * T * PG, 128)


def kernel(x, c, positions, ada_w, ada_b, norm1_g, norm2_g, conv_w_in, conv_b_in, conv_w_dw,
           conv_b_dw, conv_ln_g, conv_ln_b, conv_w_out, kv_in_g, w_dkv, kv_norm_g, w_ukv, w_dq,
           q_norm_g, w_uq, w_o, router_w, router_b, exp_w_gate, exp_w_up, exp_w_down, final_g):
    c8 = jnp.pad(c, ((0, 8 - B), (0, 0)))
    mod = _ada(c8, ada_w, ada_b[:, None, :])[:, :B].reshape(DEPTH, B, 6, D)

    invf = (ROPE_THETA ** (-jnp.arange(0, DR, 2, dtype=f32) / DR))[:, None]
    cos_t, sin_t = _rope_tables(positions[:, None, :], invf)

    rwt = router_w.astype(f32).T
    rw_hi = rwt.astype(bf16)
    rw_lo = (rwt - rw_hi.astype(f32)).astype(bf16)
    rw = jnp.concatenate([rw_hi, rw_lo], axis=0)
    rb = router_b.astype(f32)[:, None]

    w_ukv3 = w_ukv.reshape(KVR, H, DN + DV)
    wuk = w_ukv3[:, :, :DN].reshape(KVR, H * DN).astype(bf16)
    wuvt = w_ukv3[:, :, DN:].reshape(KVR, H * DV).T.astype(bf16)
    kv_w = (kv_in_g[None, :], w_dkv[:, :KVR].astype(bf16), w_dkv[:, KVR:].T.astype(bf16),
            kv_norm_g[None, :], wuk, wuvt)

    moe_in = None
    kc = vt = None
    for l in range(DEPTH):
        if l < N_A:
            wdw = jnp.pad(conv_w_dw[l], ((0, HALO - CONV_W), (0, 0))).reshape(HALO, CG, 128)
            x, h2p, eidx, ew = _conv_layer(
                x, moe_in, mod[l], norm1_g[l][None, :], conv_w_in[l].astype(bf16),
                conv_b_in[l][None, :], wdw, conv_b_dw[l].reshape(CG, 128), conv_ln_g[l][None, :],
                conv_ln_b[l][None, :], conv_w_out[l].astype(bf16), norm2_g[l][None, :], rw, rb)
        else:
            j = l - N_A
            outs = _preattn(
                x, moe_in, mod[l], norm1_g[l][None, :], w_dq[j].astype(bf16),
                q_norm_g[j][None, :], w_uq[j].T.astype(bf16), cos_t, sin_t,
                kv=kv_w if j == 0 else None)
            if j == 0:
                x, qt, kc, vt = outs
            else:
                x, qt = outs
            o = _attention(qt, kc, vt)
            x, h2p, eidx, ew = _postattn(
                x, o, mod[l], w_o[j].astype(bf16), norm2_g[l][None, :], rw, rb)
        yg = _moe(l, h2p, eidx, exp_w_gate, exp_w_up, exp_w_down)
        moe_in = (yg, yg, ew, mod[l])
    return _final(x, moe_in, final_g[None, :])
```

```python
import functools
import math

import jax
import jax.numpy as jnp
from jax import lax
from jax.experimental import pallas as pl
from jax.experimental.pallas import tpu as pltpu

f32 = jnp.float32
bf16 = jnp.bfloat16
u32 = jnp.uint32
i32 = jnp.int32

D = 1024
B = 4
S = 4096
T = B * S
DEPTH = 4
N_A = 2
CONV_W = 31
C = 1024
H = 8
DN = 128
DR = 64
DV = 128
DQK = DN + DR
QR = 384
KVR = 256
ROPE_THETA = 10000.0
ATTN_SCALE = 1.0 / math.sqrt(DN + DR)
QSCALE = ATTN_SCALE * math.log2(math.e)
E = 16
NG = 4
EG = 4
TOPK = 2
DE = 512
EPS = 1e-6
NEG = -1e30

TM = 512
NT = S // TM
HALO = 32
CG = C // 128
OB = 8
TH = 256
TQ = 512
TK = 512
RB = 512
N_ROWS = T * TOPK + E * RB
NB = N_ROWS // RB
GR = 2048
DU = 8
HP = D // 2
PG = HP // 128

VMEM_LIMIT = 56 * 1024 * 1024


def _cp(n_axes):
    return pltpu.CompilerParams(
        dimension_semantics=("arbitrary",) * n_axes, vmem_limit_bytes=VMEM_LIMIT)


def _full(shape):
    n = len(shape)
    return pl.BlockSpec(shape, lambda *_: (0,) * n)


def _rms(x, g):
    return x * lax.rsqrt(jnp.mean(x * x, axis=-1, keepdims=True) + EPS) * g


def _sigmoid(x):
    return 0.5 * jnp.tanh(0.5 * x) + 0.5


def _pack_pair(a, b):
    ua = lax.bitcast_convert_type(a.astype(bf16).astype(f32), u32)
    ub = lax.bitcast_convert_type(b.astype(bf16).astype(f32), u32)
    return (ua >> 16) | ub


def _unpack_pair(w):
    a = lax.bitcast_convert_type(w << 16, f32)
    b = lax.bitcast_convert_type(w & jnp.uint32(0xFFFF0000), f32)
    return a, b


def _unpack_rows(w):
    a, b = _unpack_pair(w)
    return jnp.concatenate([a, b], axis=1)


def _load_rows(ref, rows, row0=0):
    return jnp.concatenate(
        [ref[pl.ds(row0 * PG + k, rows, stride=PG), :] for k in range(PG)], axis=1)


def _store_rows(ref, w, rows, row0=0):
    for k in range(PG):
        ref[pl.ds(row0 * PG + k, rows, stride=PG), :] = w[:, 128 * k:128 * (k + 1)]


def _dot_nt(a, b):
    return lax.dot_general(a, b, (((1,), (1,)), ((), ())), preferred_element_type=f32)


def _top2sum4(a, b, c, d):
    lo1, hi1 = jnp.minimum(a, b), jnp.maximum(a, b)
    lo2, hi2 = jnp.minimum(c, d), jnp.maximum(c, d)
    return jnp.maximum(hi1, hi2) + jnp.maximum(jnp.minimum(hi1, hi2), jnp.maximum(lo1, lo2))


def _route_and_pack(x_new, mod_ref, n2g_ref, rw_ref, rb_ref, h2p_ref, eidx_ref, ewt_ref,
                    row0=0):
    n = x_new.shape[0]
    sh2 = mod_ref[0, 3:4, :]
    sc2 = mod_ref[0, 4:5, :]
    h2 = _rms(x_new, n2g_ref[...]) * (1.0 + sc2) + sh2
    _store_rows(h2p_ref, _pack_pair(h2[:, :HP], h2[:, HP:]), n, row0)

    hi = h2.astype(bf16)
    lo = (h2 - hi.astype(f32)).astype(bf16)
    rw = rw_ref[...]
    p = _dot_nt(rw, hi) + _dot_nt(rw, lo)
    logits = p[:E] + p[E:]
    s = jax.nn.sigmoid(logits)
    sb = s + rb_ref[...]

    srow = [s[e:e + 1] for e in range(E)]
    brow = [sb[e:e + 1] for e in range(E)]
    gs = [_top2sum4(*brow[EG * g:EG * g + EG]) for g in range(NG)]
    gidx = jnp.zeros_like(gs[0], dtype=i32)
    best = gs[0]
    for g in range(1, NG):
        upd = gs[g] > best
        gidx = jnp.where(upd, g, gidx)
        best = jnp.where(upd, gs[g], best)

    def pick(rows, i):
        out = rows[i]
        for g in range(1, NG):
            out = jnp.where(gidx == g, rows[EG * g + i], out)
        return out

    wv = [pick(brow, i) for i in range(EG)]
    sv = [pick(srow, i) for i in range(EG)]

    l1 = jnp.zeros_like(gidx)
    b1 = wv[0]
    for i in range(1, EG):
        upd = wv[i] > b1
        l1 = jnp.where(upd, i, l1)
        b1 = jnp.where(upd, wv[i], b1)
    l2 = jnp.zeros_like(gidx)
    b2 = jnp.where(l1 == 0, -jnp.inf, wv[0])
    for i in range(1, EG):
        cand = jnp.where(l1 == i, -jnp.inf, wv[i])
        upd = cand > b2
        l2 = jnp.where(upd, i, l2)
        b2 = jnp.where(upd, cand, b2)

    def sel(loc):
        out = sv[0]
        for i in range(1, EG):
            out = jnp.where(loc == i, sv[i], out)
        return out

    w1 = sel(l1)
    w2 = sel(l2)
    tot = w1 + w2
    eidx_ref[:, row0:row0 + n] = jnp.concatenate([gidx * EG + l1, gidx * EG + l2], axis=0)
    wt = jnp.concatenate([w1 / tot, w2 / tot, jnp.zeros((128 - TOPK, n), f32)], axis=0)
    ewt_ref[row0:row0 + n, :] = wt.T


def _moe_combine(x_ref, yg0_ref, yg1_ref, ewt_ref, modp_ref, row0=0, rows=TM):
    y0 = _unpack_rows(_load_rows(yg0_ref, rows, row0))
    y1 = _unpack_rows(_load_rows(yg1_ref, rows, row0))
    w = ewt_ref[row0:row0 + rows, :]
    g2 = modp_ref[0, 5:6, :]
    return x_ref[0, row0:row0 + rows, :] + g2 * (w[:, 0:1] * y0 + w[:, 1:2] * y1)


def _ada_kernel(c_ref, w_ref, b_ref, o_ref):
    c = c_ref[...]
    ca = (c * jax.nn.sigmoid(c)).astype(bf16)
    o_ref[0] = jnp.dot(ca, w_ref[0].astype(bf16), preferred_element_type=f32) + b_ref[0]


def _ada(c8, ada_w, ada_b3):
    tn = 1536
    return pl.pallas_call(
        _ada_kernel,
        grid=(DEPTH, 6 * D // tn),
        in_specs=[
            pl.BlockSpec((8, D), lambda l, n: (0, 0)),
            pl.BlockSpec((1, D, tn), lambda l, n: (l, 0, n)),
            pl.BlockSpec((1, 1, tn), lambda l, n: (l, 0, n)),
        ],
        out_specs=pl.BlockSpec((1, 8, tn), lambda l, n: (l, 0, n)),
        out_shape=jax.ShapeDtypeStruct((DEPTH, 8, 6 * D), f32),
        compiler_params=_cp(2),
        name="ada",
    )(c8, ada_w, ada_b3)


def _rope_kernel(pos_ref, invf_ref, cos_ref, sin_ref):
    ang = pos_ref[0].astype(f32) * invf_ref[...]
    cos_ref[0] = jnp.cos(ang)
    sin_ref[0] = jnp.sin(ang)


def _rope_tables(pos3, invf):
    return pl.pallas_call(
        _rope_kernel,
        grid=(B,),
        in_specs=[pl.BlockSpec((1, 1, S), lambda b: (b, 0, 0)), _full((DR // 2, 1))],
        out_specs=[pl.BlockSpec((1, DR // 2, S), lambda b: (b, 0, 0))] * 2,
        out_shape=[jax.ShapeDtypeStruct((B, DR // 2, S), f32)] * 2,
        compiler_params=_cp(1),
        name="rope",
    )(pos3, invf)


def _conv_kernel(has_moe_in, *refs):
    if has_moe_in:
        x_ref, yg0_ref, yg1_ref, ewp_ref, modp_ref = refs[:5]
        refs = refs[5:]
    else:
        x_ref = refs[0]
        refs = refs[1:]
    (mod_ref, n1g_ref, win_ref, bin_ref, wdw_ref, bdw_ref, lng_ref, lnb_ref, wout_ref,
     n2g_ref, rw_ref, rb_ref,
     xo_ref, h2p_ref, eidx_ref, ewt_ref, ubuf, obuf) = refs
    si = pl.program_id(1)
    sh1 = mod_ref[0, 0:1, :]
    sc1 = mod_ref[0, 1:2, :]
    g1 = mod_ref[0, 2:3, :]

    @pl.when(si == 0)
    def _():
        ubuf[0:HALO * CG, :] = jnp.zeros((HALO * CG, 128), f32)

    @pl.when(si > 0)
    def _():
        ubuf[0:HALO * CG, :] = ubuf[TM * CG:(TM + HALO) * CG, :]

    def front(r0):
        if has_moe_in:
            x = _moe_combine(x_ref, yg0_ref, yg1_ref, ewp_ref, modp_ref, r0, TH)
        else:
            x = x_ref[0, r0:r0 + TH, :]
        xo_ref[0, r0:r0 + TH, :] = x
        h = (_rms(x, n1g_ref[...]) * (1.0 + sc1) + sh1).astype(bf16)
        u = jnp.dot(h, win_ref[...], preferred_element_type=f32) + bin_ref[...]
        glu = u[:, :C] * _sigmoid(u[:, C:])
        for k in range(CG):
            ubuf[pl.ds((HALO + r0) * CG + k, TH, stride=CG), :] = glu[:, 128 * k:128 * (k + 1)]

    off = HALO - (CONV_W - 1)
    taps = [wdw_ref[j] for j in range(CONV_W)]
    bias = bdw_ref[...]

    def conv(r0):
        for t0 in range(r0, r0 + TH, OB):
            acc = [bias] * OB
            for dd in range(OB + CONV_W - 1):
                row = (t0 + off + dd) * CG
                uv = ubuf[row:row + CG, :]
                for o in range(OB):
                    j = dd - o
                    if 0 <= j < CONV_W:
                        acc[o] = acc[o] + uv * taps[j]
            for o in range(OB):
                obuf[(t0 + o) * CG:(t0 + o + 1) * CG, :] = acc[o]

    def back(r0):
        v = jnp.concatenate(
            [obuf[pl.ds(r0 * CG + k, TH, stride=CG), :] for k in range(CG)], axis=1)
        mu = jnp.mean(v, axis=-1, keepdims=True)
        vc = v - mu
        var = jnp.mean(vc * vc, axis=-1, keepdims=True)
        y = vc * lax.rsqrt(var + EPS) * lng_ref[...] + lnb_ref[...]
        y = (y * _sigmoid(y)).astype(bf16)
        mix = jnp.dot(y, wout_ref[...], preferred_element_type=f32)
        x_new = xo_ref[0, r0:r0 + TH, :] + g1 * mix
        xo_ref[0, r0:r0 + TH, :] = x_new
        _route_and_pack(x_new, mod_ref, n2g_ref, rw_ref, rb_ref, h2p_ref, eidx_ref, ewt_ref, r0)

    groups = list(range(0, TM, TH))
    front(groups[0])
    for n, r0 in enumerate(groups):
        if n + 1 < len(groups):
            front(groups[n + 1])
        conv(r0)
        if n > 0:
            back(groups[n - 1])
    back(groups[-1])


def _tok_specs():
    specs = [
        pl.BlockSpec((1, TM, D), lambda b, s: (b, s, 0)),
        pl.BlockSpec((TM * PG, 128), lambda b, s: (b * NT + s, 0)),
        pl.BlockSpec((TOPK, TM), lambda b, s: (0, b * NT + s)),
        pl.BlockSpec((TM, 128), lambda b, s: (b * NT + s, 0)),
    ]
    shapes = [
        jax.ShapeDtypeStruct((B, S, D), f32),
        jax.ShapeDtypeStruct((T * PG, 128), u32),
        jax.ShapeDtypeStruct((TOPK, T), i32),
        jax.ShapeDtypeStruct((T, 128), f32),
    ]
    return specs, shapes


def _moe_in_specs():
    return [
        pl.BlockSpec((TM * PG, 128), lambda b, s: (b * NT + s, 0)),
        pl.BlockSpec((TM * PG, 128), lambda b, s: (T // TM + b * NT + s, 0)),
        pl.BlockSpec((TM, 128), lambda b, s: (b * NT + s, 0)),
        pl.BlockSpec((1, 6, D), lambda b, s: (b, 0, 0)),
    ]


def _conv_layer(x, moe_in, mod_l, n1g, win, b_in, wdw, bdw, lng, lnb, wout, n2g, rw, rb):
    has_moe_in = moe_in is not None
    in_specs = [pl.BlockSpec((1, TM, D), lambda b, s: (b, s, 0))]
    args = [x]
    if has_moe_in:
        in_specs += _moe_in_specs()
        args += list(moe_in)
    in_specs += [
        pl.BlockSpec((1, 6, D), lambda b, s: (b, 0, 0)),
        _full((1, D)), _full((D, 2 * C)), _full((1, 2 * C)), _full((HALO, CG, 128)),
        _full((CG, 128)),
        _full((1, C)), _full((1, C)), _full((C, D)), _full((1, D)), _full((2 * E, D)),
        _full((E, 1)),
    ]
    args += [mod_l, n1g, win, b_in, wdw, bdw, lng, lnb, wout, n2g, rw, rb]
    out_specs, out_shape = _tok_specs()
    return pl.pallas_call(
        functools.partial(_conv_kernel, has_moe_in),
        grid=(B, NT),
        in_specs=in_specs,
        out_specs=out_specs,
        out_shape=out_shape,
        scratch_shapes=[pltpu.VMEM(((TM + HALO) * CG, 128), f32),
                        pltpu.VMEM((TM * CG, 128), f32)],
        compiler_params=_cp(2),
        name="conv_layer",
    )(*args)


def _preattn_kernel(with_kv, *refs):
    (x_ref, yg0_ref, yg1_ref, ewp_ref, modp_ref, mod_ref, n1g_ref, wdq_ref, qng_ref, wuqt_ref,
     cos_ref, sin_ref) = refs[:12]
    refs = refs[12:]
    if with_kv:
        kvg_ref, wlat_ref, wropet_ref, kvng_ref, wuk_ref, wuvt_ref = refs[:6]
        refs = refs[6:]
        xo_ref, qt_ref, kc_ref, vt_ref = refs
    else:
        xo_ref, qt_ref = refs

    x = _moe_combine(x_ref, yg0_ref, yg1_ref, ewp_ref, modp_ref)
    xo_ref[0] = x
    cos = cos_ref[0]
    sin = sin_ref[0]
    sh1 = mod_ref[0, 0:1, :]
    sc1 = mod_ref[0, 1:2, :]
    h = (_rms(x, n1g_ref[...]) * (1.0 + sc1) + sh1).astype(bf16)
    cq = jnp.dot(h, wdq_ref[...], preferred_element_type=f32)
    cq = _rms(cq, qng_ref[...]).astype(bf16)
    qt = _dot_nt(wuqt_ref[...], cq)
    hr = DR // 2
    for hh in range(H):
        r0 = hh * DQK
        x1 = qt[r0 + DN:r0 + DN + hr]
        x2 = qt[r0 + DN + hr:r0 + DQK]
        qt_ref[0, hh, 0:DN, :] = (qt[r0:r0 + DN] * QSCALE).astype(bf16)
        qt_ref[0, hh, DN:DN + hr, :] = ((x1 * cos - x2 * sin) * QSCALE).astype(bf16)
        qt_ref[0, hh, DN + hr:DQK, :] = ((x1 * sin + x2 * cos) * QSCALE).astype(bf16)

    if with_kv:
        xn = _rms(x, kvg_ref[...]).astype(bf16)
        lat = jnp.dot(xn, wlat_ref[...], preferred_element_type=f32)
        latn = _rms(lat, kvng_ref[...]).astype(bf16)
        krt = _dot_nt(wropet_ref[...], xn)
        k1 = krt[:hr]
        k2 = krt[hr:]
        kr = jnp.concatenate([k1 * cos - k2 * sin, k1 * sin + k2 * cos], axis=0)
        kr = kr.T.astype(bf16)
        kn = jnp.dot(latn, wuk_ref[...], preferred_element_type=f32).astype(bf16)
        vt = _dot_nt(wuvt_ref[...], latn).astype(bf16)
        for hh in range(H):
            kc_ref[0, hh, :, 0:DN] = kn[:, hh * DN:(hh + 1) * DN]
            kc_ref[0, hh, :, DN:DQK] = kr
            vt_ref[0, hh] = vt[hh * DV:(hh + 1) * DV]


def _preattn(x, moe_in, mod_l, n1g, wdq, qng, wuqt, cos_t, sin_t, kv=None):
    with_kv = kv is not None
    in_specs = [pl.BlockSpec((1, TM, D), lambda b, s: (b, s, 0))] + _moe_in_specs() + [
        pl.BlockSpec((1, 6, D), lambda b, s: (b, 0, 0)),
        _full((1, D)), _full((D, QR)), _full((1, QR)), _full((H * DQK, QR)),
        pl.BlockSpec((1, DR // 2, TM), lambda b, s: (b, 0, s)),
        pl.BlockSpec((1, DR // 2, TM), lambda b, s: (b, 0, s)),
    ]
    args = [x, *moe_in, mod_l, n1g, wdq, qng, wuqt, cos_t, sin_t]
    out_specs = [
        pl.BlockSpec((1, TM, D), lambda b, s: (b, s, 0)),
        pl.BlockSpec((1, H, DQK, TM), lambda b, s: (b, 0, 0, s)),
    ]
    out_shape = [
        jax.ShapeDtypeStruct((B, S, D), f32),
        jax.ShapeDtypeStruct((B, H, DQK, S), bf16),
    ]
    if with_kv:
        in_specs += [_full((1, D)), _full((D, KVR)), _full((DR, D)), _full((1, KVR)),
                     _full((KVR, H * DN)), _full((H * DV, KVR))]
        args += list(kv)
        out_specs += [
            pl.BlockSpec((1, H, TM, DQK), lambda b, s: (b, 0, s, 0)),
            pl.BlockSpec((1, H, DV, TM), lambda b, s: (b, 0, 0, s)),
        ]
        out_shape += [
            jax.ShapeDtypeStruct((B, H, S, DQK), bf16),
            jax.ShapeDtypeStruct((B, H, DV, S), bf16),
        ]
    return pl.pallas_call(
        functools.partial(_preattn_kernel, with_kv),
        grid=(B, NT),
        in_specs=in_specs,
        out_specs=out_specs,
        out_shape=out_shape,
        compiler_params=_cp(2),
        name="preattn_kv" if with_kv else "preattn",
    )(*args)


def _attn_kernel(qt_ref, kc_ref, vt_ref, o_ref, m_ref, l_ref, acc_ref, s_ref):
    assert TQ == TK
    pairs = [(qi, j) for qi in range(S // TQ) for j in range(qi + 1)]

    def scores(qi, j, slot):
        q = qt_ref[0, 0, :, qi * TQ:(qi + 1) * TQ]
        k = kc_ref[0, 0, j * TK:(j + 1) * TK, :]
        s_ref[slot] = jnp.dot(k, q, preferred_element_type=f32)

    def softmax_pv(qi, j, slot):
        s = s_ref[slot]
        if j == qi:
            keep = (lax.broadcasted_iota(i32, (TK, TQ), 0)
                    <= lax.broadcasted_iota(i32, (TK, TQ), 1))
            s = jnp.where(keep, s, NEG)
        v = vt_ref[0, 0, :, j * TK:(j + 1) * TK]
        if j == 0:
            m_new = jnp.max(s, axis=0, keepdims=True)
            p = jnp.exp2(s - m_new)
            l_ref[...] = jnp.sum(p, axis=0, keepdims=True)
            acc_ref[...] = jnp.dot(v, p.astype(bf16), preferred_element_type=f32)
        else:
            m_old = m_ref[...]
            m_new = jnp.maximum(m_old, jnp.max(s, axis=0, keepdims=True))
            alpha = jnp.exp2(m_old - m_new)
            p = jnp.exp2(s - m_new)
            l_ref[...] = alpha * l_ref[...] + jnp.sum(p, axis=0, keepdims=True)
            acc_ref[...] = alpha * acc_ref[...] + jnp.dot(
                v, p.astype(bf16), preferred_element_type=f32)
        m_ref[...] = m_new
        if j == qi:
            o = acc_ref[...] / l_ref[...]
            o_ref[0, qi * TQ:(qi + 1) * TQ, :] = o.T.astype(bf16)

    scores(*pairs[0], 0)
    for n, (qi, j) in enumerate(pairs):
        if n + 1 < len(pairs):
            scores(*pairs[n + 1], (n + 1) % 2)
        softmax_pv(qi, j, n % 2)


def _attention(qt, kc, vt):
    return pl.pallas_call(
        _attn_kernel,
        grid=(B, H),
        in_specs=[
            pl.BlockSpec((1, 1, DQK, S), lambda b, h: (b, h, 0, 0)),
            pl.BlockSpec((1, 1, S, DQK), lambda b, h: (b, h, 0, 0)),
            pl.BlockSpec((1, 1, DV, S), lambda b, h: (b, h, 0, 0)),
        ],
        out_specs=pl.BlockSpec((1, S, DV), lambda b, h: (b, 0, h)),
        out_shape=jax.ShapeDtypeStruct((B, S, H * DV), bf16),
        scratch_shapes=[pltpu.VMEM((1, TQ), f32), pltpu.VMEM((1, TQ), f32),
                        pltpu.VMEM((DV, TQ), f32), pltpu.VMEM((2, TK, TQ), f32)],
        compiler_params=_cp(2),
        name="attn",
    )(qt, kc, vt)


def _postattn_kernel(x_ref, o_ref, mod_ref, wo_ref, n2g_ref, rw_ref, rb_ref,
                     xo_ref, h2p_ref, eidx_ref, ewt_ref):
    g1 = mod_ref[0, 2:3, :]
    mix = jnp.dot(o_ref[0], wo_ref[...], preferred_element_type=f32)
    x_new = x_ref[0] + g1 * mix
    xo_ref[0] = x_new
    _route_and_pack(x_new, mod_ref, n2g_ref, rw_ref, rb_ref, h2p_ref, eidx_ref, ewt_ref)


def _postattn(x, o, mod_l, wo, n2g, rw, rb):
    out_specs, out_shape = _tok_specs()
    return pl.pallas_call(
        _postattn_kernel,
        grid=(B, NT),
        in_specs=[
            pl.BlockSpec((1, TM, D), lambda b, s: (b, s, 0)),
            pl.BlockSpec((1, TM, H * DV), lambda b, s: (b, s, 0)),
            pl.BlockSpec((1, 6, D), lambda b, s: (b, 0, 0)),
            _full((H * DV, D)), _full((1, D)), _full((2 * E, D)), _full((E, 1)),
        ],
        out_specs=out_specs,
        out_shape=out_shape,
        compiler_params=_cp(2),
        name="postattn",
    )(x, o, mod_l, wo, n2g, rw, rb)


def _row_copy(src_ref, dst_ref, sem, src_row, dst_row):
    return pltpu.make_async_copy(src_ref.at[src_row], dst_ref.at[dst_row], sem)


def _gather_kernel(idx_ref, src_ref, dst_ref, sem):
    def issue(g, carry):
        r0 = g * DU
        for u in range(DU):
            _row_copy(src_ref, dst_ref, sem, idx_ref[r0 + u], r0 + u).start(priority=u % 2)
        return carry

    lax.fori_loop(0, GR // DU, issue, 0)

    def drain(r, carry):
        _row_copy(src_ref, dst_ref, sem, 0, r).wait()
        return carry

    lax.fori_loop(0, GR, drain, 0, unroll=8)


def _gather_rows(src, idx):
    n = idx.shape[0]
    return pl.pallas_call(
        _gather_kernel,
        grid=(n // GR,),
        in_specs=[
            pl.BlockSpec((GR,), lambda i: (i,), memory_space=pltpu.SMEM),
            pl.BlockSpec(memory_space=pl.ANY),
        ],
        out_specs=pl.BlockSpec((GR, PG, 128), lambda i: (i, 0, 0)),
        out_shape=jax.ShapeDtypeStruct((n, PG, 128), src.dtype),
        scratch_shapes=[pltpu.SemaphoreType.DMA],
        compiler_params=_cp(1),
        name="gather_rows",
    )(idx, src)


def _expert_kernel(be_ref, first_ref, nv_ref, xb_ref, wg_ref, wu_ref, wd_ref, yb_ref,
                   wgu_bf, wd_bf):
    i = pl.program_id(0)
    nv = nv_ref[i]

    @pl.when(first_ref[i] == 1)
    def _():
        wgu_bf[:, :DE] = wg_ref[0, 0].astype(bf16)
        wgu_bf[:, DE:] = wu_ref[0, 0].astype(bf16)
        wd_bf[...] = wd_ref[0, 0].astype(bf16)

    @pl.when(nv > 0)
    def _():
        x = _unpack_rows(_load_rows(xb_ref, RB)).astype(bf16)
        gu = jnp.dot(x, wgu_bf[...], preferred_element_type=f32)
        g = gu[:, :DE]
        hmid = (g * _sigmoid(g) * gu[:, DE:]).astype(bf16)
        y = jnp.dot(hmid, wd_bf[...], preferred_element_type=f32)
        _store_rows(yb_ref, _pack_pair(y[:, :HP], y[:, HP:]), RB)

    @pl.when(nv == 0)
    def _():
        yb_ref[...] = jnp.zeros((RB * PG, 128), u32)


def _experts(layer, blk_e, blk_first, blk_nv, xb, wg, wu, wd):
    grid_spec = pltpu.PrefetchScalarGridSpec(
        num_scalar_prefetch=3,
        grid=(NB,),
        in_specs=[
            pl.BlockSpec((RB * PG, 128), lambda i, be, bf, nv: (i, 0)),
            pl.BlockSpec((1, 1, D, DE), lambda i, be, bf, nv: (layer, be[i], 0, 0)),
            pl.BlockSpec((1, 1, D, DE), lambda i, be, bf, nv: (layer, be[i], 0, 0)),
            pl.BlockSpec((1, 1, DE, D), lambda i, be, bf, nv: (layer, be[i], 0, 0)),
        ],
        out_specs=pl.BlockSpec((RB * PG, 128), lambda i, be, bf, nv: (i, 0)),
        scratch_shapes=[pltpu.VMEM((D, 2 * DE), bf16), pltpu.VMEM((DE, D), bf16)],
    )
    return pl.pallas_call(
        _expert_kernel,
        grid_spec=grid_spec,
        out_shape=jax.ShapeDtypeStruct((N_ROWS * PG, 128), u32),
        compiler_params=_cp(1),
        name="experts",
    )(blk_e, blk_first, blk_nv, xb, wg, wu, wd)


PCH = 256
NBP = 256


def _route_pos_kernel(eidx_ref, dest_ref, blk_ref, pre_ref):
    upper = (lax.broadcasted_iota(i32, (PCH, PCH), 0)
             < lax.broadcasted_iota(i32, (PCH, PCH), 1)).astype(bf16)
    eio = lax.broadcasted_iota(i32, (E, PCH), 0)

    def onehot(c0, width, eiota):
        e0 = eidx_ref[0:1, pl.ds(c0, width)]
        e1 = eidx_ref[1:2, pl.ds(c0, width)]
        return jnp.concatenate([e0 == eiota, e1 == eiota], axis=0)

    ones_r = jnp.ones((8, PCH), bf16)

    def count(c, carry):
        col, row = carry
        c0 = pl.multiple_of(c * PCH, PCH)
        oh = onehot(c0, PCH, eio).astype(bf16)
        pre_ref[:, pl.ds(c0, PCH)] = col + jnp.dot(oh, upper, preferred_element_type=f32)
        col = col + jnp.sum(oh.astype(f32), axis=1, keepdims=True)
        row = row + _dot_nt(ones_r, oh[:E]) + _dot_nt(ones_r, oh[E:])
        return col, row

    tot_c, tot_r = lax.fori_loop(
        0, T // PCH, count, (jnp.zeros((2 * E, 1), f32), jnp.zeros((8, E), f32)))
    cnt0_c = tot_c[:E]
    cnt_c = tot_c[:E] + tot_c[E:]
    shift = RB.bit_length() - 1
    assert RB == 1 << shift
    padded_r = lax.shift_left(
        lax.shift_right_logical(tot_r[0:1].astype(i32) + (RB - 1), shift), shift).astype(f32)
    ee = lax.broadcasted_iota(i32, (E, E), 0)
    ep = lax.broadcasted_iota(i32, (E, E), 1)
    start_col = jnp.sum(jnp.where(ep < ee, padded_r, 0.0), axis=1, keepdims=True)
    end_col = jnp.sum(jnp.where(ep <= ee, padded_r, 0.0), axis=1, keepdims=True)
    vend_col = start_col + cnt_c
    basef = jnp.concatenate([start_col, start_col + cnt0_c], axis=0)

    wide = 2048
    eiow = lax.broadcasted_iota(i32, (E, wide), 0)

    def place(c, carry):
        c0 = pl.multiple_of(c * wide, wide)
        oh = onehot(c0, wide, eiow)
        val = jnp.where(oh, pre_ref[:, pl.ds(c0, wide)] + basef, 0.0)
        d0 = jnp.sum(val[:E], axis=0, keepdims=True)
        d1 = jnp.sum(val[E:], axis=0, keepdims=True)
        dest_ref[0:1, pl.ds(c0, wide)] = d0.astype(i32)
        dest_ref[1:2, pl.ds(c0, wide)] = d1.astype(i32)
        return carry

    lax.fori_loop(0, T // wide, place, 0)

    start_b = (lax.broadcasted_iota(i32, (1, NBP), 1) * RB).astype(f32)
    be = jnp.minimum(jnp.sum((end_col <= start_b).astype(f32), axis=0, keepdims=True), E - 1.0)
    bprev = jnp.minimum(
        jnp.sum((end_col <= start_b - RB).astype(f32), axis=0, keepdims=True), E - 1.0)
    first = jnp.logical_or(start_b == 0.0, be != bprev)
    eion = lax.broadcasted_iota(i32, (E, NBP), 0).astype(f32)
    vend_b = jnp.sum(jnp.where(eion == be, vend_col, 0.0), axis=0, keepdims=True)
    nv = jnp.clip(vend_b - start_b, 0.0, float(RB))
    blk_ref[...] = jnp.zeros((8, NBP), i32)
    blk_ref[0:1, :] = be.astype(i32)
    blk_ref[1:2, :] = first.astype(i32)
    blk_ref[2:3, :] = nv.astype(i32)


def _route_pos(eidx):
    return pl.pallas_call(
        _route_pos_kernel,
        in_specs=[pl.BlockSpec(memory_space=pltpu.VMEM)],
        out_specs=[pl.BlockSpec(memory_space=pltpu.VMEM)] * 2,
        out_shape=[jax.ShapeDtypeStruct((TOPK, T), i32), jax.ShapeDtypeStruct((8, NBP), i32)],
        scratch_shapes=[pltpu.VMEM((2 * E, T), f32)],
        compiler_params=pltpu.CompilerParams(vmem_limit_bytes=VMEM_LIMIT),
        name="route_pos",
    )(eidx)


TD = 1024
ZR = 1024


def _dispatch_kernel(d0_ref, d1_ref, h2p_ref, xb_ref, zbuf, sem):
    @pl.when(pl.program_id(0) == 0)
    def _():
        zbuf[...] = jnp.zeros((ZR, PG, 128), u32)
        for j in range(N_ROWS // ZR):
            pltpu.make_async_copy(zbuf, xb_ref.at[pl.ds(j * ZR, ZR)], sem).start()
        for j in range(N_ROWS // ZR):
            pltpu.make_async_copy(zbuf, xb_ref.at[pl.ds(j * ZR, ZR)], sem).wait()

    def issue(g, carry):
        r0 = g * DU
        for u in range(DU):
            _row_copy(h2p_ref, xb_ref, sem, r0 + u, d0_ref[r0 + u]).start(priority=0)
            _row_copy(h2p_ref, xb_ref, sem, r0 + u, d1_ref[r0 + u]).start(priority=1)
        return carry

    lax.fori_loop(0, TD // DU, issue, 0)

    def drain(r, carry):
        _row_copy(h2p_ref, xb_ref, sem, 0, 0).wait()
        _row_copy(h2p_ref, xb_ref, sem, 0, 0).wait()
        return carry

    lax.fori_loop(0, TD, drain, 0, unroll=8)


def _dispatch(h2p, d0, d1):
    return pl.pallas_call(
        _dispatch_kernel,
        grid=(T // TD,),
        in_specs=[
            pl.BlockSpec((TD,), lambda i: (i,), memory_space=pltpu.SMEM),
            pl.BlockSpec((TD,), lambda i: (i,), memory_space=pltpu.SMEM),
            pl.BlockSpec((TD, PG, 128), lambda i: (i, 0, 0)),
        ],
        out_specs=pl.BlockSpec(memory_space=pl.ANY),
        out_shape=jax.ShapeDtypeStruct((N_ROWS, PG, 128), u32),
        scratch_shapes=[pltpu.VMEM((ZR, PG, 128), u32), pltpu.SemaphoreType.DMA],
        compiler_params=_cp(1),
        name="dispatch",
    )(d0, d1, h2p)


def _final_kernel(x_ref, yg0_ref, yg1_ref, ewp_ref, modp_ref, g_ref, o_ref):
    o_ref[0] = _rms(_moe_combine(x_ref, yg0_ref, yg1_ref, ewp_ref, modp_ref), g_ref[...])


def _final(x, moe_in, g):
    return pl.pallas_call(
        _final_kernel,
        grid=(B, NT),
        in_specs=[pl.BlockSpec((1, TM, D), lambda b, s: (b, s, 0))] + _moe_in_specs()
        + [_full((1, D))],
        out_specs=pl.BlockSpec((1, TM, D), lambda b, s: (b, s, 0)),
        out_shape=jax.ShapeDtypeStruct((B, S, D), f32),
        compiler_params=_cp(2),
        name="final_norm",
    )(x, *moe_in, g)


def _moe(layer, h2p, eidx, wg, wu, wd):
    dest, blk = _route_pos(eidx)
    xb = _dispatch(h2p.reshape(T, PG, 128), dest[0], dest[1])
    yb = _experts(layer, blk[0, :NB], blk[1, :NB], blk[2, :NB], xb.reshape(N_ROWS * PG, 128),
                  wg, wu, wd)
    yg = _gather_rows(yb.reshape(N_ROWS, PG, 128), dest.reshape(-1))
    return yg.reshape(TOPK * T * PG, 128)


def kernel(x, c, positions, ada_w, ada_b, norm1_g, norm2_g, conv_w_in, conv_b_in, conv_w_dw,
           conv_b_dw, conv_ln_g, conv_ln_b, conv_w_out, kv_in_g, w_dkv, kv_norm_g, w_ukv, w_dq,
           q_norm_g, w_uq, w_o, router_w, router_b, exp_w_gate, exp_w_up, exp_w_down, final_g):
    c8 = jnp.pad(c, ((0, 8 - B), (0, 0)))
    mod = _ada(c8, ada_w, ada_b[:, None, :])[:, :B].reshape(DEPTH, B, 6, D)

    invf = (ROPE_THETA ** (-jnp.arange(0, DR, 2, dtype=f32) / DR))[:, None]
    cos_t, sin_t = _rope_tables(positions[:, None, :], invf)

    rwt = router_w.astype(f32).T
    rw_hi = rwt.astype(bf16)
    rw_lo = (rwt - rw_hi.astype(f32)).astype(bf16)
    rw = jnp.concatenate([rw_hi, rw_lo], axis=0)
    rb = router_b.astype(f32)[:, None]

    w_ukv3 = w_ukv.reshape(KVR, H, DN + DV)
    wuk = w_ukv3[:, :, :DN].reshape(KVR, H * DN).astype(bf16)
    wuvt = w_ukv3[:, :, DN:].reshape(KVR, H * DV).T.astype(bf16)
    kv_w = (kv_in_g[None, :], w_dkv[:, :KVR].astype(bf16), w_dkv[:, KVR:].T.astype(bf16),
            kv_norm_g[None, :], wuk, wuvt)

    moe_in = None
    kc = vt = None
    for l in range(DEPTH):
        if l < N_A:
            wdw = jnp.pad(conv_w_dw[l], ((0, HALO - CONV_W), (0, 0))).reshape(HALO, CG, 128)
            x, h2p, eidx, ew = _conv_layer(
                x, moe_in, mod[l], norm1_g[l][None, :], conv_w_in[l].astype(bf16),
                conv_b_in[l][None, :], wdw, conv_b_dw[l].reshape(CG, 128), conv_ln_g[l][None, :],
                conv_ln_b[l][None, :], conv_w_out[l].astype(bf16), norm2_g[l][None, :], rw, rb)
        else:
            j = l - N_A
            outs = _preattn(
                x, moe_in, mod[l], norm1_g[l][None, :], w_dq[j].astype(bf16),
                q_norm_g[j][None, :], w_uq[j].T.astype(bf16), cos_t, sin_t,
                kv=kv_w if j == 0 else None)
            if j == 0:
                x, qt, kc, vt = outs
            else:
                x, qt = outs
            o = _attention(qt, kc, vt)
            x, h2p, eidx, ew = _postattn(
                x, o, mod[l], w_o[j].astype(bf16), norm2_g[l][None, :], rw, rb)
        yg = _moe(l, h2p, eidx, exp_w_gate, exp_w_up, exp_w_down)
        moe_in = (yg, yg, ew, mod[l])
    return _final(x, moe_in, final_g[None, :])
```

```python
import functools
import math

import jax
import jax.numpy as jnp
from jax import lax
from jax.experimental import pallas as pl
from jax.experimental.pallas import tpu as pltpu

f32 = jnp.float32
bf16 = jnp.bfloat16
u32 = jnp.uint32
i32 = jnp.int32

D = 1024
B = 4
S = 4096
T = B * S
DEPTH = 4
N_A = 2
CONV_W = 31
C = 1024
H = 8
DN = 128
DR = 64
DV = 128
DQK = DN + DR
QR = 384
KVR = 256
ROPE_THETA = 10000.0
ATTN_SCALE = 1.0 / math.sqrt(DN + DR)
QSCALE = ATTN_SCALE * math.log2(math.e)
E = 16
NG = 4
EG = 4
TOPK = 2
DE = 512
EPS = 1e-6
NEG = -1e30

TM = 512
NT = S // TM
HALO = 32
CG = C // 128
OB = 8
TH = 256
TP = 512
TQ = 512
TK = 512
RB = 1024
N_ROWS = T * TOPK + E * RB
NB = N_ROWS // RB
GR = 2048
DU = 8
HP = D // 2
PG = HP // 128

VMEM_LIMIT = 56 * 1024 * 1024


def _cp(n_axes):
    return pltpu.CompilerParams(
        dimension_semantics=("arbitrary",) * n_axes, vmem_limit_bytes=VMEM_LIMIT)


def _full(shape):
    n = len(shape)
    return pl.BlockSpec(shape, lambda *_: (0,) * n)


def _rms(x, g):
    return x * lax.rsqrt(jnp.mean(x * x, axis=-1, keepdims=True) + EPS) * g


def _sigmoid(x):
    return 0.5 * jnp.tanh(0.5 * x) + 0.5


def _pack_pair(a, b):
    ua = lax.bitcast_convert_type(a.astype(bf16).astype(f32), u32)
    ub = lax.bitcast_convert_type(b.astype(bf16).astype(f32), u32)
    return (ua >> 16) | ub


def _unpack_pair(w):
    a = lax.bitcast_convert_type(w << 16, f32)
    b = lax.bitcast_convert_type(w & jnp.uint32(0xFFFF0000), f32)
    return a, b


def _unpack_rows(w):
    a, b = _unpack_pair(w)
    return jnp.concatenate([a, b], axis=1)


def _load_rows(ref, rows, row0=0):
    return jnp.concatenate(
        [ref[pl.ds(row0 * PG + k, rows, stride=PG), :] for k in range(PG)], axis=1)


def _store_rows(ref, w, rows, row0=0):
    for k in range(PG):
        ref[pl.ds(row0 * PG + k, rows, stride=PG), :] = w[:, 128 * k:128 * (k + 1)]


def _dot_nt(a, b):
    return lax.dot_general(a, b, (((1,), (1,)), ((), ())), preferred_element_type=f32)


def _top2sum4(a, b, c, d):
    lo1, hi1 = jnp.minimum(a, b), jnp.maximum(a, b)
    lo2, hi2 = jnp.minimum(c, d), jnp.maximum(c, d)
    return jnp.maximum(hi1, hi2) + jnp.maximum(jnp.minimum(hi1, hi2), jnp.maximum(lo1, lo2))


def _route_and_pack(x_new, mod_ref, n2g_ref, rw_ref, rb_ref, h2p_ref, eidx_ref, ewt_ref,
                    row0=0):
    n = x_new.shape[0]
    sh2 = mod_ref[0, 3:4, :]
    sc2 = mod_ref[0, 4:5, :]
    h2 = _rms(x_new, n2g_ref[...]) * (1.0 + sc2) + sh2
    _store_rows(h2p_ref, _pack_pair(h2[:, :HP], h2[:, HP:]), n, row0)

    hi = h2.astype(bf16)
    lo = (h2 - hi.astype(f32)).astype(bf16)
    rw = rw_ref[...]
    p = _dot_nt(rw, hi) + _dot_nt(rw, lo)
    logits = p[:E] + p[E:]
    s = jax.nn.sigmoid(logits)
    sb = s + rb_ref[...]

    srow = [s[e:e + 1] for e in range(E)]
    brow = [sb[e:e + 1] for e in range(E)]
    gs = [_top2sum4(*brow[EG * g:EG * g + EG]) for g in range(NG)]
    gidx = jnp.zeros_like(gs[0], dtype=i32)
    best = gs[0]
    for g in range(1, NG):
        upd = gs[g] > best
        gidx = jnp.where(upd, g, gidx)
        best = jnp.where(upd, gs[g], best)

    def pick(rows, i):
        out = rows[i]
        for g in range(1, NG):
            out = jnp.where(gidx == g, rows[EG * g + i], out)
        return out

    wv = [pick(brow, i) for i in range(EG)]
    sv = [pick(srow, i) for i in range(EG)]

    l1 = jnp.zeros_like(gidx)
    b1 = wv[0]
    for i in range(1, EG):
        upd = wv[i] > b1
        l1 = jnp.where(upd, i, l1)
        b1 = jnp.where(upd, wv[i], b1)
    l2 = jnp.zeros_like(gidx)
    b2 = jnp.where(l1 == 0, -jnp.inf, wv[0])
    for i in range(1, EG):
        cand = jnp.where(l1 == i, -jnp.inf, wv[i])
        upd = cand > b2
        l2 = jnp.where(upd, i, l2)
        b2 = jnp.where(upd, cand, b2)

    def sel(loc):
        out = sv[0]
        for i in range(1, EG):
            out = jnp.where(loc == i, sv[i], out)
        return out

    w1 = sel(l1)
    w2 = sel(l2)
    tot = w1 + w2
    eidx_ref[:, row0:row0 + n] = jnp.concatenate([gidx * EG + l1, gidx * EG + l2], axis=0)
    wt = jnp.concatenate([w1 / tot, w2 / tot, jnp.zeros((128 - TOPK, n), f32)], axis=0)
    ewt_ref[row0:row0 + n, :] = wt.T


def _moe_combine(x_ref, yg0_ref, yg1_ref, ewt_ref, modp_ref, row0=0, rows=TM):
    y0 = _unpack_rows(_load_rows(yg0_ref, rows, row0))
    y1 = _unpack_rows(_load_rows(yg1_ref, rows, row0))
    w = ewt_ref[row0:row0 + rows, :]
    g2 = modp_ref[0, 5:6, :]
    return x_ref[0, row0:row0 + rows, :] + g2 * (w[:, 0:1] * y0 + w[:, 1:2] * y1)


def _ada_kernel(c_ref, w_ref, b_ref, o_ref):
    c = c_ref[...]
    ca = (c * jax.nn.sigmoid(c)).astype(bf16)
    o_ref[0] = jnp.dot(ca, w_ref[0].astype(bf16), preferred_element_type=f32) + b_ref[0]


def _ada(c8, ada_w, ada_b3):
    tn = 1536
    return pl.pallas_call(
        _ada_kernel,
        grid=(DEPTH, 6 * D // tn),
        in_specs=[
            pl.BlockSpec((8, D), lambda l, n: (0, 0)),
            pl.BlockSpec((1, D, tn), lambda l, n: (l, 0, n)),
            pl.BlockSpec((1, 1, tn), lambda l, n: (l, 0, n)),
        ],
        out_specs=pl.BlockSpec((1, 8, tn), lambda l, n: (l, 0, n)),
        out_shape=jax.ShapeDtypeStruct((DEPTH, 8, 6 * D), f32),
        compiler_params=_cp(2),
        name="ada",
    )(c8, ada_w, ada_b3)


def _rope_kernel(pos_ref, invf_ref, cos_ref, sin_ref):
    ang = pos_ref[0].astype(f32) * invf_ref[...]
    cos_ref[0] = jnp.cos(ang)
    sin_ref[0] = jnp.sin(ang)


def _rope_tables(pos3, invf):
    return pl.pallas_call(
        _rope_kernel,
        grid=(B,),
        in_specs=[pl.BlockSpec((1, 1, S), lambda b: (b, 0, 0)), _full((DR // 2, 1))],
        out_specs=[pl.BlockSpec((1, DR // 2, S), lambda b: (b, 0, 0))] * 2,
        out_shape=[jax.ShapeDtypeStruct((B, DR // 2, S), f32)] * 2,
        compiler_params=_cp(1),
        name="rope",
    )(pos3, invf)


def _conv_kernel(has_moe_in, *refs):
    if has_moe_in:
        x_ref, yg0_ref, yg1_ref, ewp_ref, modp_ref = refs[:5]
        refs = refs[5:]
    else:
        x_ref = refs[0]
        refs = refs[1:]
    (mod_ref, n1g_ref, win_ref, bin_ref, wdw_ref, bdw_ref, lng_ref, lnb_ref, wout_ref,
     n2g_ref, rw_ref, rb_ref,
     xo_ref, h2p_ref, eidx_ref, ewt_ref, ubuf, obuf) = refs
    si = pl.program_id(1)
    sh1 = mod_ref[0, 0:1, :]
    sc1 = mod_ref[0, 1:2, :]
    g1 = mod_ref[0, 2:3, :]

    @pl.when(si == 0)
    def _():
        ubuf[0:HALO * CG, :] = jnp.zeros((HALO * CG, 128), f32)

    @pl.when(si > 0)
    def _():
        ubuf[0:HALO * CG, :] = ubuf[TM * CG:(TM + HALO) * CG, :]

    def front(r0):
        if has_moe_in:
            x = _moe_combine(x_ref, yg0_ref, yg1_ref, ewp_ref, modp_ref, r0, TH)
        else:
            x = x_ref[0, r0:r0 + TH, :]
        xo_ref[0, r0:r0 + TH, :] = x
        h = (_rms(x, n1g_ref[...]) * (1.0 + sc1) + sh1).astype(bf16)
        u = jnp.dot(h, win_ref[...], preferred_element_type=f32) + bin_ref[...]
        glu = u[:, :C] * _sigmoid(u[:, C:])
        for k in range(CG):
            ubuf[pl.ds((HALO + r0) * CG + k, TH, stride=CG), :] = glu[:, 128 * k:128 * (k + 1)]

    off = HALO - (CONV_W - 1)
    taps = [wdw_ref[j] for j in range(CONV_W)]
    bias = bdw_ref[...]

    def conv(r0):
        for t0 in range(r0, r0 + TH, OB):
            acc = [bias] * OB
            for dd in range(OB + CONV_W - 1):
                row = (t0 + off + dd) * CG
                uv = ubuf[row:row + CG, :]
                for o in range(OB):
                    j = dd - o
                    if 0 <= j < CONV_W:
                        acc[o] = acc[o] + uv * taps[j]
            for o in range(OB):
                obuf[(t0 + o) * CG:(t0 + o + 1) * CG, :] = acc[o]

    def back(r0):
        v = jnp.concatenate(
            [obuf[pl.ds(r0 * CG + k, TH, stride=CG), :] for k in range(CG)], axis=1)
        mu = jnp.mean(v, axis=-1, keepdims=True)
        vc = v - mu
        var = jnp.mean(vc * vc, axis=-1, keepdims=True)
        y = vc * lax.rsqrt(var + EPS) * lng_ref[...] + lnb_ref[...]
        y = (y * _sigmoid(y)).astype(bf16)
        mix = jnp.dot(y, wout_ref[...], preferred_element_type=f32)
        x_new = xo_ref[0, r0:r0 + TH, :] + g1 * mix
        xo_ref[0, r0:r0 + TH, :] = x_new
        _route_and_pack(x_new, mod_ref, n2g_ref, rw_ref, rb_ref, h2p_ref, eidx_ref, ewt_ref, r0)

    groups = list(range(0, TM, TH))
    front(groups[0])
    for n, r0 in enumerate(groups):
        if n + 1 < len(groups):
            front(groups[n + 1])
        conv(r0)
        if n > 0:
            back(groups[n - 1])
    back(groups[-1])


def _tok_specs():
    specs = [
        pl.BlockSpec((1, TM, D), lambda b, s: (b, s, 0)),
        pl.BlockSpec((TM * PG, 128), lambda b, s: (b * NT + s, 0)),
        pl.BlockSpec((TOPK, TM), lambda b, s: (0, b * NT + s)),
        pl.BlockSpec((TM, 128), lambda b, s: (b * NT + s, 0)),
    ]
    shapes = [
        jax.ShapeDtypeStruct((B, S, D), f32),
        jax.ShapeDtypeStruct((T * PG, 128), u32),
        jax.ShapeDtypeStruct((TOPK, T), i32),
        jax.ShapeDtypeStruct((T, 128), f32),
    ]
    return specs, shapes


def _moe_in_specs():
    return [
        pl.BlockSpec((TM * PG, 128), lambda b, s: (b * NT + s, 0)),
        pl.BlockSpec((TM * PG, 128), lambda b, s: (T // TM + b * NT + s, 0)),
        pl.BlockSpec((TM, 128), lambda b, s: (b * NT + s, 0)),
        pl.BlockSpec((1, 6, D), lambda b, s: (b, 0, 0)),
    ]


def _conv_layer(x, moe_in, mod_l, n1g, win, b_in, wdw, bdw, lng, lnb, wout, n2g, rw, rb):
    has_moe_in = moe_in is not None
    in_specs = [pl.BlockSpec((1, TM, D), lambda b, s: (b, s, 0))]
    args = [x]
    if has_moe_in:
        in_specs += _moe_in_specs()
        args += list(moe_in)
    in_specs += [
        pl.BlockSpec((1, 6, D), lambda b, s: (b, 0, 0)),
        _full((1, D)), _full((D, 2 * C)), _full((1, 2 * C)), _full((HALO, CG, 128)),
        _full((CG, 128)),
        _full((1, C)), _full((1, C)), _full((C, D)), _full((1, D)), _full((2 * E, D)),
        _full((E, 1)),
    ]
    args += [mod_l, n1g, win, b_in, wdw, bdw, lng, lnb, wout, n2g, rw, rb]
    out_specs, out_shape = _tok_specs()
    return pl.pallas_call(
        functools.partial(_conv_kernel, has_moe_in),
        grid=(B, NT),
        in_specs=in_specs,
        out_specs=out_specs,
        out_shape=out_shape,
        scratch_shapes=[pltpu.VMEM(((TM + HALO) * CG, 128), f32),
                        pltpu.VMEM((TM * CG, 128), f32)],
        compiler_params=_cp(2),
        name="conv_layer",
    )(*args)


def _preattn_kernel(with_kv, *refs):
    (x_ref, yg0_ref, yg1_ref, ewp_ref, modp_ref, mod_ref, n1g_ref, wdq_ref, qng_ref, wuqt_ref,
     cos_ref, sin_ref) = refs[:12]
    refs = refs[12:]
    if with_kv:
        kvg_ref, wlat_ref, wropet_ref, kvng_ref, wuk_ref, wuvt_ref = refs[:6]
        refs = refs[6:]
        xo_ref, qt_ref, kc_ref, vt_ref = refs
    else:
        xo_ref, qt_ref = refs

    sh1 = mod_ref[0, 0:1, :]
    sc1 = mod_ref[0, 1:2, :]
    hr = DR // 2
    for t0 in range(0, TM, TP):
        ts = slice(t0, t0 + TP)
        x = _moe_combine(x_ref, yg0_ref, yg1_ref, ewp_ref, modp_ref, t0, TP)
        xo_ref[0, ts, :] = x
        cos = cos_ref[0, :, ts]
        sin = sin_ref[0, :, ts]
        h = (_rms(x, n1g_ref[...]) * (1.0 + sc1) + sh1).astype(bf16)
        cq = jnp.dot(h, wdq_ref[...], preferred_element_type=f32)
        cq = _rms(cq, qng_ref[...]).astype(bf16)
        qt = _dot_nt(wuqt_ref[...], cq)
        for hh in range(H):
            r0 = hh * DQK
            x1 = qt[r0 + DN:r0 + DN + hr]
            x2 = qt[r0 + DN + hr:r0 + DQK]
            qt_ref[0, hh, 0:DN, ts] = (qt[r0:r0 + DN] * QSCALE).astype(bf16)
            qt_ref[0, hh, DN:DN + hr, ts] = ((x1 * cos - x2 * sin) * QSCALE).astype(bf16)
            qt_ref[0, hh, DN + hr:DQK, ts] = ((x1 * sin + x2 * cos) * QSCALE).astype(bf16)

        if with_kv:
            xn = _rms(x, kvg_ref[...]).astype(bf16)
            lat = jnp.dot(xn, wlat_ref[...], preferred_element_type=f32)
            latn = _rms(lat, kvng_ref[...]).astype(bf16)
            krt = _dot_nt(wropet_ref[...], xn)
            k1 = krt[:hr]
            k2 = krt[hr:]
            kr = jnp.concatenate([k1 * cos - k2 * sin, k1 * sin + k2 * cos], axis=0)
            kr = kr.T.astype(bf16)
            kn = jnp.dot(latn, wuk_ref[...], preferred_element_type=f32).astype(bf16)
            vt = _dot_nt(wuvt_ref[...], latn).astype(bf16)
            for hh in range(H):
                kc_ref[0, hh, ts, 0:DN] = kn[:, hh * DN:(hh + 1) * DN]
                kc_ref[0, hh, ts, DN:DQK] = kr
                vt_ref[0, hh, :, ts] = vt[hh * DV:(hh + 1) * DV]


def _preattn(x, moe_in, mod_l, n1g, wdq, qng, wuqt, cos_t, sin_t, kv=None):
    with_kv = kv is not None
    in_specs = [pl.BlockSpec((1, TM, D), lambda b, s: (b, s, 0))] + _moe_in_specs() + [
        pl.BlockSpec((1, 6, D), lambda b, s: (b, 0, 0)),
        _full((1, D)), _full((D, QR)), _full((1, QR)), _full((H * DQK, QR)),
        pl.BlockSpec((1, DR // 2, TM), lambda b, s: (b, 0, s)),
        pl.BlockSpec((1, DR // 2, TM), lambda b, s: (b, 0, s)),
    ]
    args = [x, *moe_in, mod_l, n1g, wdq, qng, wuqt, cos_t, sin_t]
    out_specs = [
        pl.BlockSpec((1, TM, D), lambda b, s: (b, s, 0)),
        pl.BlockSpec((1, H, DQK, TM), lambda b, s: (b, 0, 0, s)),
    ]
    out_shape = [
        jax.ShapeDtypeStruct((B, S, D), f32),
        jax.ShapeDtypeStruct((B, H, DQK, S), bf16),
    ]
    if with_kv:
        in_specs += [_full((1, D)), _full((D, KVR)), _full((DR, D)), _full((1, KVR)),
                     _full((KVR, H * DN)), _full((H * DV, KVR))]
        args += list(kv)
        out_specs += [
            pl.BlockSpec((1, H, TM, DQK), lambda b, s: (b, 0, s, 0)),
            pl.BlockSpec((1, H, DV, TM), lambda b, s: (b, 0, 0, s)),
        ]
        out_shape += [
            jax.ShapeDtypeStruct((B, H, S, DQK), bf16),
            jax.ShapeDtypeStruct((B, H, DV, S), bf16),
        ]
    return pl.pallas_call(
        functools.partial(_preattn_kernel, with_kv),
        grid=(B, NT),
        in_specs=in_specs,
        out_specs=out_specs,
        out_shape=out_shape,
        compiler_params=_cp(2),
        name="preattn_kv" if with_kv else "preattn",
    )(*args)


def _attn_kernel(qt_ref, kc_ref, vt_ref, o_ref, m_ref, l_ref, acc_ref, s_ref):
    assert TQ == TK
    pairs = [(qi, j) for qi in range(S // TQ) for j in range(qi + 1)]

    def scores(qi, j, slot):
        q = qt_ref[0, 0, :, qi * TQ:(qi + 1) * TQ]
        k = kc_ref[0, 0, j * TK:(j + 1) * TK, :]
        s_ref[slot] = jnp.dot(k, q, preferred_element_type=f32)

    def softmax_pv(qi, j, slot):
        s = s_ref[slot]
        if j == qi:
            keep = (lax.broadcasted_iota(i32, (TK, TQ), 0)
                    <= lax.broadcasted_iota(i32, (TK, TQ), 1))
            s = jnp.where(keep, s, NEG)
        v = vt_ref[0, 0, :, j * TK:(j + 1) * TK]
        if j == 0:
            m_new = jnp.max(s, axis=0, keepdims=True)
            p = jnp.exp2(s - m_new)
            l_ref[...] = jnp.sum(p, axis=0, keepdims=True)
            acc_ref[...] = jnp.dot(v, p.astype(bf16), preferred_element_type=f32)
        else:
            m_old = m_ref[...]
            m_new = jnp.maximum(m_old, jnp.max(s, axis=0, keepdims=True))
            alpha = jnp.exp2(m_old - m_new)
            p = jnp.exp2(s - m_new)
            l_ref[...] = alpha * l_ref[...] + jnp.sum(p, axis=0, keepdims=True)
            acc_ref[...] = alpha * acc_ref[...] + jnp.dot(
                v, p.astype(bf16), preferred_element_type=f32)
        m_ref[...] = m_new
        if j == qi:
            o = acc_ref[...] / l_ref[...]
            o_ref[0, qi * TQ:(qi + 1) * TQ, :] = o.T.astype(bf16)

    scores(*pairs[0], 0)
    for n, (qi, j) in enumerate(pairs):
        if n + 1 < len(pairs):
            scores(*pairs[n + 1], (n + 1) % 2)
        softmax_pv(qi, j, n % 2)


def _attention(qt, kc, vt):
    return pl.pallas_call(
        _attn_kernel,
        grid=(B, H),
        in_specs=[
            pl.BlockSpec((1, 1, DQK, S), lambda b, h: (b, h, 0, 0)),
            pl.BlockSpec((1, 1, S, DQK), lambda b, h: (b, h, 0, 0)),
            pl.BlockSpec((1, 1, DV, S), lambda b, h: (b, h, 0, 0)),
        ],
        out_specs=pl.BlockSpec((1, S, DV), lambda b, h: (b, 0, h)),
        out_shape=jax.ShapeDtypeStruct((B, S, H * DV), bf16),
        scratch_shapes=[pltpu.VMEM((1, TQ), f32), pltpu.VMEM((1, TQ), f32),
                        pltpu.VMEM((DV, TQ), f32), pltpu.VMEM((2, TK, TQ), f32)],
        compiler_params=_cp(2),
        name="attn",
    )(qt, kc, vt)


def _postattn_kernel(x_ref, o_ref, mod_ref, wo_ref, n2g_ref, rw_ref, rb_ref,
                     xo_ref, h2p_ref, eidx_ref, ewt_ref):
    g1 = mod_ref[0, 2:3, :]
    mix = jnp.dot(o_ref[0], wo_ref[...], preferred_element_type=f32)
    x_new = x_ref[0] + g1 * mix
    xo_ref[0] = x_new
    _route_and_pack(x_new, mod_ref, n2g_ref, rw_ref, rb_ref, h2p_ref, eidx_ref, ewt_ref)


def _postattn(x, o, mod_l, wo, n2g, rw, rb):
    out_specs, out_shape = _tok_specs()
    return pl.pallas_call(
        _postattn_kernel,
        grid=(B, NT),
        in_specs=[
            pl.BlockSpec((1, TM, D), lambda b, s: (b, s, 0)),
            pl.BlockSpec((1, TM, H * DV), lambda b, s: (b, s, 0)),
            pl.BlockSpec((1, 6, D), lambda b, s: (b, 0, 0)),
            _full((H * DV, D)), _full((1, D)), _full((2 * E, D)), _full((E, 1)),
        ],
        out_specs=out_specs,
        out_shape=out_shape,
        compiler_params=_cp(2),
        name="postattn",
    )(x, o, mod_l, wo, n2g, rw, rb)


def _row_copy(src_ref, dst_ref, sem, src_row, dst_row):
    return pltpu.make_async_copy(src_ref.at[src_row], dst_ref.at[dst_row], sem)


def _gather_kernel(idx_ref, src_ref, dst_ref, sem):
    def issue(g, carry):
        r0 = g * DU
        for u in range(DU):
            _row_copy(src_ref, dst_ref, sem, idx_ref[r0 + u], r0 + u).start(priority=u % 2)
        return carry

    lax.fori_loop(0, GR // DU, issue, 0)

    def drain(r, carry):
        _row_copy(src_ref, dst_ref, sem, 0, r).wait()
        return carry

    lax.fori_loop(0, GR, drain, 0, unroll=8)


def _gather_rows(src, idx):
    n = idx.shape[0]
    return pl.pallas_call(
        _gather_kernel,
        grid=(n // GR,),
        in_specs=[
            pl.BlockSpec((GR,), lambda i: (i,), memory_space=pltpu.SMEM),
            pl.BlockSpec(memory_space=pl.ANY),
        ],
        out_specs=pl.BlockSpec((GR, PG, 128), lambda i: (i, 0, 0)),
        out_shape=jax.ShapeDtypeStruct((n, PG, 128), src.dtype),
        scratch_shapes=[pltpu.SemaphoreType.DMA],
        compiler_params=_cp(1),
        name="gather_rows",
    )(idx, src)


def _expert_kernel(be_ref, first_ref, nv_ref, xb_ref, wg_ref, wu_ref, wd_ref, yb_ref,
                   wgu_bf, wd_bf):
    i = pl.program_id(0)
    nv = nv_ref[i]

    @pl.when(first_ref[i] == 1)
    def _():
        wgu_bf[:, :DE] = wg_ref[0, 0].astype(bf16)
        wgu_bf[:, DE:] = wu_ref[0, 0].astype(bf16)
        wd_bf[...] = wd_ref[0, 0].astype(bf16)

    @pl.when(nv > 0)
    def _():
        x = _unpack_rows(_load_rows(xb_ref, RB)).astype(bf16)
        gu = jnp.dot(x, wgu_bf[...], preferred_element_type=f32)
        g = gu[:, :DE]
        hmid = (g * _sigmoid(g) * gu[:, DE:]).astype(bf16)
        y = jnp.dot(hmid, wd_bf[...], preferred_element_type=f32)
        _store_rows(yb_ref, _pack_pair(y[:, :HP], y[:, HP:]), RB)

    @pl.when(nv == 0)
    def _():
        yb_ref[...] = jnp.zeros((RB * PG, 128), u32)


def _experts(layer, blk_e, blk_first, blk_nv, xb, wg, wu, wd):
    grid_spec = pltpu.PrefetchScalarGridSpec(
        num_scalar_prefetch=3,
        grid=(NB,),
        in_specs=[
            pl.BlockSpec((RB * PG, 128), lambda i, be, bf, nv: (i, 0)),
            pl.BlockSpec((1, 1, D, DE), lambda i, be, bf, nv: (layer, be[i], 0, 0)),
            pl.BlockSpec((1, 1, D, DE), lambda i, be, bf, nv: (layer, be[i], 0, 0)),
            pl.BlockSpec((1, 1, DE, D), lambda i, be, bf, nv: (layer, be[i], 0, 0)),
        ],
        out_specs=pl.BlockSpec((RB * PG, 128), lambda i, be, bf, nv: (i, 0)),
        scratch_shapes=[pltpu.VMEM((D, 2 * DE), bf16), pltpu.VMEM((DE, D), bf16)],
    )
    return pl.pallas_call(
        _expert_kernel,
        grid_spec=grid_spec,
        out_shape=jax.ShapeDtypeStruct((N_ROWS * PG, 128), u32),
        compiler_params=_cp(1),
        name="experts",
    )(blk_e, blk_first, blk_nv, xb, wg, wu, wd)


PCH = 256
NBP = 256


def _route_pos_kernel(eidx_ref, dest_ref, blk_ref, pre_ref):
    upper = (lax.broadcasted_iota(i32, (PCH, PCH), 0)
             < lax.broadcasted_iota(i32, (PCH, PCH), 1)).astype(bf16)
    eio = lax.broadcasted_iota(i32, (E, PCH), 0)

    def onehot(c0, width, eiota):
        e0 = eidx_ref[0:1, pl.ds(c0, width)]
        e1 = eidx_ref[1:2, pl.ds(c0, width)]
        return jnp.concatenate([e0 == eiota, e1 == eiota], axis=0)

    ones_r = jnp.ones((8, PCH), bf16)

    def count(c, carry):
        col, row = carry
        c0 = pl.multiple_of(c * PCH, PCH)
        oh = onehot(c0, PCH, eio).astype(bf16)
        pre_ref[:, pl.ds(c0, PCH)] = col + jnp.dot(oh, upper, preferred_element_type=f32)
        col = col + jnp.sum(oh.astype(f32), axis=1, keepdims=True)
        row = row + _dot_nt(ones_r, oh[:E]) + _dot_nt(ones_r, oh[E:])
        return col, row

    tot_c, tot_r = lax.fori_loop(
        0, T // PCH, count, (jnp.zeros((2 * E, 1), f32), jnp.zeros((8, E), f32)))
    cnt0_c = tot_c[:E]
    cnt_c = tot_c[:E] + tot_c[E:]
    shift = RB.bit_length() - 1
    assert RB == 1 << shift
    padded_r = lax.shift_left(
        lax.shift_right_logical(tot_r[0:1].astype(i32) + (RB - 1), shift), shift).astype(f32)
    ee = lax.broadcasted_iota(i32, (E, E), 0)
    ep = lax.broadcasted_iota(i32, (E, E), 1)
    start_col = jnp.sum(jnp.where(ep < ee, padded_r, 0.0), axis=1, keepdims=True)
    end_col = jnp.sum(jnp.where(ep <= ee, padded_r, 0.0), axis=1, keepdims=True)
    vend_col = start_col + cnt_c
    basef = jnp.concatenate([start_col, start_col + cnt0_c], axis=0)

    wide = 2048
    eiow = lax.broadcasted_iota(i32, (E, wide), 0)

    def place(c, carry):
        c0 = pl.multiple_of(c * wide, wide)
        oh = onehot(c0, wide, eiow)
        val = jnp.where(oh, pre_ref[:, pl.ds(c0, wide)] + basef, 0.0)
        d0 = jnp.sum(val[:E], axis=0, keepdims=True)
        d1 = jnp.sum(val[E:], axis=0, keepdims=True)
        dest_ref[0:1, pl.ds(c0, wide)] = d0.astype(i32)
        dest_ref[1:2, pl.ds(c0, wide)] = d1.astype(i32)
        return carry

    lax.fori_loop(0, T // wide, place, 0)

    start_b = (lax.broadcasted_iota(i32, (1, NBP), 1) * RB).astype(f32)
    be = jnp.minimum(jnp.sum((end_col <= start_b).astype(f32), axis=0, keepdims=True), E - 1.0)
    bprev = jnp.minimum(
        jnp.sum((end_col <= start_b - RB).astype(f32), axis=0, keepdims=True), E - 1.0)
    first = jnp.logical_or(start_b == 0.0, be != bprev)
    eion = lax.broadcasted_iota(i32, (E, NBP), 0).astype(f32)
    vend_b = jnp.sum(jnp.where(eion == be, vend_col, 0.0), axis=0, keepdims=True)
    nv = jnp.clip(vend_b - start_b, 0.0, float(RB))
    blk_ref[...] = jnp.zeros((8, NBP), i32)
    blk_ref[0:1, :] = be.astype(i32)
    blk_ref[1:2, :] = first.astype(i32)
    blk_ref[2:3, :] = nv.astype(i32)


def _route_pos(eidx):
    return pl.pallas_call(
        _route_pos_kernel,
        in_specs=[pl.BlockSpec(memory_space=pltpu.VMEM)],
        out_specs=[pl.BlockSpec(memory_space=pltpu.VMEM)] * 2,
        out_shape=[jax.ShapeDtypeStruct((TOPK, T), i32), jax.ShapeDtypeStruct((8, NBP), i32)],
        scratch_shapes=[pltpu.VMEM((2 * E, T), f32)],
        compiler_params=pltpu.CompilerParams(vmem_limit_bytes=VMEM_LIMIT),
        name="route_pos",
    )(eidx)


TD = 1024


def _dispatch_kernel(nv_ref, d0_ref, d1_ref, h2p_ref, xb_ref, zbuf, sem):
    @pl.when(pl.program_id(0) == 0)
    def _():
        zbuf[...] = jnp.zeros((RB, PG, 128), u32)

        def zero_block(b):
            return pltpu.make_async_copy(zbuf, xb_ref.at[pl.ds(b * RB, RB)], sem)

        for b in range(NB):
            @pl.when(nv_ref[b] < RB)
            def _():
                zero_block(b).start()

        for b in range(NB):
            @pl.when(nv_ref[b] < RB)
            def _():
                zero_block(b).wait()

    def issue(g, carry):
        r0 = g * DU
        for u in range(DU):
            _row_copy(h2p_ref, xb_ref, sem, r0 + u, d0_ref[r0 + u]).start(priority=0)
            _row_copy(h2p_ref, xb_ref, sem, r0 + u, d1_ref[r0 + u]).start(priority=1)
        return carry

    lax.fori_loop(0, TD // DU, issue, 0)

    def drain(r, carry):
        _row_copy(h2p_ref, xb_ref, sem, 0, 0).wait()
        _row_copy(h2p_ref, xb_ref, sem, 0, 0).wait()
        return carry

    lax.fori_loop(0, TD, drain, 0, unroll=8)


def _dispatch(h2p, d0, d1, blk_nv):
    return pl.pallas_call(
        _dispatch_kernel,
        grid=(T // TD,),
        in_specs=[
            pl.BlockSpec(memory_space=pltpu.SMEM),
            pl.BlockSpec((TD,), lambda i: (i,), memory_space=pltpu.SMEM),
            pl.BlockSpec((TD,), lambda i: (i,), memory_space=pltpu.SMEM),
            pl.BlockSpec((TD, PG, 128), lambda i: (i, 0, 0)),
        ],
        out_specs=pl.BlockSpec(memory_space=pl.ANY),
        out_shape=jax.ShapeDtypeStruct((N_ROWS, PG, 128), u32),
        scratch_shapes=[pltpu.VMEM((RB, PG, 128), u32), pltpu.SemaphoreType.DMA],
        compiler_params=_cp(1),
        name="dispatch",
    )(blk_nv, d0, d1, h2p)


def _final_kernel(x_ref, yg0_ref, yg1_ref, ewp_ref, modp_ref, g_ref, o_ref):
    o_ref[0] = _rms(_moe_combine(x_ref, yg0_ref, yg1_ref, ewp_ref, modp_ref), g_ref[...])


def _final(x, moe_in, g):
    return pl.pallas_call(
        _final_kernel,
        grid=(B, NT),
        in_specs=[pl.BlockSpec((1, TM, D), lambda b, s: (b, s, 0))] + _moe_in_specs()
        + [_full((1, D))],
        out_specs=pl.BlockSpec((1, TM, D), lambda b, s: (b, s, 0)),
        out_shape=jax.ShapeDtypeStruct((B, S, D), f32),
        compiler_params=_cp(2),
        name="final_norm",
    )(x, *moe_in, g)


def _moe(layer, h2p, eidx, wg, wu, wd):
    dest, blk = _route_pos(eidx)
    blk_e, blk_first, blk_nv = blk[0, :NB], blk[1, :NB], blk[2, :NB]
    xb = _dispatch(h2p.reshape(T, PG, 128), dest[0], dest[1], blk_nv)
    yb = _experts(layer, blk_e, blk_first, blk_nv, xb.reshape(N_ROWS * PG, 128), wg, wu, wd)
    yg = _gather_rows(yb.reshape(N_ROWS, PG, 128), dest.reshape(-1))
    return yg.reshape(TOPK * T * PG, 128)


def kernel(x, c, positions, ada_w, ada_b, norm1_g, norm2_g, conv_w_in, conv_b_in, conv_w_dw,
           conv_b_dw, conv_ln_g, conv_ln_b, conv_w_out, kv_in_g, w_dkv, kv_norm_g, w_ukv, w_dq,
           q_norm_g, w_uq, w_o, router_w, router_b, exp_w_gate, exp_w_up, exp_w_down, final_g):
    c8 = jnp.pad(c, ((0, 8 - B), (0, 0)))
    mod = _ada(c8, ada_w, ada_b[:, None, :])[:, :B].reshape(DEPTH, B, 6, D)

    invf = (ROPE_THETA ** (-jnp.arange(0, DR, 2, dtype=f32) / DR))[:, None]
    cos_t, sin_t = _rope_tables(positions[:, None, :], invf)

    rwt = router_w.astype(f32).T
    rw_hi = rwt.astype(bf16)
    rw_lo = (rwt - rw_hi.astype(f32)).astype(bf16)
    rw = jnp.concatenate([rw_hi, rw_lo], axis=0)
    rb = router_b.astype(f32)[:, None]

    w_ukv3 = w_ukv.reshape(KVR, H, DN + DV)
    wuk = w_ukv3[:, :, :DN].reshape(KVR, H * DN).astype(bf16)
    wuvt = w_ukv3[:, :, DN:].reshape(KVR, H * DV).T.astype(bf16)
    kv_w = (kv_in_g[None, :], w_dkv[:, :KVR].astype(bf16), w_dkv[:, KVR:].T.astype(bf16),
            kv_norm_g[None, :], wuk, wuvt)

    moe_in = None
    kc = vt = None
    for l in range(DEPTH):
        if l < N_A:
            wdw = jnp.pad(conv_w_dw[l], ((0, HALO - CONV_W), (0, 0))).reshape(HALO, CG, 128)
            x, h2p, eidx, ew = _conv_layer(
                x, moe_in, mod[l], norm1_g[l][None, :], conv_w_in[l].astype(bf16),
                conv_b_in[l][None, :], wdw, conv_b_dw[l].reshape(CG, 128), conv_ln_g[l][None, :],
                conv_ln_b[l][None, :], conv_w_out[l].astype(bf16), norm2_g[l][None, :], rw, rb)
        else:
            j = l - N_A
            outs = _preattn(
                x, moe_in, mod[l], norm1_g[l][None, :], w_dq[j].astype(bf16),
                q_norm_g[j][None, :], w_uq[j].T.astype(bf16), cos_t, sin_t,
                kv=kv_w if j == 0 else None)
            if j == 0:
                x, qt, kc, vt = outs
            else:
                x, qt = outs
            o = _attention(qt, kc, vt)
            x, h2p, eidx, ew = _postattn(
                x, o, mod[l], w_o[j].astype(bf16), norm2_g[l][None, :], rw, rb)
        yg = _moe(l, h2p, eidx, exp_w_gate, exp_w_up, exp_w_down)
        moe_in = (yg, yg, ew, mod[l])
    return _final(x, moe_in, final_g[None, :])
```

```python
import functools
import math

import jax
import jax.numpy as jnp
from jax import lax
from jax.experimental import pallas as pl
from jax.experimental.pallas import tpu as pltpu

f32 = jnp.float32
bf16 = jnp.bfloat16
u32 = jnp.uint32
i32 = jnp.int32

D = 1024
B = 4
S = 4096
T = B * S
DEPTH = 4
N_A = 2
CONV_W = 31
C = 1024
H = 8
DN = 128
DR = 64
DV = 128
DQK = DN + DR
QR = 384
KVR = 256
ROPE_THETA = 10000.0
ATTN_SCALE = 1.0 / math.sqrt(DN + DR)
QSCALE = ATTN_SCALE * math.log2(math.e)
E = 16
NG = 4
EG = 4
TOPK = 2
DE = 512
EPS = 1e-6
NEG = -1e30

TM = 512
NT = S // TM
HALO = 32
CG = C // 128
OB = 8
TH = 256
TP = 512
TQ = 512
TK = 512
RB = 1024
N_ROWS = T * TOPK + E * RB
NB = N_ROWS // RB
GR = 4096
DU = 8
HP = D // 2
PG = HP // 128

VMEM_LIMIT = 56 * 1024 * 1024


def _cp(n_axes):
    return pltpu.CompilerParams(
        dimension_semantics=("arbitrary",) * n_axes, vmem_limit_bytes=VMEM_LIMIT)


def _full(shape):
    n = len(shape)
    return pl.BlockSpec(shape, lambda *_: (0,) * n)


def _rms(x, g):
    return x * lax.rsqrt(jnp.mean(x * x, axis=-1, keepdims=True) + EPS) * g


def _sigmoid(x):
    return 0.5 * jnp.tanh(0.5 * x) + 0.5


def _pack_pair(a, b):
    ua = lax.bitcast_convert_type(a.astype(bf16).astype(f32), u32)
    ub = lax.bitcast_convert_type(b.astype(bf16).astype(f32), u32)
    return (ua >> 16) | ub


def _unpack_pair(w):
    a = lax.bitcast_convert_type(w << 16, f32)
    b = lax.bitcast_convert_type(w & jnp.uint32(0xFFFF0000), f32)
    return a, b


def _unpack_rows(w):
    a, b = _unpack_pair(w)
    return jnp.concatenate([a, b], axis=1)


def _load_rows(ref, rows, row0=0):
    return jnp.concatenate(
        [ref[pl.ds(row0 * PG + k, rows, stride=PG), :] for k in range(PG)], axis=1)


def _store_rows(ref, w, rows, row0=0):
    for k in range(PG):
        ref[pl.ds(row0 * PG + k, rows, stride=PG), :] = w[:, 128 * k:128 * (k + 1)]


def _dot_nt(a, b):
    return lax.dot_general(a, b, (((1,), (1,)), ((), ())), preferred_element_type=f32)


def _top2sum4(a, b, c, d):
    lo1, hi1 = jnp.minimum(a, b), jnp.maximum(a, b)
    lo2, hi2 = jnp.minimum(c, d), jnp.maximum(c, d)
    return jnp.maximum(hi1, hi2) + jnp.maximum(jnp.minimum(hi1, hi2), jnp.maximum(lo1, lo2))


def _route_and_pack(x_new, mod_ref, n2g_ref, rw_ref, rb_ref, h2p_ref, eidx_ref, ewt_ref,
                    row0=0):
    n = x_new.shape[0]
    sh2 = mod_ref[0, 3:4, :]
    sc2 = mod_ref[0, 4:5, :]
    h2 = _rms(x_new, n2g_ref[...]) * (1.0 + sc2) + sh2
    _store_rows(h2p_ref, _pack_pair(h2[:, :HP], h2[:, HP:]), n, row0)

    hi = h2.astype(bf16)
    lo = (h2 - hi.astype(f32)).astype(bf16)
    rw = rw_ref[...]
    p = _dot_nt(rw, hi) + _dot_nt(rw, lo)
    logits = p[:E] + p[E:]
    s = jax.nn.sigmoid(logits)
    sb = s + rb_ref[...]

    srow = [s[e:e + 1] for e in range(E)]
    brow = [sb[e:e + 1] for e in range(E)]
    gs = [_top2sum4(*brow[EG * g:EG * g + EG]) for g in range(NG)]
    gidx = jnp.zeros_like(gs[0], dtype=i32)
    best = gs[0]
    for g in range(1, NG):
        upd = gs[g] > best
        gidx = jnp.where(upd, g, gidx)
        best = jnp.where(upd, gs[g], best)

    def pick(rows, i):
        out = rows[i]
        for g in range(1, NG):
            out = jnp.where(gidx == g, rows[EG * g + i], out)
        return out

    wv = [pick(brow, i) for i in range(EG)]
    sv = [pick(srow, i) for i in range(EG)]

    l1 = jnp.zeros_like(gidx)
    b1 = wv[0]
    for i in range(1, EG):
        upd = wv[i] > b1
        l1 = jnp.where(upd, i, l1)
        b1 = jnp.where(upd, wv[i], b1)
    l2 = jnp.zeros_like(gidx)
    b2 = jnp.where(l1 == 0, -jnp.inf, wv[0])
    for i in range(1, EG):
        cand = jnp.where(l1 == i, -jnp.inf, wv[i])
        upd = cand > b2
        l2 = jnp.where(upd, i, l2)
        b2 = jnp.where(upd, cand, b2)

    def sel(loc):
        out = sv[0]
        for i in range(1, EG):
            out = jnp.where(loc == i, sv[i], out)
        return out

    w1 = sel(l1)
    w2 = sel(l2)
    tot = w1 + w2
    eidx_ref[:, row0:row0 + n] = jnp.concatenate([gidx * EG + l1, gidx * EG + l2], axis=0)
    wt = jnp.concatenate([w1 / tot, w2 / tot, jnp.zeros((128 - TOPK, n), f32)], axis=0)
    ewt_ref[row0:row0 + n, :] = wt.T


def _moe_combine(x_ref, yg0_ref, yg1_ref, ewt_ref, modp_ref, row0=0, rows=TM):
    y0 = _unpack_rows(_load_rows(yg0_ref, rows, row0))
    y1 = _unpack_rows(_load_rows(yg1_ref, rows, row0))
    w = ewt_ref[row0:row0 + rows, :]
    g2 = modp_ref[0, 5:6, :]
    return x_ref[0, row0:row0 + rows, :] + g2 * (w[:, 0:1] * y0 + w[:, 1:2] * y1)


def _ada_kernel(c_ref, w_ref, b_ref, o_ref):
    c = c_ref[...]
    ca = (c * jax.nn.sigmoid(c)).astype(bf16)
    o_ref[0] = jnp.dot(ca, w_ref[0].astype(bf16), preferred_element_type=f32) + b_ref[0]


def _ada(c8, ada_w, ada_b3):
    tn = 1536
    return pl.pallas_call(
        _ada_kernel,
        grid=(DEPTH, 6 * D // tn),
        in_specs=[
            pl.BlockSpec((8, D), lambda l, n: (0, 0)),
            pl.BlockSpec((1, D, tn), lambda l, n: (l, 0, n)),
            pl.BlockSpec((1, 1, tn), lambda l, n: (l, 0, n)),
        ],
        out_specs=pl.BlockSpec((1, 8, tn), lambda l, n: (l, 0, n)),
        out_shape=jax.ShapeDtypeStruct((DEPTH, 8, 6 * D), f32),
        compiler_params=_cp(2),
        name="ada",
    )(c8, ada_w, ada_b3)


def _rope_kernel(pos_ref, invf_ref, cos_ref, sin_ref):
    ang = pos_ref[0].astype(f32) * invf_ref[...]
    cos_ref[0] = jnp.cos(ang)
    sin_ref[0] = jnp.sin(ang)


def _rope_tables(pos3, invf):
    return pl.pallas_call(
        _rope_kernel,
        grid=(B,),
        in_specs=[pl.BlockSpec((1, 1, S), lambda b: (b, 0, 0)), _full((DR // 2, 1))],
        out_specs=[pl.BlockSpec((1, DR // 2, S), lambda b: (b, 0, 0))] * 2,
        out_shape=[jax.ShapeDtypeStruct((B, DR // 2, S), f32)] * 2,
        compiler_params=_cp(1),
        name="rope",
    )(pos3, invf)


def _conv_kernel(has_moe_in, *refs):
    if has_moe_in:
        x_ref, yg0_ref, yg1_ref, ewp_ref, modp_ref = refs[:5]
        refs = refs[5:]
    else:
        x_ref = refs[0]
        refs = refs[1:]
    (mod_ref, n1g_ref, win_ref, bin_ref, wdw_ref, bdw_ref, lng_ref, lnb_ref, wout_ref,
     n2g_ref, rw_ref, rb_ref,
     xo_ref, h2p_ref, eidx_ref, ewt_ref, ubuf, obuf) = refs
    si = pl.program_id(1)
    sh1 = mod_ref[0, 0:1, :]
    sc1 = mod_ref[0, 1:2, :]
    g1 = mod_ref[0, 2:3, :]

    @pl.when(si == 0)
    def _():
        ubuf[0:HALO * CG, :] = jnp.zeros((HALO * CG, 128), f32)

    @pl.when(si > 0)
    def _():
        ubuf[0:HALO * CG, :] = ubuf[TM * CG:(TM + HALO) * CG, :]

    def front(r0):
        if has_moe_in:
            x = _moe_combine(x_ref, yg0_ref, yg1_ref, ewp_ref, modp_ref, r0, TH)
        else:
            x = x_ref[0, r0:r0 + TH, :]
        xo_ref[0, r0:r0 + TH, :] = x
        h = (_rms(x, n1g_ref[...]) * (1.0 + sc1) + sh1).astype(bf16)
        u = jnp.dot(h, win_ref[...], preferred_element_type=f32) + bin_ref[...]
        glu = u[:, :C] * _sigmoid(u[:, C:])
        for k in range(CG):
            ubuf[pl.ds((HALO + r0) * CG + k, TH, stride=CG), :] = glu[:, 128 * k:128 * (k + 1)]

    off = HALO - (CONV_W - 1)
    taps = [wdw_ref[j] for j in range(CONV_W)]
    bias = bdw_ref[...]

    def conv(r0):
        for t0 in range(r0, r0 + TH, OB):
            acc = [bias] * OB
            for dd in range(OB + CONV_W - 1):
                row = (t0 + off + dd) * CG
                uv = ubuf[row:row + CG, :]
                for o in range(OB):
                    j = dd - o
                    if 0 <= j < CONV_W:
                        acc[o] = acc[o] + uv * taps[j]
            for o in range(OB):
                obuf[(t0 + o) * CG:(t0 + o + 1) * CG, :] = acc[o]

    def back(r0):
        v = jnp.concatenate(
            [obuf[pl.ds(r0 * CG + k, TH, stride=CG), :] for k in range(CG)], axis=1)
        mu = jnp.mean(v, axis=-1, keepdims=True)
        vc = v - mu
        var = jnp.mean(vc * vc, axis=-1, keepdims=True)
        y = vc * lax.rsqrt(var + EPS) * lng_ref[...] + lnb_ref[...]
        y = (y * _sigmoid(y)).astype(bf16)
        mix = jnp.dot(y, wout_ref[...], preferred_element_type=f32)
        x_new = xo_ref[0, r0:r0 + TH, :] + g1 * mix
        xo_ref[0, r0:r0 + TH, :] = x_new
        _route_and_pack(x_new, mod_ref, n2g_ref, rw_ref, rb_ref, h2p_ref, eidx_ref, ewt_ref, r0)

    groups = list(range(0, TM, TH))
    front(groups[0])
    for n, r0 in enumerate(groups):
        if n + 1 < len(groups):
            front(groups[n + 1])
        conv(r0)
        if n > 0:
            back(groups[n - 1])
    back(groups[-1])


def _tok_specs():
    specs = [
        pl.BlockSpec((1, TM, D), lambda b, s: (b, s, 0)),
        pl.BlockSpec((TM * PG, 128), lambda b, s: (b * NT + s, 0)),
        pl.BlockSpec((TOPK, TM), lambda b, s: (0, b * NT + s)),
        pl.BlockSpec((TM, 128), lambda b, s: (b * NT + s, 0)),
    ]
    shapes = [
        jax.ShapeDtypeStruct((B, S, D), f32),
        jax.ShapeDtypeStruct((T * PG, 128), u32),
        jax.ShapeDtypeStruct((TOPK, T), i32),
        jax.ShapeDtypeStruct((T, 128), f32),
    ]
    return specs, shapes


def _moe_in_specs():
    return [
        pl.BlockSpec((TM * PG, 128), lambda b, s: (b * NT + s, 0)),
        pl.BlockSpec((TM * PG, 128), lambda b, s: (T // TM + b * NT + s, 0)),
        pl.BlockSpec((TM, 128), lambda b, s: (b * NT + s, 0)),
        pl.BlockSpec((1, 6, D), lambda b, s: (b, 0, 0)),
    ]


def _conv_layer(x, moe_in, mod_l, n1g, win, b_in, wdw, bdw, lng, lnb, wout, n2g, rw, rb):
    has_moe_in = moe_in is not None
    in_specs = [pl.BlockSpec((1, TM, D), lambda b, s: (b, s, 0))]
    args = [x]
    if has_moe_in:
        in_specs += _moe_in_specs()
        args += list(moe_in)
    in_specs += [
        pl.BlockSpec((1, 6, D), lambda b, s: (b, 0, 0)),
        _full((1, D)), _full((D, 2 * C)), _full((1, 2 * C)), _full((HALO, CG, 128)),
        _full((CG, 128)),
        _full((1, C)), _full((1, C)), _full((C, D)), _full((1, D)), _full((2 * E, D)),
        _full((E, 1)),
    ]
    args += [mod_l, n1g, win, b_in, wdw, bdw, lng, lnb, wout, n2g, rw, rb]
    out_specs, out_shape = _tok_specs()
    return pl.pallas_call(
        functools.partial(_conv_kernel, has_moe_in),
        grid=(B, NT),
        in_specs=in_specs,
        out_specs=out_specs,
        out_shape=out_shape,
        scratch_shapes=[pltpu.VMEM(((TM + HALO) * CG, 128), f32),
                        pltpu.VMEM((TM * CG, 128), f32)],
        compiler_params=_cp(2),
        name="conv_layer",
    )(*args)


def _preattn_kernel(with_kv, *refs):
    (x_ref, yg0_ref, yg1_ref, ewp_ref, modp_ref, mod_ref, n1g_ref, wdq_ref, qng_ref, wuqt_ref,
     cos_ref, sin_ref) = refs[:12]
    refs = refs[12:]
    if with_kv:
        kvg_ref, wlat_ref, wropet_ref, kvng_ref, wuk_ref, wuvt_ref = refs[:6]
        refs = refs[6:]
        xo_ref, qt_ref, kc_ref, vt_ref = refs
    else:
        xo_ref, qt_ref = refs

    sh1 = mod_ref[0, 0:1, :]
    sc1 = mod_ref[0, 1:2, :]
    hr = DR // 2
    for t0 in range(0, TM, TP):
        ts = slice(t0, t0 + TP)
        x = _moe_combine(x_ref, yg0_ref, yg1_ref, ewp_ref, modp_ref, t0, TP)
        xo_ref[0, ts, :] = x
        cos = cos_ref[0, :, ts]
        sin = sin_ref[0, :, ts]
        h = (_rms(x, n1g_ref[...]) * (1.0 + sc1) + sh1).astype(bf16)
        cq = jnp.dot(h, wdq_ref[...], preferred_element_type=f32)
        cq = _rms(cq, qng_ref[...]).astype(bf16)
        qt = _dot_nt(wuqt_ref[...], cq)
        for hh in range(H):
            r0 = hh * DQK
            x1 = qt[r0 + DN:r0 + DN + hr]
            x2 = qt[r0 + DN + hr:r0 + DQK]
            qt_ref[0, hh, 0:DN, ts] = (qt[r0:r0 + DN] * QSCALE).astype(bf16)
            qt_ref[0, hh, DN:DN + hr, ts] = ((x1 * cos - x2 * sin) * QSCALE).astype(bf16)
            qt_ref[0, hh, DN + hr:DQK, ts] = ((x1 * sin + x2 * cos) * QSCALE).astype(bf16)

        if with_kv:
            xn = _rms(x, kvg_ref[...]).astype(bf16)
            lat = jnp.dot(xn, wlat_ref[...], preferred_element_type=f32)
            latn = _rms(lat, kvng_ref[...]).astype(bf16)
            krt = _dot_nt(wropet_ref[...], xn)
            k1 = krt[:hr]
            k2 = krt[hr:]
            kr = jnp.concatenate([k1 * cos - k2 * sin, k1 * sin + k2 * cos], axis=0)
            kr = kr.T.astype(bf16)
            kn = jnp.dot(latn, wuk_ref[...], preferred_element_type=f32).astype(bf16)
            vt = _dot_nt(wuvt_ref[...], latn).astype(bf16)
            for hh in range(H):
                kc_ref[0, hh, ts, 0:DN] = kn[:, hh * DN:(hh + 1) * DN]
                kc_ref[0, hh, ts, DN:DQK] = kr
                vt_ref[0, hh, :, ts] = vt[hh * DV:(hh + 1) * DV]


def _preattn(x, moe_in, mod_l, n1g, wdq, qng, wuqt, cos_t, sin_t, kv=None):
    with_kv = kv is not None
    in_specs = [pl.BlockSpec((1, TM, D), lambda b, s: (b, s, 0))] + _moe_in_specs() + [
        pl.BlockSpec((1, 6, D), lambda b, s: (b, 0, 0)),
        _full((1, D)), _full((D, QR)), _full((1, QR)), _full((H * DQK, QR)),
        pl.BlockSpec((1, DR // 2, TM), lambda b, s: (b, 0, s)),
        pl.BlockSpec((1, DR // 2, TM), lambda b, s: (b, 0, s)),
    ]
    args = [x, *moe_in, mod_l, n1g, wdq, qng, wuqt, cos_t, sin_t]
    out_specs = [
        pl.BlockSpec((1, TM, D), lambda b, s: (b, s, 0)),
        pl.BlockSpec((1, H, DQK, TM), lambda b, s: (b, 0, 0, s)),
    ]
    out_shape = [
        jax.ShapeDtypeStruct((B, S, D), f32),
        jax.ShapeDtypeStruct((B, H, DQK, S), bf16),
    ]
    if with_kv:
        in_specs += [_full((1, D)), _full((D, KVR)), _full((DR, D)), _full((1, KVR)),
                     _full((KVR, H * DN)), _full((H * DV, KVR))]
        args += list(kv)
        out_specs += [
            pl.BlockSpec((1, H, TM, DQK), lambda b, s: (b, 0, s, 0)),
            pl.BlockSpec((1, H, DV, TM), lambda b, s: (b, 0, 0, s)),
        ]
        out_shape += [
            jax.ShapeDtypeStruct((B, H, S, DQK), bf16),
            jax.ShapeDtypeStruct((B, H, DV, S), bf16),
        ]
    return pl.pallas_call(
        functools.partial(_preattn_kernel, with_kv),
        grid=(B, NT),
        in_specs=in_specs,
        out_specs=out_specs,
        out_shape=out_shape,
        compiler_params=_cp(2),
        name="preattn_kv" if with_kv else "preattn",
    )(*args)


def _attn_kernel(qt_ref, kc_ref, vt_ref, o_ref, m_ref, l_ref, acc_ref, s_ref):
    assert TQ == TK
    pairs = [(qi, j) for qi in range(S // TQ) for j in range(qi + 1)]

    def scores(qi, j, slot):
        q = qt_ref[0, 0, :, qi * TQ:(qi + 1) * TQ]
        k = kc_ref[0, 0, j * TK:(j + 1) * TK, :]
        s_ref[slot] = jnp.dot(k, q, preferred_element_type=f32)

    def softmax_pv(qi, j, slot):
        s = s_ref[slot]
        if j == qi:
            keep = (lax.broadcasted_iota(i32, (TK, TQ), 0)
                    <= lax.broadcasted_iota(i32, (TK, TQ), 1))
            s = jnp.where(keep, s, NEG)
        v = vt_ref[0, 0, :, j * TK:(j + 1) * TK]
        if j == 0:
            m_new = jnp.max(s, axis=0, keepdims=True)
            p = jnp.exp2(s - m_new)
            l_ref[...] = jnp.sum(p, axis=0, keepdims=True)
            acc_ref[...] = jnp.dot(v, p.astype(bf16), preferred_element_type=f32)
        else:
            m_old = m_ref[...]
            m_new = jnp.maximum(m_old, jnp.max(s, axis=0, keepdims=True))
            alpha = jnp.exp2(m_old - m_new)
            p = jnp.exp2(s - m_new)
            l_ref[...] = alpha * l_ref[...] + jnp.sum(p, axis=0, keepdims=True)
            acc_ref[...] = alpha * acc_ref[...] + jnp.dot(
                v, p.astype(bf16), preferred_element_type=f32)
        m_ref[...] = m_new
        if j == qi:
            o = acc_ref[...] / l_ref[...]
            o_ref[0, qi * TQ:(qi + 1) * TQ, :] = o.T.astype(bf16)

    scores(*pairs[0], 0)
    for n, (qi, j) in enumerate(pairs):
        if n + 1 < len(pairs):
            scores(*pairs[n + 1], (n + 1) % 2)
        softmax_pv(qi, j, n % 2)


def _attention(qt, kc, vt):
    return pl.pallas_call(
        _attn_kernel,
        grid=(B, H),
        in_specs=[
            pl.BlockSpec((1, 1, DQK, S), lambda b, h: (b, h, 0, 0)),
            pl.BlockSpec((1, 1, S, DQK), lambda b, h: (b, h, 0, 0)),
            pl.BlockSpec((1, 1, DV, S), lambda b, h: (b, h, 0, 0)),
        ],
        out_specs=pl.BlockSpec((1, S, DV), lambda b, h: (b, 0, h)),
        out_shape=jax.ShapeDtypeStruct((B, S, H * DV), bf16),
        scratch_shapes=[pltpu.VMEM((1, TQ), f32), pltpu.VMEM((1, TQ), f32),
                        pltpu.VMEM((DV, TQ), f32), pltpu.VMEM((2, TK, TQ), f32)],
        compiler_params=_cp(2),
        name="attn",
    )(qt, kc, vt)


def _postattn_kernel(x_ref, o_ref, mod_ref, wo_ref, n2g_ref, rw_ref, rb_ref,
                     xo_ref, h2p_ref, eidx_ref, ewt_ref):
    g1 = mod_ref[0, 2:3, :]
    mix = jnp.dot(o_ref[0], wo_ref[...], preferred_element_type=f32)
    x_new = x_ref[0] + g1 * mix
    xo_ref[0] = x_new
    _route_and_pack(x_new, mod_ref, n2g_ref, rw_ref, rb_ref, h2p_ref, eidx_ref, ewt_ref)


def _postattn(x, o, mod_l, wo, n2g, rw, rb):
    out_specs, out_shape = _tok_specs()
    return pl.pallas_call(
        _postattn_kernel,
        grid=(B, NT),
        in_specs=[
            pl.BlockSpec((1, TM, D), lambda b, s: (b, s, 0)),
            pl.BlockSpec((1, TM, H * DV), lambda b, s: (b, s, 0)),
            pl.BlockSpec((1, 6, D), lambda b, s: (b, 0, 0)),
            _full((H * DV, D)), _full((1, D)), _full((2 * E, D)), _full((E, 1)),
        ],
        out_specs=out_specs,
        out_shape=out_shape,
        compiler_params=_cp(2),
        name="postattn",
    )(x, o, mod_l, wo, n2g, rw, rb)


def _row_copy(src_ref, dst_ref, sem, src_row, dst_row):
    return pltpu.make_async_copy(src_ref.at[src_row], dst_ref.at[dst_row], sem)


def _gather_kernel(idx_ref, src_ref, dst_ref, sem):
    def issue(g, carry):
        r0 = g * DU
        for u in range(DU):
            _row_copy(src_ref, dst_ref, sem, idx_ref[r0 + u], r0 + u).start(priority=u % 2)
        return carry

    lax.fori_loop(0, GR // DU, issue, 0)

    def drain(r, carry):
        _row_copy(src_ref, dst_ref, sem, 0, r).wait()
        return carry

    lax.fori_loop(0, GR, drain, 0, unroll=8)


def _gather_rows(src, idx):
    n = idx.shape[0]
    return pl.pallas_call(
        _gather_kernel,
        grid=(n // GR,),
        in_specs=[
            pl.BlockSpec((GR,), lambda i: (i,), memory_space=pltpu.SMEM),
            pl.BlockSpec(memory_space=pl.ANY),
        ],
        out_specs=pl.BlockSpec((GR, PG, 128), lambda i: (i, 0, 0)),
        out_shape=jax.ShapeDtypeStruct((n, PG, 128), src.dtype),
        scratch_shapes=[pltpu.SemaphoreType.DMA],
        compiler_params=_cp(1),
        name="gather_rows",
    )(idx, src)


def _expert_kernel(be_ref, first_ref, nv_ref, xb_ref, wg_ref, wu_ref, wd_ref, yb_ref,
                   wgu_bf, wd_bf):
    i = pl.program_id(0)
    nv = nv_ref[i]

    @pl.when(first_ref[i] == 1)
    def _():
        wgu_bf[:, :DE] = wg_ref[0, 0].astype(bf16)
        wgu_bf[:, DE:] = wu_ref[0, 0].astype(bf16)
        wd_bf[...] = wd_ref[0, 0].astype(bf16)

    @pl.when(nv > 0)
    def _():
        x = _unpack_rows(_load_rows(xb_ref, RB)).astype(bf16)
        gu = jnp.dot(x, wgu_bf[...], preferred_element_type=f32)
        g = gu[:, :DE]
        hmid = (g * _sigmoid(g) * gu[:, DE:]).astype(bf16)
        y = jnp.dot(hmid, wd_bf[...], preferred_element_type=f32)
        _store_rows(yb_ref, _pack_pair(y[:, :HP], y[:, HP:]), RB)

    @pl.when(nv == 0)
    def _():
        yb_ref[...] = jnp.zeros((RB * PG, 128), u32)


def _experts(layer, blk_e, blk_first, blk_nv, xb, wg, wu, wd):
    grid_spec = pltpu.PrefetchScalarGridSpec(
        num_scalar_prefetch=3,
        grid=(NB,),
        in_specs=[
            pl.BlockSpec((RB * PG, 128), lambda i, be, bf, nv: (i, 0)),
            pl.BlockSpec((1, 1, D, DE), lambda i, be, bf, nv: (layer, be[i], 0, 0)),
            pl.BlockSpec((1, 1, D, DE), lambda i, be, bf, nv: (layer, be[i], 0, 0)),
            pl.BlockSpec((1, 1, DE, D), lambda i, be, bf, nv: (layer, be[i], 0, 0)),
        ],
        out_specs=pl.BlockSpec((RB * PG, 128), lambda i, be, bf, nv: (i, 0)),
        scratch_shapes=[pltpu.VMEM((D, 2 * DE), bf16), pltpu.VMEM((DE, D), bf16)],
    )
    return pl.pallas_call(
        _expert_kernel,
        grid_spec=grid_spec,
        out_shape=jax.ShapeDtypeStruct((N_ROWS * PG, 128), u32),
        compiler_params=_cp(1),
        name="experts",
    )(blk_e, blk_first, blk_nv, xb, wg, wu, wd)


PCH = 256
NBP = 256


def _route_pos_kernel(eidx_ref, dest_ref, blk_ref, pre_ref):
    upper = (lax.broadcasted_iota(i32, (PCH, PCH), 0)
             < lax.broadcasted_iota(i32, (PCH, PCH), 1)).astype(bf16)
    eio = lax.broadcasted_iota(i32, (E, PCH), 0)

    def onehot(c0, width, eiota):
        e0 = eidx_ref[0:1, pl.ds(c0, width)]
        e1 = eidx_ref[1:2, pl.ds(c0, width)]
        return jnp.concatenate([e0 == eiota, e1 == eiota], axis=0)

    ones_r = jnp.ones((8, PCH), bf16)

    def count(c, carry):
        col, row = carry
        c0 = pl.multiple_of(c * PCH, PCH)
        oh = onehot(c0, PCH, eio).astype(bf16)
        pre_ref[:, pl.ds(c0, PCH)] = col + jnp.dot(oh, upper, preferred_element_type=f32)
        col = col + jnp.sum(oh.astype(f32), axis=1, keepdims=True)
        row = row + _dot_nt(ones_r, oh[:E]) + _dot_nt(ones_r, oh[E:])
        return col, row

    tot_c, tot_r = lax.fori_loop(
        0, T // PCH, count, (jnp.zeros((2 * E, 1), f32), jnp.zeros((8, E), f32)))
    cnt0_c = tot_c[:E]
    cnt_c = tot_c[:E] + tot_c[E:]
    shift = RB.bit_length() - 1
    assert RB == 1 << shift
    padded_r = lax.shift_left(
        lax.shift_right_logical(tot_r[0:1].astype(i32) + (RB - 1), shift), shift).astype(f32)
    ee = lax.broadcasted_iota(i32, (E, E), 0)
    ep = lax.broadcasted_iota(i32, (E, E), 1)
    start_col = jnp.sum(jnp.where(ep < ee, padded_r, 0.0), axis=1, keepdims=True)
    end_col = jnp.sum(jnp.where(ep <= ee, padded_r, 0.0), axis=1, keepdims=True)
    vend_col = start_col + cnt_c
    basef = jnp.concatenate([start_col, start_col + cnt0_c], axis=0)

    wide = 2048
    eiow = lax.broadcasted_iota(i32, (E, wide), 0)

    def place(c, carry):
        c0 = pl.multiple_of(c * wide, wide)
        oh = onehot(c0, wide, eiow)
        val = jnp.where(oh, pre_ref[:, pl.ds(c0, wide)] + basef, 0.0)
        d0 = jnp.sum(val[:E], axis=0, keepdims=True)
        d1 = jnp.sum(val[E:], axis=0, keepdims=True)
        dest_ref[0:1, pl.ds(c0, wide)] = d0.astype(i32)
        dest_ref[1:2, pl.ds(c0, wide)] = d1.astype(i32)
        return carry

    lax.fori_loop(0, T // wide, place, 0)

    start_b = (lax.broadcasted_iota(i32, (1, NBP), 1) * RB).astype(f32)
    be = jnp.minimum(jnp.sum((end_col <= start_b).astype(f32), axis=0, keepdims=True), E - 1.0)
    bprev = jnp.minimum(
        jnp.sum((end_col <= start_b - RB).astype(f32), axis=0, keepdims=True), E - 1.0)
    first = jnp.logical_or(start_b == 0.0, be != bprev)
    eion = lax.broadcasted_iota(i32, (E, NBP), 0).astype(f32)
    vend_b = jnp.sum(jnp.where(eion == be, vend_col, 0.0), axis=0, keepdims=True)
    nv = jnp.clip(vend_b - start_b, 0.0, float(RB))
    blk_ref[...] = jnp.zeros((8, NBP), i32)
    blk_ref[0:1, :] = be.astype(i32)
    blk_ref[1:2, :] = first.astype(i32)
    blk_ref[2:3, :] = nv.astype(i32)


def _route_pos(eidx):
    return pl.pallas_call(
        _route_pos_kernel,
        in_specs=[pl.BlockSpec(memory_space=pltpu.VMEM)],
        out_specs=[pl.BlockSpec(memory_space=pltpu.VMEM)] * 2,
        out_shape=[jax.ShapeDtypeStruct((TOPK, T), i32), jax.ShapeDtypeStruct((8, NBP), i32)],
        scratch_shapes=[pltpu.VMEM((2 * E, T), f32)],
        compiler_params=pltpu.CompilerParams(vmem_limit_bytes=VMEM_LIMIT),
        name="route_pos",
    )(eidx)


TD = 2048


def _dispatch_kernel(nv_ref, d0_ref, d1_ref, h2p_ref, xb_ref, zbuf, sem):
    @pl.when(pl.program_id(0) == 0)
    def _():
        zbuf[...] = jnp.zeros((RB, PG, 128), u32)

        def zero_block(b):
            return pltpu.make_async_copy(zbuf, xb_ref.at[pl.ds(b * RB, RB)], sem)

        for b in range(NB):
            @pl.when(nv_ref[b] < RB)
            def _():
                zero_block(b).start()

        for b in range(NB):
            @pl.when(nv_ref[b] < RB)
            def _():
                zero_block(b).wait()

    def issue(g, carry):
        r0 = g * DU
        for u in range(DU):
            _row_copy(h2p_ref, xb_ref, sem, r0 + u, d0_ref[r0 + u]).start(priority=0)
            _row_copy(h2p_ref, xb_ref, sem, r0 + u, d1_ref[r0 + u]).start(priority=1)
        return carry

    lax.fori_loop(0, TD // DU, issue, 0)

    def drain(r, carry):
        _row_copy(h2p_ref, xb_ref, sem, 0, 0).wait()
        _row_copy(h2p_ref, xb_ref, sem, 0, 0).wait()
        return carry

    lax.fori_loop(0, TD, drain, 0, unroll=8)


def _dispatch(h2p, d0, d1, blk_nv):
    return pl.pallas_call(
        _dispatch_kernel,
        grid=(T // TD,),
        in_specs=[
            pl.BlockSpec(memory_space=pltpu.SMEM),
            pl.BlockSpec((TD,), lambda i: (i,), memory_space=pltpu.SMEM),
            pl.BlockSpec((TD,), lambda i: (i,), memory_space=pltpu.SMEM),
            pl.BlockSpec((TD, PG, 128), lambda i: (i, 0, 0)),
        ],
        out_specs=pl.BlockSpec(memory_space=pl.ANY),
        out_shape=jax.ShapeDtypeStruct((N_ROWS, PG, 128), u32),
        scratch_shapes=[pltpu.VMEM((RB, PG, 128), u32), pltpu.SemaphoreType.DMA],
        compiler_params=_cp(1),
        name="dispatch",
    )(blk_nv, d0, d1, h2p)


def _final_kernel(x_ref, yg0_ref, yg1_ref, ewp_ref, modp_ref, g_ref, o_ref):
    o_ref[0] = _rms(_moe_combine(x_ref, yg0_ref, yg1_ref, ewp_ref, modp_ref), g_ref[...])


def _final(x, moe_in, g):
    return pl.pallas_call(
        _final_kernel,
        grid=(B, NT),
        in_specs=[pl.BlockSpec((1, TM, D), lambda b, s: (b, s, 0))] + _moe_in_specs()
        + [_full((1, D))],
        out_specs=pl.BlockSpec((1, TM, D), lambda b, s: (b, s, 0)),
        out_shape=jax.ShapeDtypeStruct((B, S, D), f32),
        compiler_params=_cp(2),
        name="final_norm",
    )(x, *moe_in, g)


def _moe(layer, h2p, eidx, wg, wu, wd):
    dest, blk = _route_pos(eidx)
    blk_e, blk_first, blk_nv = blk[0, :NB], blk[1, :NB], blk[2, :NB]
    xb = _dispatch(h2p.reshape(T, PG, 128), dest[0], dest[1], blk_nv)
    yb = _experts(layer, blk_e, blk_first, blk_nv, xb.reshape(N_ROWS * PG, 128), wg, wu, wd)
    yg = _gather_rows(yb.reshape(N_ROWS, PG, 128), dest.reshape(-1))
    return yg.reshape(TOPK * T * PG, 128)


def kernel(x, c, positions, ada_w, ada_b, norm1_g, norm2_g, conv_w_in, conv_b_in, conv_w_dw,
           conv_b_dw, conv_ln_g, conv_ln_b, conv_w_out, kv_in_g, w_dkv, kv_norm_g, w_ukv, w_dq,
           q_norm_g, w_uq, w_o, router_w, router_b, exp_w_gate, exp_w_up, exp_w_down, final_g):
    c8 = jnp.pad(c, ((0, 8 - B), (0, 0)))
    mod = _ada(c8, ada_w, ada_b[:, None, :])[:, :B].reshape(DEPTH, B, 6, D)

    invf = (ROPE_THETA ** (-jnp.arange(0, DR, 2, dtype=f32) / DR))[:, None]
    cos_t, sin_t = _rope_tables(positions[:, None, :], invf)

    rwt = router_w.astype(f32).T
    rw_hi = rwt.astype(bf16)
    rw_lo = (rwt - rw_hi.astype(f32)).astype(bf16)
    rw = jnp.concatenate([rw_hi, rw_lo], axis=0)
    rb = router_b.astype(f32)[:, None]

    w_ukv3 = w_ukv.reshape(KVR, H, DN + DV)
    wuk = w_ukv3[:, :, :DN].reshape(KVR, H * DN).astype(bf16)
    wuvt = w_ukv3[:, :, DN:].reshape(KVR, H * DV).T.astype(bf16)
    kv_w = (kv_in_g[None, :], w_dkv[:, :KVR].astype(bf16), w_dkv[:, KVR:].T.astype(bf16),
            kv_norm_g[None, :], wuk, wuvt)

    moe_in = None
    kc = vt = None
    for l in range(DEPTH):
        if l < N_A:
            wdw = jnp.pad(conv_w_dw[l], ((0, HALO - CONV_W), (0, 0))).reshape(HALO, CG, 128)
            x, h2p, eidx, ew = _conv_layer(
                x, moe_in, mod[l], norm1_g[l][None, :], conv_w_in[l].astype(bf16),
                conv_b_in[l][None, :], wdw, conv_b_dw[l].reshape(CG, 128), conv_ln_g[l][None, :],
                conv_ln_b[l][None, :], conv_w_out[l].astype(bf16), norm2_g[l][None, :], rw, rb)
        else:
            j = l - N_A
            outs = _preattn(
                x, moe_in, mod[l], norm1_g[l][None, :], w_dq[j].astype(bf16),
                q_norm_g[j][None, :], w_uq[j].T.astype(bf16), cos_t, sin_t,
                kv=kv_w if j == 0 else None)
            if j == 0:
                x, qt, kc, vt = outs
            else:
                x, qt = outs
            o = _attention(qt, kc, vt)
            x, h2p, eidx, ew = _postattn(
                x, o, mod[l], w_o[j].astype(bf16), norm2_g[l][None, :], rw, rb)
        yg = _moe(l, h2p, eidx, exp_w_gate, exp_w_up, exp_w_down)
        moe_in = (yg, yg, ew, mod[l])
    return _final(x, moe_in, final_g[None, :])
```
